```python
import jax, jax.numpy as jnp
from jax import lax
import numpy as np

D_MODEL = 1024
BATCH = 2
SEQ = 8192
DEPTH = 2
DEC_BATCH = 128
DEC_SEQ = 8
PAST_LEN = 8192
PAGE_SIZE = 128

N_MIXERS = 2
N_SWA_LAYERS = (DEPTH + N_MIXERS - 1) // N_MIXERS
N_MOBA_LAYERS = DEPTH // N_MIXERS
N_HEADS = 16
N_KV_HEADS = 4
HEAD_DIM = 64
GROUP = N_HEADS // N_KV_HEADS
Q_DIM = N_HEADS * HEAD_DIM
KV_DIM = N_KV_HEADS * HEAD_DIM
QKV_DIM = Q_DIM + 2 * KV_DIM
ROPE_THETA = 10000.0
WINDOW = 128
SWA_BLOCK = WINDOW
MOBA_BLOCK = 256
MOBA_TOPK = 3
MOBA_Q_BLOCK = 64
N_GROUPS = 8
EXPERTS_PER_GROUP = 8
N_EXPERTS = N_GROUPS * EXPERTS_PER_GROUP
EXPERT_TOPK = 2
EXPERT_FF = D_MODEL // 4
EXPERT_ROW_BLOCK = 128
RMS_EPS = 1e-6
NEG_INF = -1e30

kernel_name = 'hybrid_swa_sink_moba_hier_moe_step'


def rmsnorm(x, g):
    x32 = x.astype(jnp.float32)
    y = x32 * lax.rsqrt(jnp.mean(x32 * x32, axis=-1, keepdims=True) + RMS_EPS)
    return (y * g.astype(jnp.float32)).astype(x.dtype)


def rope(x, pos):
    half = HEAD_DIM // 2
    inv = ROPE_THETA ** (-jnp.arange(half, dtype=jnp.float32) / half)
    ang = pos.astype(jnp.float32)[:, None] * inv[None, :]
    cos = jnp.cos(ang)[:, None, :].astype(x.dtype)
    sin = jnp.sin(ang)[:, None, :].astype(x.dtype)
    x1, x2 = x[..., :half], x[..., half:]
    return jnp.concatenate([x1 * cos - x2 * sin, x2 * cos + x1 * sin], axis=-1)


def split_heads(qkv, pos):
    b, s, _ = qkv.shape
    q = qkv[..., :Q_DIM].reshape(b, s, N_HEADS, HEAD_DIM)
    k = qkv[..., Q_DIM:Q_DIM + KV_DIM].reshape(b, s, N_KV_HEADS, HEAD_DIM)
    v = qkv[..., Q_DIM + KV_DIM:].reshape(b, s, N_KV_HEADS, HEAD_DIM)
    return rope(q, pos), rope(k, pos), v


def sink_attention(q, k, v, q_pos, k_pos, sinks):
    scale = HEAD_DIM ** -0.5
    s = jnp.einsum('bnqkgd,bnlkd->bnkgql', q, k).astype(jnp.float32) * scale
    d = q_pos[:, :, None] - k_pos[:, None, :]
    ok = (d >= 0) & (d < WINDOW) & (k_pos[:, None, :] >= 0)
    s = jnp.where(ok[None, :, None, None], s, NEG_INF)
    sink = jnp.broadcast_to(sinks.astype(jnp.float32).reshape(1, 1, N_KV_HEADS, GROUP, 1, 1),
                            s.shape[:-1] + (1,))
    p = jax.nn.softmax(jnp.concatenate([s, sink], axis=-1), axis=-1)[..., :-1]
    return jnp.einsum('bnkgql,bnlkd->bnqkgd', p.astype(v.dtype), v)


def swa_prompt(h, w_qkv, b_qkv, sinks, w_o, b_o):
    b, s, _ = h.shape
    pos = jnp.arange(s)
    q, k, v = split_heads(h @ w_qkv + b_qkv, pos)
    nc = s // SWA_BLOCK
    qb = q.reshape(b, nc, SWA_BLOCK, N_KV_HEADS, GROUP, HEAD_DIM)
    kb = k.reshape(b, nc, SWA_BLOCK, N_KV_HEADS, HEAD_DIM)
    vb = v.reshape(b, nc, SWA_BLOCK, N_KV_HEADS, HEAD_DIM)
    shift = lambda t: jnp.concatenate([jnp.zeros_like(t[:, :1]), t[:, :-1]], axis=1)
    k_band = jnp.concatenate([shift(kb), kb], axis=2)
    v_band = jnp.concatenate([shift(vb), vb], axis=2)
    q_pos = pos.reshape(nc, SWA_BLOCK)
    k_pos = jnp.concatenate([q_pos - SWA_BLOCK, q_pos], axis=1)
    o = sink_attention(qb, k_band, v_band, q_pos, k_pos, sinks)
    y = o.reshape(b, s, Q_DIM) @ w_o + b_o
    win = min(WINDOW, s)
    return y, k[:, s - win:], v[:, s - win:]


def swa_sample(h, cache_k, cache_v, w_qkv, b_qkv, sinks, w_o, b_o):
    db, nq, _ = h.shape
    win_buf = cache_k.shape[1]
    pos = PAST_LEN + jnp.arange(nq)
    q, k, v = split_heads(h @ w_qkv + b_qkv, pos)
    kk = jnp.concatenate([cache_k.astype(k.dtype), k], axis=1)
    vv = jnp.concatenate([cache_v.astype(v.dtype), v], axis=1)
    k_pos = jnp.concatenate([PAST_LEN - win_buf + jnp.arange(win_buf), pos])
    o = sink_attention(q.reshape(db, 1, nq, N_KV_HEADS, GROUP, HEAD_DIM), kk[:, None], vv[:, None],
                       pos[None], k_pos[None], sinks)
    y = o.reshape(db, nq, Q_DIM) @ w_o + b_o
    return y, kk[:, -win_buf:], vv[:, -win_buf:]


def moba_select(q, kmean, own_blk, n_sel):
    if n_sel == 0:
        return None, None
    bt, nq = q.shape[:2]
    nb = kmean.shape[1]
    qg = q.astype(jnp.float32).reshape(bt, nq, N_KV_HEADS, GROUP, HEAD_DIM)
    gate = jnp.einsum('bqkgd,bnkd->bqkgn', qg, kmean).reshape(bt, nq, N_HEADS, nb)
    gate = jnp.where(jnp.arange(nb) < own_blk, gate, NEG_INF)
    _, idx = lax.top_k(gate, n_sel)
    return idx, idx < own_blk


def moba_core(q, q_pos, idx, valid, n_sel, fetch_k, fetch_v, own_k, own_v, own_pos):
    bt, nq = q.shape[:2]
    n_own = own_k.shape[1]
    scale = HEAD_DIM ** -0.5
    logits = []
    for s in range(n_sel):
        sc = jnp.einsum('bqhd,bqhjd->bqhj', q, fetch_k(idx[..., s])).astype(jnp.float32) * scale
        logits.append(jnp.where(valid[..., s, None], sc, NEG_INF))
    qg = q.reshape(bt, nq, N_KV_HEADS, GROUP, HEAD_DIM)
    so = jnp.einsum('bqkgd,blkd->bqkgl', qg, own_k).astype(jnp.float32).reshape(bt, nq, N_HEADS, n_own) * scale
    causal = own_pos[None, :] <= q_pos[:, None]
    logits.append(jnp.where(causal[None, :, None, :], so, NEG_INF))
    p = jax.nn.softmax(jnp.concatenate(logits, axis=-1), axis=-1).astype(q.dtype)
    off = n_sel * MOBA_BLOCK
    p_own = p[..., off:].reshape(bt, nq, N_KV_HEADS, GROUP, n_own)
    out = jnp.einsum('bqkgl,blkd->bqkgd', p_own, own_v).reshape(bt, nq, N_HEADS, HEAD_DIM)
    for s in range(n_sel):
        out = out + jnp.einsum('bqhj,bqhjd->bqhd', p[..., s * MOBA_BLOCK:(s + 1) * MOBA_BLOCK],
                               fetch_v(idx[..., s]))
    return out


def moba_prompt(h, w_qkv, w_o):
    b, s, _ = h.shape
    pos = jnp.arange(s)
    q, k, v = split_heads(h @ w_qkv, pos)
    nb = -(-s // MOBA_BLOCK)
    pad = ((0, 0), (0, nb * MOBA_BLOCK - s), (0, 0), (0, 0))
    kp = jnp.pad(k, pad).reshape(b, nb, MOBA_BLOCK, N_KV_HEADS, HEAD_DIM)
    vp = jnp.pad(v, pad).reshape(b, nb, MOBA_BLOCK, N_KV_HEADS, HEAD_DIM)
    kmean = kp.astype(jnp.float32).mean(axis=2)
    n_sel = min(MOBA_TOPK, nb - 1)
    b_ar = jnp.arange(b)[:, None, None]
    hk = (jnp.arange(N_HEADS) // GROUP)[None, None, :]
    fetch_k = lambda blk: kp[b_ar, blk, :, hk, :]
    fetch_v = lambda blk: vp[b_ar, blk, :, hk, :]
    nc = s // MOBA_Q_BLOCK
    qc = q.reshape(b, nc, MOBA_Q_BLOCK, N_HEADS, HEAD_DIM).swapaxes(0, 1)

    def chunk(args):
        c, q_c = args
        q_pos = c * MOBA_Q_BLOCK + jnp.arange(MOBA_Q_BLOCK)
        own_blk = (c * MOBA_Q_BLOCK) // MOBA_BLOCK
        own_k = lax.dynamic_index_in_dim(kp, own_blk, axis=1, keepdims=False)
        own_v = lax.dynamic_index_in_dim(vp, own_blk, axis=1, keepdims=False)
        own_pos = own_blk * MOBA_BLOCK + jnp.arange(MOBA_BLOCK)
        idx, valid = moba_select(q_c, kmean, own_blk, n_sel)
        return moba_core(q_c, q_pos, idx, valid, n_sel, fetch_k, fetch_v, own_k, own_v, own_pos)

    o = lax.map(chunk, (jnp.arange(nc), qc))
    y = o.swapaxes(0, 1).reshape(b, s, Q_DIM) @ w_o
    return y, k, v


def moba_sample(h, pool_k, pool_v, page_table, w_qkv, w_o):
    db, nq, _ = h.shape
    pos = PAST_LEN + jnp.arange(nq)
    q, k, v = split_heads(h @ w_qkv, pos)
    ppb = MOBA_BLOCK // PAGE_SIZE
    n_full = PAST_LEN // MOBA_BLOCK
    own_past = PAST_LEN - n_full * MOBA_BLOCK
    page_mean = pool_k.astype(jnp.float32).mean(axis=1)
    kmean = page_mean[page_table[:, :n_full * ppb]].reshape(
        db, n_full, ppb, N_KV_HEADS, HEAD_DIM).mean(axis=2)
    n_sel = min(MOBA_TOPK, n_full)
    b_ar = jnp.arange(db)[:, None, None, None]
    hk = (jnp.arange(N_HEADS) // GROUP)[None, None, :, None]
    pg = jnp.arange(ppb)

    def fetch(pool, blk):
        phys = page_table[b_ar, blk[..., None] * ppb + pg]
        return pool[phys, :, hk, :].reshape(db, nq, N_HEADS, MOBA_BLOCK, HEAD_DIM).astype(q.dtype)

    own_pages = page_table[:, n_full * ppb:]
    own_k = jnp.concatenate([pool_k[own_pages].reshape(db, own_past, N_KV_HEADS, HEAD_DIM).astype(k.dtype), k], axis=1)
    own_v = jnp.concatenate([pool_v[own_pages].reshape(db, own_past, N_KV_HEADS, HEAD_DIM).astype(v.dtype), v], axis=1)
    own_pos = n_full * MOBA_BLOCK + jnp.arange(own_past + nq)
    idx, valid = moba_select(q, kmean, n_full, n_sel)
    o = moba_core(q, pos, idx, valid, n_sel, lambda blk: fetch(pool_k, blk),
                  lambda blk: fetch(pool_v, blk), own_k, own_v, own_pos)
    return o.reshape(db, nq, Q_DIM) @ w_o, k, v


def grouped_expert_ffn(x, eidx, w_gate, w_up, w_down):
    t, kk = eidx.shape
    a = t * kk
    flat_e = eidx.reshape(a)
    order = jnp.argsort(flat_e)
    sorted_e = flat_e[order]
    counts = jnp.bincount(flat_e, length=N_EXPERTS)
    padded = (counts + EXPERT_ROW_BLOCK - 1) // EXPERT_ROW_BLOCK * EXPERT_ROW_BLOCK
    pad_end = jnp.cumsum(padded)
    start = jnp.cumsum(counts) - counts
    dest = (pad_end - padded)[sorted_e] + jnp.arange(a) - start[sorted_e]
    n_blk = -(-a // EXPERT_ROW_BLOCK) + N_EXPERTS
    rows = jnp.zeros((n_blk * EXPERT_ROW_BLOCK, x.shape[1]), x.dtype).at[dest].set(x[order // kk])
    blk_e = jnp.minimum(jnp.searchsorted(pad_end, jnp.arange(n_blk) * EXPERT_ROW_BLOCK, side='right'),
                        N_EXPERTS - 1)

    def expert_block(args):
        xb, e = args
        hid = jax.nn.silu(xb @ w_gate[e]) * (xb @ w_up[e])
        return hid @ w_down[e]

    out = lax.map(expert_block, (rows.reshape(n_blk, EXPERT_ROW_BLOCK, -1), blk_e))
    y_sorted = out.reshape(n_blk * EXPERT_ROW_BLOCK, -1)[dest]
    y = jnp.zeros((a, x.shape[1]), x.dtype).at[order].set(y_sorted)
    return y.reshape(t, kk, x.shape[1])


def hier_moe(h, w_gr, b_gr, w_er, b_er, w_gate, w_up, w_down):
    shp = h.shape
    x = h.reshape(-1, shp[-1])
    t = x.shape[0]
    g_logits = (x @ w_gr + b_gr).astype(jnp.float32)
    g_prob = jax.nn.softmax(g_logits, axis=-1)
    _, grp = lax.top_k(g_logits, 1)
    g_w = jnp.take_along_axis(g_prob, grp, axis=1)[:, 0]
    e_logits = (x @ w_er + b_er).astype(jnp.float32).reshape(t, N_GROUPS, EXPERTS_PER_GROUP)
    e_logits = jnp.take_along_axis(e_logits, grp[:, :, None], axis=1)[:, 0]
    top_v, top_i = lax.top_k(e_logits, EXPERT_TOPK)
    w = (jax.nn.softmax(top_v, axis=-1) * g_w[:, None]).astype(x.dtype)
    eidx = grp * EXPERTS_PER_GROUP + top_i
    y = grouped_expert_ffn(x, eidx, w_gate, w_up, w_down)
    return jnp.einsum('tk,tkd->td', w, y).reshape(shp)


def setup_inputs(seed: int = 0) -> dict:
    key = jax.random.key(seed)
    ks = jax.random.split(key, 24)
    n_pages = PAST_LEN // PAGE_SIZE
    n_pool = (DEC_BATCH * n_pages * 5 + 3) // 4
    win_buf = min(WINDOW, PAST_LEN)
    f32 = jnp.float32
    nrm = lambda k, shape, s: s * jax.random.normal(k, shape, f32)
    page_table = jax.random.permutation(ks[6], n_pool)[:DEC_BATCH * n_pages].reshape(
        DEC_BATCH, n_pages).astype(jnp.int32)
    return {
        'x_prompt': nrm(ks[0], (BATCH, SEQ, D_MODEL), 1.0),
        'x_sample': nrm(ks[1], (DEC_BATCH, DEC_SEQ, D_MODEL), 1.0),
        'cache_swa_k': nrm(ks[2], (N_SWA_LAYERS, DEC_BATCH, win_buf, N_KV_HEADS, HEAD_DIM), 1.0),
        'cache_swa_v': nrm(ks[3], (N_SWA_LAYERS, DEC_BATCH, win_buf, N_KV_HEADS, HEAD_DIM), 1.0),
        'cache_moba_k': nrm(ks[4], (N_MOBA_LAYERS, n_pool, PAGE_SIZE, N_KV_HEADS, HEAD_DIM), 1.0),
        'cache_moba_v': nrm(ks[5], (N_MOBA_LAYERS, n_pool, PAGE_SIZE, N_KV_HEADS, HEAD_DIM), 1.0),
        'page_table': page_table,
        'norm_mix': 1.0 + nrm(ks[7], (DEPTH, D_MODEL), 0.05),
        'norm_ffn': 1.0 + nrm(ks[8], (DEPTH, D_MODEL), 0.05),
        'norm_final': 1.0 + nrm(ks[9], (D_MODEL,), 0.05),
        'swa_w_qkv': nrm(ks[10], (N_SWA_LAYERS, D_MODEL, QKV_DIM), D_MODEL ** -0.5),
        'swa_b_qkv': nrm(ks[11], (N_SWA_LAYERS, QKV_DIM), 0.02),
        'swa_sinks': nrm(ks[12], (N_SWA_LAYERS, N_HEADS), 1.0),
        'swa_w_o': nrm(ks[13], (N_SWA_LAYERS, Q_DIM, D_MODEL), Q_DIM ** -0.5),
        'swa_b_o': nrm(ks[14], (N_SWA_LAYERS, D_MODEL), 0.02),
        'moba_w_qkv': nrm(ks[15], (N_MOBA_LAYERS, D_MODEL, QKV_DIM), D_MODEL ** -0.5),
        'moba_w_o': nrm(ks[16], (N_MOBA_LAYERS, Q_DIM, D_MODEL), Q_DIM ** -0.5),
        'w_group_router': nrm(ks[17], (DEPTH, D_MODEL, N_GROUPS), D_MODEL ** -0.5),
        'b_group_router': nrm(ks[18], (DEPTH, N_GROUPS), 0.01),
        'w_expert_router': nrm(ks[19], (DEPTH, D_MODEL, N_EXPERTS), D_MODEL ** -0.5),
        'b_expert_router': nrm(ks[20], (DEPTH, N_EXPERTS), 0.01),
        'w_gate': nrm(ks[21], (DEPTH, N_EXPERTS, D_MODEL, EXPERT_FF), D_MODEL ** -0.5),
        'w_up': nrm(ks[22], (DEPTH, N_EXPERTS, D_MODEL, EXPERT_FF), D_MODEL ** -0.5),
        'w_down': nrm(ks[23], (DEPTH, N_EXPERTS, EXPERT_FF, D_MODEL), EXPERT_FF ** -0.5),
    }


def reference(x_prompt, x_sample, cache_swa_k, cache_swa_v, cache_moba_k, cache_moba_v, page_table,
              norm_mix, norm_ffn, norm_final, swa_w_qkv, swa_b_qkv, swa_sinks, swa_w_o, swa_b_o,
              moba_w_qkv, moba_w_o, w_group_router, b_group_router, w_expert_router, b_expert_router,
              w_gate, w_up, w_down):
    xp, xs = x_prompt, x_sample
    swa_kp, swa_vp, swa_ks, swa_vs = [], [], [], []
    moba_kp, moba_vp, moba_ks, moba_vs = [], [], [], []
    for i in range(DEPTH):
        j = i // N_MIXERS
        hp = rmsnorm(xp, norm_mix[i])
        hs = rmsnorm(xs, norm_mix[i])
        if i % N_MIXERS == 0:
            yp, kp_, vp_ = swa_prompt(hp, swa_w_qkv[j], swa_b_qkv[j], swa_sinks[j], swa_w_o[j], swa_b_o[j])
            ys, ks_, vs_ = swa_sample(hs, cache_swa_k[j], cache_swa_v[j], swa_w_qkv[j], swa_b_qkv[j],
                                      swa_sinks[j], swa_w_o[j], swa_b_o[j])
            swa_kp.append(kp_); swa_vp.append(vp_); swa_ks.append(ks_); swa_vs.append(vs_)
        else:
            yp, kp_, vp_ = moba_prompt(hp, moba_w_qkv[j], moba_w_o[j])
            ys, ks_, vs_ = moba_sample(hs, cache_moba_k[j], cache_moba_v[j], page_table,
                                       moba_w_qkv[j], moba_w_o[j])
            moba_kp.append(kp_); moba_vp.append(vp_); moba_ks.append(ks_); moba_vs.append(vs_)
        xp = xp + yp
        xs = xs + ys
        moe_w = (w_group_router[i], b_group_router[i], w_expert_router[i], b_expert_router[i],
                 w_gate[i], w_up[i], w_down[i])
        xp = xp + hier_moe(rmsnorm(xp, norm_ffn[i]), *moe_w)
        xs = xs + hier_moe(rmsnorm(xs, norm_ffn[i]), *moe_w)
    y_prompt = rmsnorm(xp, norm_final)
    y_sample = rmsnorm(xs, norm_final)
    swa_k_prompt = jnp.stack(swa_kp)
    swa_v_prompt = jnp.stack(swa_vp)
    swa_k_sample = jnp.stack(swa_ks)
    swa_v_sample = jnp.stack(swa_vs)
    moba_k_prompt = jnp.stack(moba_kp)
    moba_v_prompt = jnp.stack(moba_vp)
    moba_k_sample = jnp.stack(moba_ks)
    moba_v_sample = jnp.stack(moba_vs)
    return (y_prompt, y_sample, swa_k_prompt, swa_v_prompt, swa_k_sample, swa_v_sample,
            moba_k_prompt, moba_v_prompt, moba_k_sample, moba_v_sample)
```

```python
import functools

import jax
import jax.numpy as jnp
from jax import lax
from jax.experimental import pallas as pl
from jax.experimental.pallas import tpu as pltpu

D_MODEL = 1024
N_HEADS = 16
N_KV_HEADS = 4
HEAD_DIM = 64
GROUP = N_HEADS // N_KV_HEADS
Q_DIM = N_HEADS * HEAD_DIM
KV_DIM = N_KV_HEADS * HEAD_DIM
QKV_DIM = Q_DIM + 2 * KV_DIM
ROPE_THETA = 10000.0
WINDOW = 128
MOBA_BLOCK = 256
MOBA_TOPK = 3
PAGE_SIZE = 128
N_GROUPS = 8
EXPERTS_PER_GROUP = 8
N_EXPERTS = N_GROUPS * EXPERTS_PER_GROUP
EXPERT_TOPK = 2
EXPERT_FF = D_MODEL // 4
RMS_EPS = 1e-6
NEG_INF = -1e30
BELOW_NEG_INF = -3e38

LANES = 128
GROUP_LANES = GROUP * HEAD_DIM
ROW_TILE = 256
EXPERT_ROWS = 256
PAGES_PER_STEP = 8
VMEM_LIMIT = 56 * 1024 * 1024

F32 = jnp.float32
BF16 = jnp.bfloat16
NT_DIMS = (((1,), (1,)), ((), ()))


def _params(*sem):
    return pltpu.CompilerParams(dimension_semantics=sem, vmem_limit_bytes=VMEM_LIMIT)


def _rms(x, g):
    return x * lax.rsqrt(jnp.mean(x * x, axis=-1, keepdims=True) + RMS_EPS) * g


def _replicate_head(x128, odd):
    swapped = pltpu.roll(x128, HEAD_DIM, axis=1)
    low = lax.broadcasted_iota(jnp.int32, x128.shape, 1) < HEAD_DIM
    return jnp.where(low != odd, x128, swapped)


def _replicate_kv(x256, kvh):
    col = x256[:, (kvh // 2) * LANES:(kvh // 2 + 1) * LANES]
    rep = _replicate_head(col, kvh % 2 == 1)
    return jnp.concatenate([rep, rep], axis=1)


def _expand_heads(q):
    head = lax.shift_right_logical(lax.broadcasted_iota(jnp.int32, q.shape, 1), 6)
    return jnp.concatenate([jnp.where(head == h, q, 0.0) for h in range(GROUP)], axis=0)


def _collapse_heads(o_rep, rows):
    head = lax.shift_right_logical(lax.broadcasted_iota(jnp.int32, (rows, GROUP_LANES), 1), 6)
    out = jnp.zeros((rows, GROUP_LANES), F32)
    for h in range(GROUP):
        out = jnp.where(head == h, o_rep[h * rows:(h + 1) * rows], out)
    return out


def _top3_mask(gate, n_valid):
    lane = lax.broadcasted_iota(jnp.int32, gate.shape, 1)
    lane_f = lane.astype(F32)
    valid = lane < n_valid
    g = jnp.where(valid, gate, NEG_INF)
    sel = jnp.zeros(gate.shape, jnp.bool_)
    for _ in range(MOBA_TOPK):
        mx = jnp.max(g, axis=1, keepdims=True)
        first = jnp.min(jnp.where(g == mx, lane_f, 1e9), axis=1, keepdims=True)
        hit = lane_f == first
        sel = jnp.logical_or(sel, jnp.logical_and(hit, valid))
        g = jnp.where(hit, BELOW_NEG_INF, g)
    return sel


def _qkv_kernel(n_add, with_kmean, *refs):
    refs = list(refs)
    x_ref = refs.pop(0)
    add_refs = [refs.pop(0) for _ in range(n_add)]
    g_ref, w_ref, b_ref, cos_ref, sin_ref = refs[:5]
    outs = refs[5:]
    x = x_ref[...]
    for r in add_refs:
        x = x + r[...]
    if n_add:
        outs.pop(0)[...] = x
    q_ref, k_ref, v_ref, krep_ref, vrep_ref = outs[:5]
    h = _rms(x, g_ref[...]).astype(BF16)
    y = jnp.dot(h, w_ref[...], preferred_element_type=F32) + b_ref[...]
    cos = cos_ref[...]
    sin = sin_ref[...]
    low = lax.bitwise_and(lax.broadcasted_iota(jnp.int32, cos.shape, 1), HEAD_DIM - 1) < HEAD_DIM // 2

    def rope(slab):
        partner = jnp.where(low, pltpu.roll(slab, LANES - HEAD_DIM // 2, axis=1),
                            pltpu.roll(slab, HEAD_DIM // 2, axis=1))
        return slab * cos + partner * sin

    scale = HEAD_DIM ** -0.5
    for c in range(Q_DIM // LANES):
        q_ref[:, c * LANES:(c + 1) * LANES] = rope(y[:, c * LANES:(c + 1) * LANES]) * scale
    k = jnp.concatenate([rope(y[:, Q_DIM + c * LANES:Q_DIM + (c + 1) * LANES])
                         for c in range(KV_DIM // LANES)], axis=1)
    v = y[:, Q_DIM + KV_DIM:]
    k_ref[...] = k
    v_ref[...] = v
    for kvh in range(N_KV_HEADS):
        krep_ref[kvh] = _replicate_kv(k, kvh).astype(BF16)
        vrep_ref[kvh] = _replicate_kv(v, kvh).astype(BF16)
    if with_kmean:
        kmean = jnp.mean(k, axis=0, keepdims=True)
        outs[5][0] = jnp.broadcast_to(kmean, (8, KV_DIM))


def _qkv_rope(x, adds, g, w, b, cos_t, sin_t, with_kmean):
    t = x.shape[0]
    nt = t // ROW_TILE
    row = lambda i: (i, 0)
    fixed = lambda i: (0, 0)
    in_specs = [pl.BlockSpec((ROW_TILE, D_MODEL), row)]
    args = [x]
    for arr, off in adds:
        in_specs.append(pl.BlockSpec((ROW_TILE, D_MODEL), functools.partial(lambda o, i: (i + o, 0), off // ROW_TILE)))
        args.append(arr)
    in_specs += [pl.BlockSpec((1, D_MODEL), fixed), pl.BlockSpec((D_MODEL, QKV_DIM), fixed),
                 pl.BlockSpec((1, QKV_DIM), fixed), pl.BlockSpec((ROW_TILE, LANES), row),
                 pl.BlockSpec((ROW_TILE, LANES), row)]
    args += [g.reshape(1, D_MODEL), w.astype(BF16), b.reshape(1, QKV_DIM), cos_t, sin_t]
    out_shape, out_specs = [], []
    if adds:
        out_shape.append(jax.ShapeDtypeStruct((t, D_MODEL), F32))
        out_specs.append(pl.BlockSpec((ROW_TILE, D_MODEL), row))
    out_shape += [jax.ShapeDtypeStruct((t, Q_DIM), F32), jax.ShapeDtypeStruct((t, KV_DIM), F32),
                  jax.ShapeDtypeStruct((t, KV_DIM), F32),
                  jax.ShapeDtypeStruct((N_KV_HEADS, t, GROUP_LANES), BF16),
                  jax.ShapeDtypeStruct((N_KV_HEADS, t, GROUP_LANES), BF16)]
    out_specs += [pl.BlockSpec((ROW_TILE, Q_DIM), row), pl.BlockSpec((ROW_TILE, KV_DIM), row),
                  pl.BlockSpec((ROW_TILE, KV_DIM), row),
                  pl.BlockSpec((N_KV_HEADS, ROW_TILE, GROUP_LANES), lambda i: (0, i, 0)),
                  pl.BlockSpec((N_KV_HEADS, ROW_TILE, GROUP_LANES), lambda i: (0, i, 0))]
    if with_kmean:
        out_shape.append(jax.ShapeDtypeStruct((nt, 8, KV_DIM), F32))
        out_specs.append(pl.BlockSpec((1, 8, KV_DIM), lambda i: (i, 0, 0)))
    return pl.pallas_call(
        functools.partial(_qkv_kernel, len(adds), with_kmean),
        grid=(nt,), in_specs=in_specs, out_specs=out_specs, out_shape=out_shape,
        compiler_params=_params("arbitrary"), name="qkv_rope")(*args)


def _sink_softmax_pv(s, ok, sink_col, v):
    s = jnp.where(ok, s, NEG_INF)
    m = jnp.maximum(jnp.max(s, axis=1, keepdims=True), sink_col)
    p = jnp.exp(s - m)
    denom = jnp.sum(p, axis=1, keepdims=True) + jnp.exp(sink_col - m)
    return jnp.dot(p.astype(BF16), v, preferred_element_type=F32) / denom


def _sink_column(sinks_ref, kvh, rows):
    blk = lax.shift_right_logical(lax.broadcasted_iota(jnp.int32, (GROUP * rows, 1), 0), rows.bit_length() - 1)
    col = jnp.zeros((GROUP * rows, 1), F32)
    for h in range(GROUP):
        col = jnp.where(blk == h, sinks_ref[kvh * GROUP + h], col)
    return col


def _swa_prompt_kernel(sinks_ref, q_ref, kprev_ref, kcur_ref, vprev_ref, vcur_ref, o_ref):
    kvh = pl.program_id(1)
    blk = pl.program_id(2)
    rows = WINDOW
    qexp = _expand_heads(q_ref[...]).astype(BF16)
    k = jnp.concatenate([kprev_ref[0], kcur_ref[0]], axis=0)
    v = jnp.concatenate([vprev_ref[0], vcur_ref[0]], axis=0)
    s = lax.dot_general(qexp, k, NT_DIMS, preferred_element_type=F32)
    qi = lax.bitwise_and(lax.broadcasted_iota(jnp.int32, s.shape, 0), rows - 1)
    kj = lax.broadcasted_iota(jnp.int32, s.shape, 1)
    first_key = jnp.where(blk > 0, 0, WINDOW)
    ok = (kj > qi) & (kj <= qi + WINDOW) & (kj >= first_key)
    o_rep = _sink_softmax_pv(s, ok, _sink_column(sinks_ref, kvh, rows), v)
    o_ref[...] = _collapse_heads(o_rep, rows).astype(BF16)


def _swa_prompt(q, k_rep, v_rep, sinks, batch, seq):
    t = batch * seq
    nb = seq // WINDOW
    cur = lambda b, h, i: (h, b * nb + i, 0)
    prev = lambda b, h, i: (h, b * nb + jnp.maximum(i - 1, 0), 0)
    qo = lambda b, h, i: (b * nb + i, h)
    return pl.pallas_call(
        _swa_prompt_kernel,
        grid=(batch, N_KV_HEADS, nb),
        in_specs=[pl.BlockSpec(memory_space=pltpu.SMEM),
                  pl.BlockSpec((WINDOW, GROUP_LANES), qo),
                  pl.BlockSpec((1, WINDOW, GROUP_LANES), prev), pl.BlockSpec((1, WINDOW, GROUP_LANES), cur),
                  pl.BlockSpec((1, WINDOW, GROUP_LANES), prev), pl.BlockSpec((1, WINDOW, GROUP_LANES), cur)],
        out_specs=pl.BlockSpec((WINDOW, GROUP_LANES), qo),
        out_shape=jax.ShapeDtypeStruct((t, Q_DIM), BF16),
        compiler_params=_params("arbitrary", "arbitrary", "arbitrary"), name="swa_prompt",
    )(sinks, q, k_rep, k_rep, v_rep, v_rep)


def _swa_sample_kernel(seqs, nq, sinks_ref, q_ref, kn_ref, vn_ref, ck_ref, cv_ref, o_ref, ko_ref, vo_ref):
    win = ck_ref.shape[1]

    def one_seq(s, carry):
        r0 = pl.multiple_of(s * nq, nq)
        q = q_ref[pl.ds(r0, nq), :]
        kn = kn_ref[pl.ds(r0, nq), :]
        vn = vn_ref[pl.ds(r0, nq), :]
        ck = ck_ref[s]
        cv = cv_ref[s]
        ko_ref[s] = jnp.concatenate([ck[nq:], kn], axis=0)
        vo_ref[s] = jnp.concatenate([cv[nq:], vn], axis=0)
        kk = jnp.concatenate([ck, kn], axis=0)
        vv = jnp.concatenate([cv, vn], axis=0)
        outs = []
        for kvh in range(N_KV_HEADS):
            qexp = _expand_heads(q[:, kvh * GROUP_LANES:(kvh + 1) * GROUP_LANES]).astype(BF16)
            k = _replicate_kv(kk, kvh).astype(BF16)
            v = _replicate_kv(vv, kvh).astype(BF16)
            sc = lax.dot_general(qexp, k, NT_DIMS, preferred_element_type=F32)
            qi = lax.bitwise_and(lax.broadcasted_iota(jnp.int32, sc.shape, 0), nq - 1)
            kj = lax.broadcasted_iota(jnp.int32, sc.shape, 1)
            ok = ((kj < win) & (kj > qi + win - WINDOW)) | ((kj >= win) & (kj - win <= qi))
            o_rep = _sink_softmax_pv(sc, ok, _sink_column(sinks_ref, kvh, nq), v)
            outs.append(_collapse_heads(o_rep, nq))
        o_ref[pl.ds(r0, nq), :] = jnp.concatenate(outs, axis=1).astype(BF16)
        return carry

    lax.fori_loop(0, seqs, one_seq, 0)


def _swa_sample(q, k, v, cache_k, cache_v, sinks, row0, nq):
    db, win = cache_k.shape[0], cache_k.shape[1]
    seqs = 8
    rows = seqs * nq
    tok = lambda g: (row0 // rows + g, 0)
    cache = lambda g: (g, 0, 0)
    return pl.pallas_call(
        functools.partial(_swa_sample_kernel, seqs, nq),
        grid=(db // seqs,),
        in_specs=[pl.BlockSpec(memory_space=pltpu.SMEM),
                  pl.BlockSpec((rows, Q_DIM), tok), pl.BlockSpec((rows, KV_DIM), tok),
                  pl.BlockSpec((rows, KV_DIM), tok),
                  pl.BlockSpec((seqs, win, KV_DIM), cache), pl.BlockSpec((seqs, win, KV_DIM), cache)],
        out_specs=[pl.BlockSpec((rows, Q_DIM), lambda g: (g, 0)),
                   pl.BlockSpec((seqs, win, KV_DIM), cache), pl.BlockSpec((seqs, win, KV_DIM), cache)],
        out_shape=[jax.ShapeDtypeStruct((db * nq, Q_DIM), BF16),
                   jax.ShapeDtypeStruct((db, win, KV_DIM), F32), jax.ShapeDtypeStruct((db, win, KV_DIM), F32)],
        compiler_params=_params("arbitrary"), name="swa_sample",
    )(sinks, q, k, v, cache_k.reshape(db, win, KV_DIM), cache_v.reshape(db, win, KV_DIM))


def _moba_prompt_kernel(q_ref, k_ref, v_ref, kmean_ref, o_ref, m_ref, l_ref, acc_ref):
    i = pl.program_id(2)
    rows = MOBA_BLOCK
    qexp32 = _expand_heads(q_ref[...])
    qexp = qexp32.astype(BF16)
    gate = lax.dot_general(qexp32, kmean_ref[0, 0], NT_DIMS, preferred_element_type=F32,
                           precision=lax.Precision.HIGHEST)
    sel = jnp.where(_top3_mask(gate, i), 1.0, 0.0)
    blk_lane = lax.broadcasted_iota(jnp.int32, sel.shape, 1)

    def scores(j):
        start = pl.multiple_of(j * MOBA_BLOCK, MOBA_BLOCK)
        s = lax.dot_general(qexp, k_ref[0, pl.ds(start, MOBA_BLOCK), :], NT_DIMS,
                            preferred_element_type=F32)
        return s, v_ref[0, pl.ds(start, MOBA_BLOCK), :]

    s, v = scores(i)
    qi = lax.bitwise_and(lax.broadcasted_iota(jnp.int32, s.shape, 0), rows - 1)
    kj = lax.broadcasted_iota(jnp.int32, s.shape, 1)
    s = jnp.where(kj <= qi, s, NEG_INF)
    m = jnp.max(s, axis=1, keepdims=True)
    p = jnp.exp(s - m)
    m_ref[...] = m
    l_ref[...] = jnp.sum(p, axis=1, keepdims=True)
    acc_ref[...] = jnp.dot(p.astype(BF16), v, preferred_element_type=F32)

    def past_block(j, carry):
        s, v = scores(j)
        chosen = jnp.sum(jnp.where(blk_lane == j, sel, 0.0), axis=1, keepdims=True) > 0.0
        s = jnp.where(chosen, s, NEG_INF)
        m_old = m_ref[...]
        m_new = jnp.maximum(m_old, jnp.max(s, axis=1, keepdims=True))
        alpha = jnp.exp(m_old - m_new)
        p = jnp.exp(s - m_new)
        m_ref[...] = m_new
        l_ref[...] = alpha * l_ref[...] + jnp.sum(p, axis=1, keepdims=True)
        acc_ref[...] = alpha * acc_ref[...] + jnp.dot(p.astype(BF16), v, preferred_element_type=F32)
        return carry

    lax.fori_loop(0, i, past_block, 0)
    o_ref[...] = _collapse_heads(acc_ref[...] / l_ref[...], rows).astype(BF16)


def _moba_prompt(q, k_rep, v_rep, kmean_rep, batch, seq):
    nb = seq // MOBA_BLOCK
    qo = lambda b, h, i: (b * nb + i, h)
    kv = lambda b, h, i: (h, b, 0)
    return pl.pallas_call(
        _moba_prompt_kernel,
        grid=(batch, N_KV_HEADS, nb),
        in_specs=[pl.BlockSpec((MOBA_BLOCK, GROUP_LANES), qo),
                  pl.BlockSpec((1, seq, GROUP_LANES), kv), pl.BlockSpec((1, seq, GROUP_LANES), kv),
                  pl.BlockSpec((1, 1, nb, GROUP_LANES), lambda b, h, i: (b, h, 0, 0))],
        out_specs=pl.BlockSpec((MOBA_BLOCK, GROUP_LANES), qo),
        out_shape=jax.ShapeDtypeStruct((batch * seq, Q_DIM), BF16),
        scratch_shapes=[pltpu.VMEM((GROUP * MOBA_BLOCK, 1), F32), pltpu.VMEM((GROUP * MOBA_BLOCK, 1), F32),
                        pltpu.VMEM((GROUP * MOBA_BLOCK, GROUP_LANES), F32)],
        compiler_params=_params("arbitrary", "arbitrary", "arbitrary"), name="moba_prompt",
    )(q, k_rep, v_rep, kmean_rep)


def _moba_sample_kernel(nq, n_chunks, pt_ref, qexp_ref, kn_ref, vn_ref, *refs):
    del pt_ref
    pages = PAGES_PER_STEP
    k_refs = refs[:pages]
    v_refs = refs[pages:2 * pages]
    o_ref = refs[2 * pages]
    s_ref, ksum_ref, acc_ref, l_ref = refs[2 * pages + 1:]
    ph = pl.program_id(1)
    c = pl.program_id(2)
    keys = pages * PAGE_SIZE
    blocks_per_step = keys // MOBA_BLOCK
    n_blocks = n_chunks * blocks_per_step
    rows = N_HEADS * nq
    col0 = pl.multiple_of(c * keys, keys)

    @pl.when(ph == 0)
    def _():
        kb = jnp.concatenate([r[0] for r in k_refs], axis=0)
        s_ref[:, pl.ds(col0, keys)] = lax.dot_general(
            qexp_ref[0].astype(BF16), kb.astype(BF16), NT_DIMS, preferred_element_type=F32)
        for n in range(blocks_per_step):
            ksum_ref[pl.ds(c * blocks_per_step + n, 1), :] = jnp.sum(
                kb[n * MOBA_BLOCK:(n + 1) * MOBA_BLOCK], axis=0, keepdims=True)

    @pl.when((ph == 0) & (c == n_chunks - 1))
    def _():
        qexp32 = qexp_ref[0]
        kmean = ksum_ref[...] * (1.0 / MOBA_BLOCK)
        gate = lax.dot_general(qexp32, kmean, NT_DIMS, preferred_element_type=F32,
                               precision=lax.Precision.HIGHEST)
        sel = jnp.where(_top3_mask(gate, n_blocks), 1.0, 0.0)
        s_own = lax.dot_general(qexp32.astype(BF16), kn_ref[...].astype(BF16), NT_DIMS,
                                preferred_element_type=F32)
        qi = lax.bitwise_and(lax.broadcasted_iota(jnp.int32, s_own.shape, 0), nq - 1)
        kj = lax.broadcasted_iota(jnp.int32, s_own.shape, 1)
        s_own = jnp.where(kj <= qi, s_own, NEG_INF)
        m = jnp.max(s_own, axis=1, keepdims=True)
        for n in range(n_blocks):
            blk = jnp.where(sel[:, n:n + 1] > 0.0, s_ref[:, n * MOBA_BLOCK:(n + 1) * MOBA_BLOCK], NEG_INF)
            s_ref[:, n * MOBA_BLOCK:(n + 1) * MOBA_BLOCK] = blk
            m = jnp.maximum(m, jnp.max(blk, axis=1, keepdims=True))
        p_own = jnp.exp(s_own - m)
        l = jnp.sum(p_own, axis=1, keepdims=True)
        for n in range(n_blocks):
            p = jnp.exp(s_ref[:, n * MOBA_BLOCK:(n + 1) * MOBA_BLOCK] - m)
            s_ref[:, n * MOBA_BLOCK:(n + 1) * MOBA_BLOCK] = p
            l = l + jnp.sum(p, axis=1, keepdims=True)
        l_ref[...] = l
        acc_ref[...] = jnp.dot(p_own.astype(BF16), vn_ref[...].astype(BF16), preferred_element_type=F32)

    @pl.when(ph == 1)
    def _():
        vb = jnp.concatenate([r[0] for r in v_refs], axis=0).astype(BF16)
        acc_ref[...] += jnp.dot(s_ref[:, pl.ds(col0, keys)].astype(BF16), vb, preferred_element_type=F32)

    @pl.when((ph == 1) & (c == n_chunks - 1))
    def _():
        o_ref[0] = acc_ref[...] / l_ref[...]


def _moba_sample(qexp, k, v, pool_k, pool_v, page_table, row0, nq):
    db, n_pages = page_table.shape
    n_pool = pool_k.shape[0]
    pages = PAGES_PER_STEP
    n_chunks = n_pages // pages
    rows = N_HEADS * nq
    pk = pool_k.reshape(n_pool, PAGE_SIZE, KV_DIM)
    pv = pool_v.reshape(n_pool, PAGE_SIZE, KV_DIM)
    last = n_chunks - 1

    def k_map(p, b, ph, c, pt):
        return (pt[b, jnp.where(ph == 0, c, last) * pages + p], 0, 0)

    def v_map(p, b, ph, c, pt):
        return (pt[b, jnp.where(ph == 0, 0, c) * pages + p], 0, 0)

    page = (1, PAGE_SIZE, KV_DIM)
    new = lambda b, ph, c, pt: (row0 // nq + b, 0)
    grid_spec = pltpu.PrefetchScalarGridSpec(
        num_scalar_prefetch=1,
        grid=(db, 2, n_chunks),
        in_specs=[pl.BlockSpec((1, rows, KV_DIM), lambda b, ph, c, pt: (b, 0, 0)),
                  pl.BlockSpec((nq, KV_DIM), new), pl.BlockSpec((nq, KV_DIM), new)]
        + [pl.BlockSpec(page, functools.partial(k_map, p)) for p in range(pages)]
        + [pl.BlockSpec(page, functools.partial(v_map, p)) for p in range(pages)],
        out_specs=pl.BlockSpec((1, rows, KV_DIM), lambda b, ph, c, pt: (b, 0, 0)),
        scratch_shapes=[pltpu.VMEM((rows, n_pages * PAGE_SIZE), F32),
                        pltpu.VMEM((n_pages * PAGE_SIZE // MOBA_BLOCK, KV_DIM), F32),
                        pltpu.VMEM((rows, KV_DIM), F32), pltpu.VMEM((rows, 1), F32)])
    return pl.pallas_call(
        functools.partial(_moba_sample_kernel, nq, n_chunks),
        grid_spec=grid_spec,
        out_shape=jax.ShapeDtypeStruct((db, rows, KV_DIM), F32),
        compiler_params=_params("arbitrary", "arbitrary", "arbitrary"), name="moba_sample",
    )(page_table, qexp, k, v, *([pk] * pages), *([pv] * pages))


def _proj_route_kernel(o_ref, x_ref, wo_ref, bo_ref, g_ref, wr_ref, br_ref, x1_ref, h_ref, route_ref):
    x1 = x_ref[...] + jnp.dot(o_ref[...], wo_ref[...], preferred_element_type=F32) + bo_ref[...]
    x1_ref[...] = x1
    h = _rms(x1, g_ref[...])
    h_ref[...] = h
    lg = jnp.dot(h, wr_ref[...], preferred_element_type=F32, precision=lax.Precision.HIGHEST) + br_ref[...]
    lane = lax.broadcasted_iota(jnp.int32, lg.shape, 1)
    lane_f = lane.astype(F32)
    is_group = lane < N_GROUPS
    gl = jnp.where(is_group, lg, NEG_INF)
    gmax = jnp.max(gl, axis=1, keepdims=True)
    grp = jnp.min(jnp.where(gl == gmax, lane_f, 1e9), axis=1, keepdims=True)
    g_w = 1.0 / jnp.sum(jnp.where(is_group, jnp.exp(gl - gmax), 0.0), axis=1, keepdims=True)
    lane_grp = lax.shift_right_logical(lane - N_GROUPS, 3).astype(F32)
    in_grp = (lane >= N_GROUPS) & (lane < N_GROUPS + N_EXPERTS) & (lane_grp == grp)
    el = jnp.where(in_grp, lg, NEG_INF)
    v1 = jnp.max(el, axis=1, keepdims=True)
    i1 = jnp.min(jnp.where(el == v1, lane_f, 1e9), axis=1, keepdims=True)
    el2 = jnp.where(lane_f == i1, BELOW_NEG_INF, el)
    v2 = jnp.max(el2, axis=1, keepdims=True)
    i2 = jnp.min(jnp.where(el2 == v2, lane_f, 1e9), axis=1, keepdims=True)
    e2w = jnp.exp(v2 - v1)
    w1 = g_w / (1.0 + e2w)
    w2 = g_w * e2w / (1.0 + e2w)
    route = jnp.where(lane == 0, i1 - N_GROUPS, 0.0)
    route = jnp.where(lane == 1, i2 - N_GROUPS, route)
    route = jnp.where(lane == 2, w1, route)
    route = jnp.where(lane == 3, w2, route)
    route_ref[...] = route


def _proj_route(o, x, wo, bo, g, w_gr, b_gr, w_er, b_er):
    t = x.shape[0]
    row = lambda i: (i, 0)
    fixed = lambda i: (0, 0)
    pad = LANES - N_GROUPS - N_EXPERTS
    wr = jnp.concatenate([w_gr, w_er, jnp.zeros((D_MODEL, pad), F32)], axis=1)
    br = jnp.concatenate([b_gr, b_er, jnp.zeros((pad,), F32)]).reshape(1, LANES)
    return pl.pallas_call(
        _proj_route_kernel,
        grid=(t // ROW_TILE,),
        in_specs=[pl.BlockSpec((ROW_TILE, Q_DIM), row), pl.BlockSpec((ROW_TILE, D_MODEL), row),
                  pl.BlockSpec((Q_DIM, D_MODEL), fixed), pl.BlockSpec((1, D_MODEL), fixed),
                  pl.BlockSpec((1, D_MODEL), fixed), pl.BlockSpec((D_MODEL, LANES), fixed),
                  pl.BlockSpec((1, LANES), fixed)],
        out_specs=[pl.BlockSpec((ROW_TILE, D_MODEL), row), pl.BlockSpec((ROW_TILE, D_MODEL), row),
                   pl.BlockSpec((ROW_TILE, LANES), row)],
        out_shape=[jax.ShapeDtypeStruct((t, D_MODEL), F32), jax.ShapeDtypeStruct((t, D_MODEL), F32),
                   jax.ShapeDtypeStruct((t, LANES), F32)],
        compiler_params=_params("arbitrary"), name="proj_route",
    )(o, x, wo.astype(BF16), bo.reshape(1, D_MODEL), g.reshape(1, D_MODEL), wr, br)


def _expert_kernel(n_steps, blk_e_ref, nvalid_ref, src_ref, src_next_ref, dst_ref, wrow_ref, h_hbm,
                   wg_ref, wu_ref, wd_ref, y_hbm, xbuf, ybuf, wg_bf, wu_bf, wd_bf, gsem, ssem):
    i = pl.program_id(0)
    slot = lax.rem(i, 2)
    rows = EXPERT_ROWS
    nvalid = nvalid_ref[i]
    nvalid_next = jnp.where(i + 1 < n_steps, nvalid_ref[jnp.minimum(i + 1, n_steps - 1)], 0)
    nvalid_prev2 = jnp.where(i >= 2, nvalid_ref[jnp.maximum(i - 2, 0)], 0)

    def row_copy_in(idx_ref, r, s):
        return pltpu.make_async_copy(h_hbm.at[pl.ds(idx_ref[0, r], 1)], xbuf.at[s, pl.ds(r, 1)], gsem.at[s])

    def row_copy_out(r, s):
        return pltpu.make_async_copy(ybuf.at[s, pl.ds(r, 1)], y_hbm.at[pl.ds(dst_ref[0, r], 1)], ssem.at[s])

    def for_rows(fn, n=None):
        def body(r, carry):
            if n is None:
                fn(r)
            else:
                pl.when(r < n)(lambda: fn(r))
            return carry
        lax.fori_loop(0, rows, body, 0, unroll=8)

    @pl.when((i == 0) & (nvalid > 0))
    def _():
        for_rows(lambda r: row_copy_in(src_ref, r, 0).start())

    @pl.when(nvalid_next > 0)
    def _():
        for_rows(lambda r: row_copy_in(src_next_ref, r, 1 - slot).start())

    @pl.when(nvalid_prev2 > 0)
    def _():
        for_rows(lambda r: row_copy_out(r, slot).wait(), nvalid_prev2)

    @pl.when(nvalid > 0)
    def _():
        e_changed = jnp.logical_or(i == 0, blk_e_ref[i] != blk_e_ref[jnp.maximum(i - 1, 0)])

        @pl.when(e_changed)
        def _():
            wg_bf[...] = wg_ref[0].astype(BF16)
            wu_bf[...] = wu_ref[0].astype(BF16)
            wd_bf[...] = wd_ref[0].astype(BF16)

        for_rows(lambda r: row_copy_in(src_ref, r, slot).wait())
        x = xbuf[slot].astype(BF16)
        gate = jnp.dot(x, wg_bf[...], preferred_element_type=F32)
        up = jnp.dot(x, wu_bf[...], preferred_element_type=F32)
        hid = (gate * jax.nn.sigmoid(gate) * up).astype(BF16)
        y = jnp.dot(hid, wd_bf[...], preferred_element_type=F32)
        ybuf[slot] = y * wrow_ref[...]
        for_rows(lambda r: row_copy_out(r, slot).start(), nvalid)


def _expert_ffn(h, e_idx, e_w, w_gate, w_up, w_down):
    t = h.shape[0]
    a = t * EXPERT_TOPK
    rows = EXPERT_ROWS
    n_blk = -(-a // rows) + N_EXPERTS
    n_steps = n_blk + 2
    flat_e = e_idx.reshape(a)
    order = jnp.argsort(flat_e)
    sorted_e = flat_e[order]
    counts = jnp.bincount(flat_e, length=N_EXPERTS)
    padded = (counts + rows - 1) // rows * rows
    pad_end = jnp.cumsum(padded)
    start = jnp.cumsum(counts) - counts
    dest = (pad_end - padded)[sorted_e] + jnp.arange(a) - start[sorted_e]
    tok = (order // EXPERT_TOPK).astype(jnp.int32)
    slot = (order % EXPERT_TOPK).astype(jnp.int32)
    n_rows = n_steps * rows
    src = jnp.zeros((n_rows,), jnp.int32).at[dest].set(tok)
    dst = jnp.zeros((n_rows,), jnp.int32).at[dest].set(slot * t + tok)
    wrow = jnp.zeros((n_rows,), F32).at[dest].set(e_w.reshape(a)[order])
    nvalid = jnp.zeros((n_rows,), jnp.int32).at[dest].set(1).reshape(n_steps, rows).sum(axis=1)
    blk_e = jnp.minimum(jnp.searchsorted(pad_end, jnp.arange(n_steps) * rows, side='right'),
                        N_EXPERTS - 1).astype(jnp.int32)
    idx_blk = lambda i, be, na: (i, 0, 0)
    idx_next = lambda i, be, na: (jnp.minimum(i + 1, n_steps - 1), 0, 0)
    w_in = lambda i, be, na: (be[i], 0, 0)
    grid_spec = pltpu.PrefetchScalarGridSpec(
        num_scalar_prefetch=2,
        grid=(n_steps,),
        in_specs=[pl.BlockSpec((1, 1, rows), idx_blk, memory_space=pltpu.SMEM),
                  pl.BlockSpec((1, 1, rows), idx_next, memory_space=pltpu.SMEM),
                  pl.BlockSpec((1, 1, rows), idx_blk, memory_space=pltpu.SMEM),
                  pl.BlockSpec((rows, 1), lambda i, be, na: (i, 0)),
                  pl.BlockSpec(memory_space=pl.ANY),
                  pl.BlockSpec((1, D_MODEL, EXPERT_FF), w_in), pl.BlockSpec((1, D_MODEL, EXPERT_FF), w_in),
                  pl.BlockSpec((1, EXPERT_FF, D_MODEL), w_in)],
        out_specs=pl.BlockSpec(memory_space=pl.ANY),
        scratch_shapes=[pltpu.VMEM((2, rows, D_MODEL), F32), pltpu.VMEM((2, rows, D_MODEL), F32),
                        pltpu.VMEM((D_MODEL, EXPERT_FF), BF16), pltpu.VMEM((D_MODEL, EXPERT_FF), BF16),
                        pltpu.VMEM((EXPERT_FF, D_MODEL), BF16),
                        pltpu.SemaphoreType.DMA((2,)), pltpu.SemaphoreType.DMA((2,))])

    def body(blk_e_ref, nvalid_ref, src_ref, src_next_ref, dst_ref, *rest):
        _expert_kernel(n_steps, blk_e_ref, nvalid_ref, src_ref.at[0], src_next_ref.at[0], dst_ref.at[0], *rest)

    return pl.pallas_call(
        body, grid_spec=grid_spec,
        out_shape=jax.ShapeDtypeStruct((EXPERT_TOPK * t, D_MODEL), F32),
        compiler_params=_params("arbitrary"), name="expert_ffn",
    )(blk_e, nvalid.astype(jnp.int32), src.reshape(n_steps, 1, rows), src.reshape(n_steps, 1, rows),
      dst.reshape(n_steps, 1, rows), wrow.reshape(n_rows, 1), h, w_gate, w_up, w_down)


def _final_kernel(x_ref, y0_ref, y1_ref, g_ref, o_ref):
    o_ref[...] = _rms(x_ref[...] + y0_ref[...] + y1_ref[...], g_ref[...])


def _final_norm(x, y2, g, row0, n_rows):
    t = x.shape[0]
    off = row0 // ROW_TILE
    return pl.pallas_call(
        _final_kernel,
        grid=(n_rows // ROW_TILE,),
        in_specs=[pl.BlockSpec((ROW_TILE, D_MODEL), lambda i: (i + off, 0)),
                  pl.BlockSpec((ROW_TILE, D_MODEL), lambda i: (i + off, 0)),
                  pl.BlockSpec((ROW_TILE, D_MODEL), lambda i: (i + off + t // ROW_TILE, 0)),
                  pl.BlockSpec((1, D_MODEL), lambda i: (0, 0))],
        out_specs=pl.BlockSpec((ROW_TILE, D_MODEL), lambda i: (i, 0)),
        out_shape=jax.ShapeDtypeStruct((n_rows, D_MODEL), F32),
        compiler_params=_params("arbitrary"), name="final_norm",
    )(x, y2, y2, g.reshape(1, D_MODEL))


def _rope_tables(pos):
    half = HEAD_DIM // 2
    inv = ROPE_THETA ** (-jnp.arange(half, dtype=F32) / half)
    ang = pos.astype(F32)[:, None] * inv[None, :]
    cos = jnp.cos(ang)
    sin = jnp.sin(ang)
    reps = LANES // HEAD_DIM
    return jnp.tile(jnp.concatenate([cos, cos], axis=1), (1, reps)), jnp.tile(jnp.concatenate([-sin, sin], axis=1), (1, reps))


def _moe(o, x, wo, bo, g, w_gr, b_gr, w_er, b_er, w_gate, w_up, w_down):
    x1, h, route = _proj_route(o, x, wo, bo, g, w_gr, b_gr, w_er, b_er)
    e_idx = jnp.clip(route[:, :EXPERT_TOPK].astype(jnp.int32), 0, N_EXPERTS - 1)
    e_w = route[:, EXPERT_TOPK:2 * EXPERT_TOPK]
    return x1, _expert_ffn(h, e_idx, e_w, w_gate, w_up, w_down)


def kernel(x_prompt, x_sample, cache_swa_k, cache_swa_v, cache_moba_k, cache_moba_v, page_table, norm_mix, norm_ffn, norm_final, swa_w_qkv, swa_b_qkv, swa_sinks, swa_w_o, swa_b_o, moba_w_qkv, moba_w_o, w_group_router, b_group_router, w_expert_router, b_expert_router, w_gate, w_up, w_down):
    batch, seq, _ = x_prompt.shape
    db, nq, _ = x_sample.shape
    past_len = page_table.shape[1] * PAGE_SIZE
    tp = batch * seq
    ts = db * nq
    t = tp + ts
    x = jnp.concatenate([x_prompt.reshape(tp, D_MODEL), x_sample.reshape(ts, D_MODEL)], axis=0)
    pos = jnp.concatenate([jnp.tile(jnp.arange(seq), batch), jnp.tile(past_len + jnp.arange(nq), db)])
    cos_t, sin_t = _rope_tables(pos)

    q, k0, v0, k_rep, v_rep = _qkv_rope(x, [], norm_mix[0], swa_w_qkv[0], swa_b_qkv[0], cos_t, sin_t, False)
    o_p = _swa_prompt(q, k_rep, v_rep, swa_sinks[0], batch, seq)
    o_s, swa_ks, swa_vs = _swa_sample(q, k0, v0, cache_swa_k[0], cache_swa_v[0], swa_sinks[0], tp, nq)
    o = jnp.concatenate([o_p, o_s], axis=0)
    x1, y2 = _moe(o, x, swa_w_o[0], swa_b_o[0], norm_ffn[0], w_group_router[0], b_group_router[0],
                  w_expert_router[0], b_expert_router[0], w_gate[0], w_up[0], w_down[0])

    x, q, k1, v1, k_rep, v_rep, kmean = _qkv_rope(
        x1, [(y2, 0), (y2, t)], norm_mix[1], moba_w_qkv[0], jnp.zeros((QKV_DIM,), F32), cos_t, sin_t, True)
    nb = seq // MOBA_BLOCK
    kmean_rep = jnp.tile(kmean[:batch * nb, 0].reshape(batch, nb, N_KV_HEADS, 1, HEAD_DIM), (1, 1, 1, GROUP, 1))
    kmean_rep = kmean_rep.reshape(batch, nb, N_KV_HEADS, GROUP_LANES).transpose(0, 2, 1, 3)
    o_p = _moba_prompt(q, k_rep, v_rep, kmean_rep, batch, seq)
    qs = q[tp:].reshape(db, nq, N_HEADS, 1, HEAD_DIM)
    slot = (jnp.arange(N_HEADS) // GROUP)[:, None] == jnp.arange(N_KV_HEADS)[None, :]
    qexp = jnp.where(slot[None, None, :, :, None], qs, 0.0).transpose(0, 2, 1, 3, 4).reshape(db, N_HEADS * nq, KV_DIM)
    o_s = _moba_sample(qexp, k1, v1, cache_moba_k[0], cache_moba_v[0], page_table, tp, nq)
    o_s = o_s.reshape(db, N_HEADS, nq, N_KV_HEADS, HEAD_DIM)
    o_s = jnp.sum(jnp.where(slot[None, :, None, :, None], o_s, 0.0), axis=3).transpose(0, 2, 1, 3)
    o = jnp.concatenate([o_p, o_s.reshape(ts, Q_DIM).astype(BF16)], axis=0)
    x1, y2 = _moe(o, x, moba_w_o[0], jnp.zeros((D_MODEL,), F32), norm_ffn[1], w_group_router[1],
                  b_group_router[1], w_expert_router[1], b_expert_router[1], w_gate[1], w_up[1], w_down[1])

    y_prompt = _final_norm(x1, y2, norm_final, 0, tp).reshape(batch, seq, D_MODEL)
    y_sample = _final_norm(x1, y2, norm_final, tp, ts).reshape(db, nq, D_MODEL)
    win = min(WINDOW, seq)
    kv5 = lambda a, b, s: a.reshape(1, b, s, N_KV_HEADS, HEAD_DIM)
    k0p = k0[:tp].reshape(batch, seq, KV_DIM)[:, seq - win:]
    v0p = v0[:tp].reshape(batch, seq, KV_DIM)[:, seq - win:]
    return (y_prompt, y_sample, kv5(k0p, batch, win), kv5(v0p, batch, win),
            kv5(swa_ks, db, swa_ks.shape[1]), kv5(swa_vs, db, swa_vs.shape[1]),
            kv5(k1[:tp], batch, seq), kv5(v1[:tp], batch, seq), kv5(k1[tp:], db, nq), kv5(v1[tp:], db, nq))
```

```python
import functools

import jax
import jax.numpy as jnp
from jax import lax
from jax.experimental import pallas as pl
from jax.experimental.pallas import tpu as pltpu

D_MODEL = 1024
N_HEADS = 16
N_KV_HEADS = 4
HEAD_DIM = 64
GROUP = N_HEADS // N_KV_HEADS
Q_DIM = N_HEADS * HEAD_DIM
KV_DIM = N_KV_HEADS * HEAD_DIM
QKV_DIM = Q_DIM + 2 * KV_DIM
ROPE_THETA = 10000.0
WINDOW = 128
MOBA_BLOCK = 256
MOBA_TOPK = 3
PAGE_SIZE = 128
N_GROUPS = 8
EXPERTS_PER_GROUP = 8
N_EXPERTS = N_GROUPS * EXPERTS_PER_GROUP
EXPERT_TOPK = 2
EXPERT_FF = D_MODEL // 4
RMS_EPS = 1e-6
NEG_INF = -1e30
BELOW_NEG_INF = -3e38

LANES = 128
GROUP_LANES = GROUP * HEAD_DIM
ROW_TILE = 256
EXPERT_ROWS = 256
PAGES_PER_STEP = 16
VMEM_LIMIT = 56 * 1024 * 1024

F32 = jnp.float32
BF16 = jnp.bfloat16
NT_DIMS = (((1,), (1,)), ((), ()))


def _params(*sem):
    return pltpu.CompilerParams(dimension_semantics=sem, vmem_limit_bytes=VMEM_LIMIT)


def _rms(x, g):
    return x * lax.rsqrt(jnp.mean(x * x, axis=-1, keepdims=True) + RMS_EPS) * g


def _replicate_head(x128, odd):
    swapped = pltpu.roll(x128, HEAD_DIM, axis=1)
    low = lax.broadcasted_iota(jnp.int32, x128.shape, 1) < HEAD_DIM
    return jnp.where(low != odd, x128, swapped)


def _replicate_kv(x256, kvh):
    col = x256[:, (kvh // 2) * LANES:(kvh // 2 + 1) * LANES]
    rep = _replicate_head(col, kvh % 2 == 1)
    return jnp.concatenate([rep, rep], axis=1)


def _expand_heads(q):
    head = lax.shift_right_logical(lax.broadcasted_iota(jnp.int32, q.shape, 1), 6)
    return jnp.concatenate([jnp.where(head == h, q, 0.0) for h in range(GROUP)], axis=0)


def _collapse_heads(o_rep, rows):
    head = lax.shift_right_logical(lax.broadcasted_iota(jnp.int32, (rows, GROUP_LANES), 1), 6)
    out = jnp.zeros((rows, GROUP_LANES), F32)
    for h in range(GROUP):
        out = jnp.where(head == h, o_rep[h * rows:(h + 1) * rows], out)
    return out


def _top3_mask(gate, n_valid):
    lane = lax.broadcasted_iota(jnp.int32, gate.shape, 1)
    lane_f = lane.astype(F32)
    valid = lane < n_valid
    g = jnp.where(valid, gate, NEG_INF)
    sel = jnp.zeros(gate.shape, jnp.bool_)
    for _ in range(MOBA_TOPK):
        mx = jnp.max(g, axis=1, keepdims=True)
        first = jnp.min(jnp.where(g == mx, lane_f, 1e9), axis=1, keepdims=True)
        hit = lane_f == first
        sel = jnp.logical_or(sel, jnp.logical_and(hit, valid))
        g = jnp.where(hit, BELOW_NEG_INF, g)
    return sel


def _qkv_kernel(n_add, moba, *refs):
    refs = list(refs)
    x_ref = refs.pop(0)
    add_refs = [refs.pop(0) for _ in range(n_add)]
    g_ref, w_ref, b_ref, cos_ref, sin_ref = refs[:5]
    outs = refs[5:]
    x = x_ref[...]
    for r in add_refs:
        x = x + r[...]
    if n_add:
        outs.pop(0)[...] = x
    q_ref, k_ref, v_ref, krep_ref = outs[:4]
    h = _rms(x, g_ref[...]).astype(BF16)
    y = jnp.dot(h, w_ref[...], preferred_element_type=F32) + b_ref[...]
    cos = cos_ref[...]
    sin = sin_ref[...]
    low = lax.bitwise_and(lax.broadcasted_iota(jnp.int32, cos.shape, 1), HEAD_DIM - 1) < HEAD_DIM // 2

    def rope(slab):
        partner = jnp.where(low, pltpu.roll(slab, LANES - HEAD_DIM // 2, axis=1),
                            pltpu.roll(slab, HEAD_DIM // 2, axis=1))
        return slab * cos + partner * sin

    scale = HEAD_DIM ** -0.5
    for c in range(Q_DIM // LANES):
        q_ref[:, c * LANES:(c + 1) * LANES] = rope(y[:, c * LANES:(c + 1) * LANES]) * scale
    k = jnp.concatenate([rope(y[:, Q_DIM + c * LANES:Q_DIM + (c + 1) * LANES])
                         for c in range(KV_DIM // LANES)], axis=1)
    v = y[:, Q_DIM + KV_DIM:]
    k_ref[...] = k
    v_ref[...] = v
    for kvh in range(N_KV_HEADS):
        krep_ref[kvh] = _replicate_kv(k, kvh).astype(BF16)
    if not moba:
        for kvh in range(N_KV_HEADS):
            outs[4][kvh] = _replicate_kv(v, kvh).astype(BF16)
    else:
        kmean_ref, qt_ref, vt_ref = outs[4:]
        kmean_ref[0] = jnp.broadcast_to(jnp.mean(k, axis=0, keepdims=True), (8, KV_DIM))
        qt_ref[...] = q_ref[...].T
        vt_ref[:, 0] = v.T.reshape(N_KV_HEADS, HEAD_DIM, ROW_TILE).astype(BF16)


def _qkv_rope(x, adds, g, w, b, cos_t, sin_t, moba):
    t = x.shape[0]
    nt = t // ROW_TILE
    row = lambda i: (i, 0)
    fixed = lambda i: (0, 0)
    in_specs = [pl.BlockSpec((ROW_TILE, D_MODEL), row)]
    args = [x]
    for arr, off in adds:
        in_specs.append(pl.BlockSpec((ROW_TILE, D_MODEL), functools.partial(lambda o, i: (i + o, 0), off // ROW_TILE)))
        args.append(arr)
    in_specs += [pl.BlockSpec((1, D_MODEL), fixed), pl.BlockSpec((D_MODEL, QKV_DIM), fixed),
                 pl.BlockSpec((1, QKV_DIM), fixed), pl.BlockSpec((ROW_TILE, LANES), row),
                 pl.BlockSpec((ROW_TILE, LANES), row)]
    args += [g.reshape(1, D_MODEL), w.astype(BF16), b.reshape(1, QKV_DIM), cos_t, sin_t]
    out_shape, out_specs = [], []
    if adds:
        out_shape.append(jax.ShapeDtypeStruct((t, D_MODEL), F32))
        out_specs.append(pl.BlockSpec((ROW_TILE, D_MODEL), row))
    rep_shape = jax.ShapeDtypeStruct((N_KV_HEADS, t, GROUP_LANES), BF16)
    rep_spec = pl.BlockSpec((N_KV_HEADS, ROW_TILE, GROUP_LANES), lambda i: (0, i, 0))
    out_shape += [jax.ShapeDtypeStruct((t, Q_DIM), F32), jax.ShapeDtypeStruct((t, KV_DIM), F32),
                  jax.ShapeDtypeStruct((t, KV_DIM), F32), rep_shape]
    out_specs += [pl.BlockSpec((ROW_TILE, Q_DIM), row), pl.BlockSpec((ROW_TILE, KV_DIM), row),
                  pl.BlockSpec((ROW_TILE, KV_DIM), row), rep_spec]
    if not moba:
        out_shape.append(rep_shape)
        out_specs.append(rep_spec)
    else:
        out_shape += [jax.ShapeDtypeStruct((nt, 8, KV_DIM), F32), jax.ShapeDtypeStruct((Q_DIM, t), F32),
                      jax.ShapeDtypeStruct((N_KV_HEADS, nt, HEAD_DIM, ROW_TILE), BF16)]
        out_specs += [pl.BlockSpec((1, 8, KV_DIM), lambda i: (i, 0, 0)),
                      pl.BlockSpec((Q_DIM, ROW_TILE), lambda i: (0, i)),
                      pl.BlockSpec((N_KV_HEADS, 1, HEAD_DIM, ROW_TILE), lambda i: (0, i, 0, 0))]
    return pl.pallas_call(
        functools.partial(_qkv_kernel, len(adds), moba),
        grid=(nt,), in_specs=in_specs, out_specs=out_specs, out_shape=out_shape,
        compiler_params=_params("arbitrary"), name="qkv_rope")(*args)


def _sink_softmax_pv(s, ok, sink_col, v):
    s = jnp.where(ok, s, NEG_INF)
    m = jnp.maximum(jnp.max(s, axis=1, keepdims=True), sink_col)
    p = jnp.exp(s - m)
    denom = jnp.sum(p, axis=1, keepdims=True) + jnp.exp(sink_col - m)
    return jnp.dot(p.astype(BF16), v, preferred_element_type=F32) / denom


def _sink_column(sinks_ref, kvh, rows):
    blk = lax.shift_right_logical(lax.broadcasted_iota(jnp.int32, (GROUP * rows, 1), 0), rows.bit_length() - 1)
    col = jnp.zeros((GROUP * rows, 1), F32)
    for h in range(GROUP):
        col = jnp.where(blk == h, sinks_ref[kvh * GROUP + h], col)
    return col


def _swa_prompt_kernel(sinks_ref, q_ref, kprev_ref, kcur_ref, vprev_ref, vcur_ref, o_ref):
    kvh = pl.program_id(1)
    blk = pl.program_id(2)
    rows = WINDOW
    qexp = _expand_heads(q_ref[...]).astype(BF16)
    k = jnp.concatenate([kprev_ref[0], kcur_ref[0]], axis=0)
    v = jnp.concatenate([vprev_ref[0], vcur_ref[0]], axis=0)
    s = lax.dot_general(qexp, k, NT_DIMS, preferred_element_type=F32)
    qi = lax.bitwise_and(lax.broadcasted_iota(jnp.int32, s.shape, 0), rows - 1)
    kj = lax.broadcasted_iota(jnp.int32, s.shape, 1)
    first_key = jnp.where(blk > 0, 0, WINDOW)
    ok = (kj > qi) & (kj <= qi + WINDOW) & (kj >= first_key)
    o_rep = _sink_softmax_pv(s, ok, _sink_column(sinks_ref, kvh, rows), v)
    o_ref[...] = _collapse_heads(o_rep, rows).astype(BF16)


def _swa_prompt(q, k_rep, v_rep, sinks, batch, seq):
    t = batch * seq
    nb = seq // WINDOW
    cur = lambda b, h, i: (h, b * nb + i, 0)
    prev = lambda b, h, i: (h, b * nb + jnp.maximum(i - 1, 0), 0)
    qo = lambda b, h, i: (b * nb + i, h)
    return pl.pallas_call(
        _swa_prompt_kernel,
        grid=(batch, N_KV_HEADS, nb),
        in_specs=[pl.BlockSpec(memory_space=pltpu.SMEM),
                  pl.BlockSpec((WINDOW, GROUP_LANES), qo),
                  pl.BlockSpec((1, WINDOW, GROUP_LANES), prev), pl.BlockSpec((1, WINDOW, GROUP_LANES), cur),
                  pl.BlockSpec((1, WINDOW, GROUP_LANES), prev), pl.BlockSpec((1, WINDOW, GROUP_LANES), cur)],
        out_specs=pl.BlockSpec((WINDOW, GROUP_LANES), qo),
        out_shape=jax.ShapeDtypeStruct((t, Q_DIM), BF16),
        compiler_params=_params("arbitrary", "arbitrary", "arbitrary"), name="swa_prompt",
    )(sinks, q, k_rep, k_rep, v_rep, v_rep)


def _swa_sample_kernel(seqs, nq, sinks_ref, q_ref, kn_ref, vn_ref, ck_ref, cv_ref, o_ref, ko_ref, vo_ref):
    win = ck_ref.shape[1]

    def one_seq(s, carry):
        r0 = pl.multiple_of(s * nq, nq)
        q = q_ref[pl.ds(r0, nq), :]
        kn = kn_ref[pl.ds(r0, nq), :]
        vn = vn_ref[pl.ds(r0, nq), :]
        ck = ck_ref[s]
        cv = cv_ref[s]
        ko_ref[s] = jnp.concatenate([ck[nq:], kn], axis=0)
        vo_ref[s] = jnp.concatenate([cv[nq:], vn], axis=0)
        kk = jnp.concatenate([ck, kn], axis=0)
        vv = jnp.concatenate([cv, vn], axis=0)
        outs = []
        for kvh in range(N_KV_HEADS):
            qexp = _expand_heads(q[:, kvh * GROUP_LANES:(kvh + 1) * GROUP_LANES]).astype(BF16)
            k = _replicate_kv(kk, kvh).astype(BF16)
            v = _replicate_kv(vv, kvh).astype(BF16)
            sc = lax.dot_general(qexp, k, NT_DIMS, preferred_element_type=F32)
            qi = lax.bitwise_and(lax.broadcasted_iota(jnp.int32, sc.shape, 0), nq - 1)
            kj = lax.broadcasted_iota(jnp.int32, sc.shape, 1)
            ok = ((kj < win) & (kj > qi + win - WINDOW)) | ((kj >= win) & (kj - win <= qi))
            o_rep = _sink_softmax_pv(sc, ok, _sink_column(sinks_ref, kvh, nq), v)
            outs.append(_collapse_heads(o_rep, nq))
        o_ref[pl.ds(r0, nq), :] = jnp.concatenate(outs, axis=1).astype(BF16)
        return carry

    lax.fori_loop(0, seqs, one_seq, 0)


def _swa_sample(q, k, v, cache_k, cache_v, sinks, row0, nq):
    db, win = cache_k.shape[0], cache_k.shape[1]
    seqs = 8
    rows = seqs * nq
    tok = lambda g: (row0 // rows + g, 0)
    cache = lambda g: (g, 0, 0)
    return pl.pallas_call(
        functools.partial(_swa_sample_kernel, seqs, nq),
        grid=(db // seqs,),
        in_specs=[pl.BlockSpec(memory_space=pltpu.SMEM),
                  pl.BlockSpec((rows, Q_DIM), tok), pl.BlockSpec((rows, KV_DIM), tok),
                  pl.BlockSpec((rows, KV_DIM), tok),
                  pl.BlockSpec((seqs, win, KV_DIM), cache), pl.BlockSpec((seqs, win, KV_DIM), cache)],
        out_specs=[pl.BlockSpec((rows, Q_DIM), lambda g: (g, 0)),
                   pl.BlockSpec((seqs, win, KV_DIM), cache), pl.BlockSpec((seqs, win, KV_DIM), cache)],
        out_shape=[jax.ShapeDtypeStruct((db * nq, Q_DIM), BF16),
                   jax.ShapeDtypeStruct((db, win, KV_DIM), F32), jax.ShapeDtypeStruct((db, win, KV_DIM), F32)],
        compiler_params=_params("arbitrary"), name="swa_sample",
    )(sinks, q, k, v, cache_k.reshape(db, win, KV_DIM), cache_v.reshape(db, win, KV_DIM))


def _top3_rows(gate, n_valid):
    row = lax.broadcasted_iota(jnp.int32, gate.shape, 0)
    row_f = row.astype(F32)
    valid = row < n_valid
    g = jnp.where(valid, gate, NEG_INF)
    sel = jnp.zeros(gate.shape, F32)
    for _ in range(MOBA_TOPK):
        mx = jnp.max(g, axis=0, keepdims=True)
        first = jnp.min(jnp.where(g == mx, row_f, 1e9), axis=0, keepdims=True)
        hit = row_f == first
        sel = jnp.where(jnp.logical_and(hit, valid), 1.0, sel)
        g = jnp.where(hit, BELOW_NEG_INF, g)
    return sel


def _moba_prompt_kernel(qt_ref, k_ref, vt_ref, kmean_ref, o_ref, sel_ref, m_ref, l_ref, acc_ref):
    i = pl.program_id(2)
    qt = qt_ref[...]
    row_head = lax.shift_right_logical(lax.broadcasted_iota(jnp.int32, qt.shape, 0), 6)
    qexp32 = jnp.concatenate([jnp.where(row_head == h, qt, 0.0) for h in range(GROUP)], axis=1)
    qexp = qexp32.astype(BF16)
    gate = jnp.dot(kmean_ref[0, 0], qexp32, preferred_element_type=F32, precision=lax.Precision.HIGHEST)
    sel_ref[...] = _top3_rows(gate, i)

    def scores(j):
        start = pl.multiple_of(j * MOBA_BLOCK, MOBA_BLOCK)
        return jnp.dot(k_ref[0, pl.ds(start, MOBA_BLOCK), :], qexp, preferred_element_type=F32)

    s = scores(i)
    key = lax.broadcasted_iota(jnp.int32, s.shape, 0)
    qry = lax.bitwise_and(lax.broadcasted_iota(jnp.int32, s.shape, 1), MOBA_BLOCK - 1)
    s = jnp.where(key <= qry, s, NEG_INF)
    m = jnp.max(s, axis=0, keepdims=True)
    p = jnp.exp(s - m)
    m_ref[...] = m
    l_ref[...] = jnp.sum(p, axis=0, keepdims=True)
    acc_ref[...] = jnp.dot(vt_ref[0, i], p.astype(BF16), preferred_element_type=F32)

    def past_block(j, carry):
        s = scores(j)
        chosen = sel_ref[pl.ds(j, 1), :] > 0.0
        m_old = m_ref[...]
        m_new = jnp.where(chosen, jnp.maximum(m_old, jnp.max(s, axis=0, keepdims=True)), m_old)
        alpha = jnp.exp(m_old - m_new)
        p = jnp.exp(s - jnp.where(chosen, m_new, -NEG_INF))
        m_ref[...] = m_new
        l_ref[...] = alpha * l_ref[...] + jnp.sum(p, axis=0, keepdims=True)
        acc_ref[...] = alpha * acc_ref[...] + jnp.dot(vt_ref[0, j], p.astype(BF16), preferred_element_type=F32)
        return carry

    lax.fori_loop(0, i, past_block, 0)
    ot = acc_ref[...] / l_ref[...]
    ot = jnp.concatenate([ot[:, h * MOBA_BLOCK:(h + 1) * MOBA_BLOCK] for h in range(GROUP)], axis=0)
    o_ref[...] = ot.T.astype(BF16)


def _moba_prompt(qt, k_rep, vt, kmean_rep, batch, seq):
    nb = seq // MOBA_BLOCK
    cols = GROUP * MOBA_BLOCK
    return pl.pallas_call(
        _moba_prompt_kernel,
        grid=(batch, N_KV_HEADS, nb),
        in_specs=[pl.BlockSpec((GROUP_LANES, MOBA_BLOCK), lambda b, h, i: (h, b * nb + i)),
                  pl.BlockSpec((1, seq, GROUP_LANES), lambda b, h, i: (h, b, 0)),
                  pl.BlockSpec((1, nb, HEAD_DIM, MOBA_BLOCK), lambda b, h, i: (h, b, 0, 0)),
                  pl.BlockSpec((1, 1, nb, GROUP_LANES), lambda b, h, i: (b, h, 0, 0))],
        out_specs=pl.BlockSpec((MOBA_BLOCK, GROUP_LANES), lambda b, h, i: (b * nb + i, h)),
        out_shape=jax.ShapeDtypeStruct((batch * seq, Q_DIM), BF16),
        scratch_shapes=[pltpu.VMEM((nb, cols), F32), pltpu.VMEM((1, cols), F32), pltpu.VMEM((1, cols), F32),
                        pltpu.VMEM((HEAD_DIM, cols), F32)],
        compiler_params=_params("arbitrary", "arbitrary", "arbitrary"), name="moba_prompt",
    )(qt, k_rep, vt, kmean_rep)


def _moba_sample_kernel(nq, n_chunks, pt_ref, qexp_ref, kn_ref, vn_ref, *refs):
    del pt_ref
    pages = PAGES_PER_STEP
    k_refs = refs[:pages]
    v_refs = refs[pages:2 * pages]
    o_ref = refs[2 * pages]
    s_ref, ksum_ref, acc_ref, l_ref = refs[2 * pages + 1:]
    ph = pl.program_id(1)
    c = pl.program_id(2)
    keys = pages * PAGE_SIZE
    blocks_per_step = keys // MOBA_BLOCK
    n_blocks = n_chunks * blocks_per_step
    rows = N_HEADS * nq
    col0 = pl.multiple_of(c * keys, keys)

    @pl.when((ph == 0) & (c == 0))
    def _():
        ksum_ref[...] = jnp.zeros(ksum_ref.shape, F32)

    @pl.when(ph == 0)
    def _():
        kt = jnp.concatenate([r[0] for r in k_refs], axis=1)
        s_ref[:, pl.ds(col0, keys)] = jnp.dot(qexp_ref[0].astype(BF16), kt.astype(BF16),
                                              preferred_element_type=F32)
        lane = lax.broadcasted_iota(jnp.int32, ksum_ref.shape, 1)
        ksum = ksum_ref[...]
        for n in range(blocks_per_step):
            blk_sum = jnp.sum(kt[:, n * MOBA_BLOCK:(n + 1) * MOBA_BLOCK], axis=1, keepdims=True)
            ksum = jnp.where(lane == c * blocks_per_step + n, blk_sum, ksum)
        ksum_ref[...] = ksum

    @pl.when((ph == 0) & (c == n_chunks - 1))
    def _():
        qexp32 = qexp_ref[0]
        kmean = ksum_ref[...] * (1.0 / MOBA_BLOCK)
        gate = jnp.dot(qexp32, kmean, preferred_element_type=F32, precision=lax.Precision.HIGHEST)
        sel = jnp.where(_top3_mask(gate, n_blocks), 1.0, 0.0)
        s_own = lax.dot_general(qexp32.astype(BF16), kn_ref[...].astype(BF16), NT_DIMS,
                                preferred_element_type=F32)
        qi = lax.bitwise_and(lax.broadcasted_iota(jnp.int32, s_own.shape, 0), nq - 1)
        kj = lax.broadcasted_iota(jnp.int32, s_own.shape, 1)
        s_own = jnp.where(kj <= qi, s_own, NEG_INF)
        m = jnp.max(s_own, axis=1, keepdims=True)
        for n in range(n_blocks):
            blk = jnp.where(sel[:, n:n + 1] > 0.0, s_ref[:, n * MOBA_BLOCK:(n + 1) * MOBA_BLOCK], NEG_INF)
            s_ref[:, n * MOBA_BLOCK:(n + 1) * MOBA_BLOCK] = blk
            m = jnp.maximum(m, jnp.max(blk, axis=1, keepdims=True))
        p_own = jnp.exp(s_own - m)
        l = jnp.sum(p_own, axis=1, keepdims=True)
        for n in range(n_blocks):
            p = jnp.exp(s_ref[:, n * MOBA_BLOCK:(n + 1) * MOBA_BLOCK] - m)
            s_ref[:, n * MOBA_BLOCK:(n + 1) * MOBA_BLOCK] = p
            l = l + jnp.sum(p, axis=1, keepdims=True)
        l_ref[...] = l
        acc_ref[...] = jnp.dot(p_own.astype(BF16), vn_ref[...].astype(BF16), preferred_element_type=F32)

    @pl.when(ph == 1)
    def _():
        vt = jnp.concatenate([r[0] for r in v_refs], axis=1).astype(BF16)
        acc_ref[...] += lax.dot_general(s_ref[:, pl.ds(col0, keys)].astype(BF16), vt, NT_DIMS,
                                        preferred_element_type=F32)

    @pl.when((ph == 1) & (c == n_chunks - 1))
    def _():
        o_ref[0] = acc_ref[...] / l_ref[...]


def _moba_sample(qexp, k, v, pool_k, pool_v, page_table, row0, nq):
    db, n_pages = page_table.shape
    n_pool = pool_k.shape[0]
    pages = PAGES_PER_STEP
    n_chunks = n_pages // pages
    rows = N_HEADS * nq
    pk = pool_k.transpose(0, 2, 3, 1).reshape(n_pool, KV_DIM, PAGE_SIZE)
    pv = pool_v.transpose(0, 2, 3, 1).reshape(n_pool, KV_DIM, PAGE_SIZE)
    last = n_chunks - 1

    def k_map(p, b, ph, c, pt):
        return (pt[b, jnp.where(ph == 0, c, last) * pages + p], 0, 0)

    def v_map(p, b, ph, c, pt):
        return (pt[b, jnp.where(ph == 0, 0, c) * pages + p], 0, 0)

    page = (1, KV_DIM, PAGE_SIZE)
    new = lambda b, ph, c, pt: (row0 // nq + b, 0)
    grid_spec = pltpu.PrefetchScalarGridSpec(
        num_scalar_prefetch=1,
        grid=(db, 2, n_chunks),
        in_specs=[pl.BlockSpec((1, rows, KV_DIM), lambda b, ph, c, pt: (b, 0, 0)),
                  pl.BlockSpec((nq, KV_DIM), new), pl.BlockSpec((nq, KV_DIM), new)]
        + [pl.BlockSpec(page, functools.partial(k_map, p)) for p in range(pages)]
        + [pl.BlockSpec(page, functools.partial(v_map, p)) for p in range(pages)],
        out_specs=pl.BlockSpec((1, rows, KV_DIM), lambda b, ph, c, pt: (b, 0, 0)),
        scratch_shapes=[pltpu.VMEM((rows, n_pages * PAGE_SIZE), F32),
                        pltpu.VMEM((KV_DIM, LANES), F32),
                        pltpu.VMEM((rows, KV_DIM), F32), pltpu.VMEM((rows, 1), F32)])
    return pl.pallas_call(
        functools.partial(_moba_sample_kernel, nq, n_chunks),
        grid_spec=grid_spec,
        out_shape=jax.ShapeDtypeStruct((db, rows, KV_DIM), F32),
        compiler_params=_params("arbitrary", "arbitrary", "arbitrary"), name="moba_sample",
    )(page_table, qexp, k, v, *([pk] * pages), *([pv] * pages))


def _proj_route_kernel(o_ref, x_ref, wo_ref, bo_ref, g_ref, wr_ref, br_ref, x1_ref, h_ref, route_ref):
    x1 = x_ref[...] + jnp.dot(o_ref[...], wo_ref[...], preferred_element_type=F32) + bo_ref[...]
    x1_ref[...] = x1
    h = _rms(x1, g_ref[...])
    h_ref[...] = h
    lg = jnp.dot(h, wr_ref[...], preferred_element_type=F32, precision=lax.Precision.HIGHEST) + br_ref[...]
    lane = lax.broadcasted_iota(jnp.int32, lg.shape, 1)
    lane_f = lane.astype(F32)
    is_group = lane < N_GROUPS
    gl = jnp.where(is_group, lg, NEG_INF)
    gmax = jnp.max(gl, axis=1, keepdims=True)
    grp = jnp.min(jnp.where(gl == gmax, lane_f, 1e9), axis=1, keepdims=True)
    g_w = 1.0 / jnp.sum(jnp.where(is_group, jnp.exp(gl - gmax), 0.0), axis=1, keepdims=True)
    lane_grp = lax.shift_right_logical(lane - N_GROUPS, 3).astype(F32)
    in_grp = (lane >= N_GROUPS) & (lane < N_GROUPS + N_EXPERTS) & (lane_grp == grp)
    el = jnp.where(in_grp, lg, NEG_INF)
    v1 = jnp.max(el, axis=1, keepdims=True)
    i1 = jnp.min(jnp.where(el == v1, lane_f, 1e9), axis=1, keepdims=True)
    el2 = jnp.where(lane_f == i1, BELOW_NEG_INF, el)
    v2 = jnp.max(el2, axis=1, keepdims=True)
    i2 = jnp.min(jnp.where(el2 == v2, lane_f, 1e9), axis=1, keepdims=True)
    e2w = jnp.exp(v2 - v1)
    w1 = g_w / (1.0 + e2w)
    w2 = g_w * e2w / (1.0 + e2w)
    route = jnp.where(lane == 0, i1 - N_GROUPS, 0.0)
    route = jnp.where(lane == 1, i2 - N_GROUPS, route)
    route = jnp.where(lane == 2, w1, route)
    route = jnp.where(lane == 3, w2, route)
    route_ref[...] = route


def _proj_route(o, x, wo, bo, g, w_gr, b_gr, w_er, b_er):
    t = x.shape[0]
    row = lambda i: (i, 0)
    fixed = lambda i: (0, 0)
    pad = LANES - N_GROUPS - N_EXPERTS
    wr = jnp.concatenate([w_gr, w_er, jnp.zeros((D_MODEL, pad), F32)], axis=1)
    br = jnp.concatenate([b_gr, b_er, jnp.zeros((pad,), F32)]).reshape(1, LANES)
    return pl.pallas_call(
        _proj_route_kernel,
        grid=(t // ROW_TILE,),
        in_specs=[pl.BlockSpec((ROW_TILE, Q_DIM), row), pl.BlockSpec((ROW_TILE, D_MODEL), row),
                  pl.BlockSpec((Q_DIM, D_MODEL), fixed), pl.BlockSpec((1, D_MODEL), fixed),
                  pl.BlockSpec((1, D_MODEL), fixed), pl.BlockSpec((D_MODEL, LANES), fixed),
                  pl.BlockSpec((1, LANES), fixed)],
        out_specs=[pl.BlockSpec((ROW_TILE, D_MODEL), row), pl.BlockSpec((ROW_TILE, D_MODEL), row),
                   pl.BlockSpec((ROW_TILE, LANES), row)],
        out_shape=[jax.ShapeDtypeStruct((t, D_MODEL), F32), jax.ShapeDtypeStruct((t, D_MODEL), F32),
                   jax.ShapeDtypeStruct((t, LANES), F32)],
        compiler_params=_params("arbitrary"), name="proj_route",
    )(o, x, wo.astype(BF16), bo.reshape(1, D_MODEL), g.reshape(1, D_MODEL), wr, br)


def _expert_kernel(n_steps, blk_e_ref, nvalid_ref, src_ref, src_next_ref, dst_ref, wrow_ref, h_hbm,
                   wg_ref, wu_ref, wd_ref, y_hbm, xbuf, ybuf, wg_bf, wu_bf, wd_bf, gsem, ssem):
    i = pl.program_id(0)
    slot = lax.rem(i, 2)
    rows = EXPERT_ROWS
    nvalid = nvalid_ref[i]
    nvalid_next = jnp.where(i + 1 < n_steps, nvalid_ref[jnp.minimum(i + 1, n_steps - 1)], 0)
    nvalid_prev2 = jnp.where(i >= 2, nvalid_ref[jnp.maximum(i - 2, 0)], 0)

    def row_copy_in(idx_ref, r, s):
        return pltpu.make_async_copy(h_hbm.at[pl.ds(idx_ref[0, r], 1)], xbuf.at[s, pl.ds(r, 1)], gsem.at[s])

    def row_copy_out(r, s):
        return pltpu.make_async_copy(ybuf.at[s, pl.ds(r, 1)], y_hbm.at[pl.ds(dst_ref[0, r], 1)], ssem.at[s])

    def for_rows(fn, n=None):
        def body(r, carry):
            if n is None:
                fn(r)
            else:
                pl.when(r < n)(lambda: fn(r))
            return carry
        lax.fori_loop(0, rows, body, 0, unroll=8)

    @pl.when((i == 0) & (nvalid > 0))
    def _():
        for_rows(lambda r: row_copy_in(src_ref, r, 0).start())

    @pl.when(nvalid_next > 0)
    def _():
        for_rows(lambda r: row_copy_in(src_next_ref, r, 1 - slot).start())

    @pl.when(nvalid_prev2 > 0)
    def _():
        for_rows(lambda r: row_copy_out(r, slot).wait(), nvalid_prev2)

    @pl.when(nvalid > 0)
    def _():
        e_changed = jnp.logical_or(i == 0, blk_e_ref[i] != blk_e_ref[jnp.maximum(i - 1, 0)])

        @pl.when(e_changed)
        def _():
            wg_bf[...] = wg_ref[0, 0].astype(BF16)
            wu_bf[...] = wu_ref[0, 0].astype(BF16)
            wd_bf[...] = wd_ref[0, 0].astype(BF16)

        for_rows(lambda r: row_copy_in(src_ref, r, slot).wait())
        x = xbuf[slot].astype(BF16)
        gate = jnp.dot(x, wg_bf[...], preferred_element_type=F32)
        up = jnp.dot(x, wu_bf[...], preferred_element_type=F32)
        hid = (gate * jax.nn.sigmoid(gate) * up).astype(BF16)
        y = jnp.dot(hid, wd_bf[...], preferred_element_type=F32)
        ybuf[slot] = y * wrow_ref[...]
        for_rows(lambda r: row_copy_out(r, slot).start(), nvalid)


def _expert_ffn(h, e_idx, e_w, w_gate, w_up, w_down, layer):
    t = h.shape[0]
    a = t * EXPERT_TOPK
    rows = EXPERT_ROWS
    n_blk = -(-a // rows) + N_EXPERTS
    n_steps = n_blk + 2
    n_rows = n_steps * rows
    flat_e = e_idx.reshape(a)
    order = jnp.argsort(flat_e).astype(jnp.int32)
    counts = jnp.sum(flat_e[:, None] == jnp.arange(N_EXPERTS, dtype=jnp.int32)[None, :], axis=0, dtype=jnp.int32)
    padded = (counts + rows - 1) // rows * rows
    pad_end = jnp.cumsum(padded)
    pad_start = pad_end - padded
    start = jnp.cumsum(counts) - counts
    blk_first = jnp.arange(n_steps, dtype=jnp.int32) * rows
    blk_e = jnp.minimum(jnp.sum(blk_first[:, None] >= pad_end[None, :], axis=1), N_EXPERTS - 1).astype(jnp.int32)
    rank0 = blk_first - pad_start[blk_e]
    nvalid = jnp.clip(counts[blk_e] - rank0, 0, rows).astype(jnp.int32)
    within = jnp.arange(rows, dtype=jnp.int32)[None, :]
    valid = within < nvalid[:, None]
    assign = order[jnp.clip((start[blk_e] + rank0)[:, None] + within, 0, a - 1)]
    tok = assign // EXPERT_TOPK
    src = jnp.where(valid, tok, 0)
    dst = jnp.where(valid, (assign % EXPERT_TOPK) * t + tok, 0)
    wrow = jnp.where(valid, e_w.reshape(a)[assign], 0.0)
    idx_blk = lambda i, be, na: (i, 0, 0)
    idx_next = lambda i, be, na: (jnp.minimum(i + 1, n_steps - 1), 0, 0)
    w_in = lambda i, be, na: (layer, be[i], 0, 0)
    grid_spec = pltpu.PrefetchScalarGridSpec(
        num_scalar_prefetch=2,
        grid=(n_steps,),
        in_specs=[pl.BlockSpec((1, 1, rows), idx_blk, memory_space=pltpu.SMEM),
                  pl.BlockSpec((1, 1, rows), idx_next, memory_space=pltpu.SMEM),
                  pl.BlockSpec((1, 1, rows), idx_blk, memory_space=pltpu.SMEM),
                  pl.BlockSpec((rows, 1), lambda i, be, na: (i, 0)),
                  pl.BlockSpec(memory_space=pl.ANY),
                  pl.BlockSpec((1, 1, D_MODEL, EXPERT_FF), w_in), pl.BlockSpec((1, 1, D_MODEL, EXPERT_FF), w_in),
                  pl.BlockSpec((1, 1, EXPERT_FF, D_MODEL), w_in)],
        out_specs=pl.BlockSpec(memory_space=pl.ANY),
        scratch_shapes=[pltpu.VMEM((2, rows, D_MODEL), F32), pltpu.VMEM((2, rows, D_MODEL), F32),
                        pltpu.VMEM((D_MODEL, EXPERT_FF), BF16), pltpu.VMEM((D_MODEL, EXPERT_FF), BF16),
                        pltpu.VMEM((EXPERT_FF, D_MODEL), BF16),
                        pltpu.SemaphoreType.DMA((2,)), pltpu.SemaphoreType.DMA((2,))])

    def body(blk_e_ref, nvalid_ref, src_ref, src_next_ref, dst_ref, *rest):
        _expert_kernel(n_steps, blk_e_ref, nvalid_ref, src_ref.at[0], src_next_ref.at[0], dst_ref.at[0], *rest)

    return pl.pallas_call(
        body, grid_spec=grid_spec,
        out_shape=jax.ShapeDtypeStruct((EXPERT_TOPK * t, D_MODEL), F32),
        compiler_params=_params("arbitrary"), name="expert_ffn",
    )(blk_e, nvalid, src.reshape(n_steps, 1, rows), src.reshape(n_steps, 1, rows),
      dst.reshape(n_steps, 1, rows), wrow.reshape(n_rows, 1), h, w_gate, w_up, w_down)


def _final_kernel(x_ref, y0_ref, y1_ref, g_ref, o_ref):
    o_ref[...] = _rms(x_ref[...] + y0_ref[...] + y1_ref[...], g_ref[...])


def _final_norm(x, y2, g, row0, n_rows):
    t = x.shape[0]
    off = row0 // ROW_TILE
    return pl.pallas_call(
        _final_kernel,
        grid=(n_rows // ROW_TILE,),
        in_specs=[pl.BlockSpec((ROW_TILE, D_MODEL), lambda i: (i + off, 0)),
                  pl.BlockSpec((ROW_TILE, D_MODEL), lambda i: (i + off, 0)),
                  pl.BlockSpec((ROW_TILE, D_MODEL), lambda i: (i + off + t // ROW_TILE, 0)),
                  pl.BlockSpec((1, D_MODEL), lambda i: (0, 0))],
        out_specs=pl.BlockSpec((ROW_TILE, D_MODEL), lambda i: (i, 0)),
        out_shape=jax.ShapeDtypeStruct((n_rows, D_MODEL), F32),
        compiler_params=_params("arbitrary"), name="final_norm",
    )(x, y2, y2, g.reshape(1, D_MODEL))


def _rope_tables(pos):
    half = HEAD_DIM // 2
    inv = ROPE_THETA ** (-jnp.arange(half, dtype=F32) / half)
    ang = pos.astype(F32)[:, None] * inv[None, :]
    cos = jnp.cos(ang)
    sin = jnp.sin(ang)
    reps = LANES // HEAD_DIM
    return jnp.tile(jnp.concatenate([cos, cos], axis=1), (1, reps)), jnp.tile(jnp.concatenate([-sin, sin], axis=1), (1, reps))


def _moe(o, x, wo, bo, g, w_gr, b_gr, w_er, b_er, w_gate, w_up, w_down, layer):
    x1, h, route = _proj_route(o, x, wo, bo, g, w_gr, b_gr, w_er, b_er)
    e_idx = jnp.clip(route[:, :EXPERT_TOPK].astype(jnp.int32), 0, N_EXPERTS - 1)
    e_w = route[:, EXPERT_TOPK:2 * EXPERT_TOPK]
    return x1, _expert_ffn(h, e_idx, e_w, w_gate, w_up, w_down, layer)


def kernel(x_prompt, x_sample, cache_swa_k, cache_swa_v, cache_moba_k, cache_moba_v, page_table, norm_mix, norm_ffn, norm_final, swa_w_qkv, swa_b_qkv, swa_sinks, swa_w_o, swa_b_o, moba_w_qkv, moba_w_o, w_group_router, b_group_router, w_expert_router, b_expert_router, w_gate, w_up, w_down):
    batch, seq, _ = x_prompt.shape
    db, nq, _ = x_sample.shape
    past_len = page_table.shape[1] * PAGE_SIZE
    tp = batch * seq
    ts = db * nq
    t = tp + ts
    x = jnp.concatenate([x_prompt.reshape(tp, D_MODEL), x_sample.reshape(ts, D_MODEL)], axis=0)
    pos = jnp.concatenate([jnp.tile(jnp.arange(seq), batch), jnp.tile(past_len + jnp.arange(nq), db)])
    cos_t, sin_t = _rope_tables(pos)

    q, k0, v0, k_rep, v_rep = _qkv_rope(x, [], norm_mix[0], swa_w_qkv[0], swa_b_qkv[0], cos_t, sin_t, False)
    o_p = _swa_prompt(q, k_rep, v_rep, swa_sinks[0], batch, seq)
    o_s, swa_ks, swa_vs = _swa_sample(q, k0, v0, cache_swa_k[0], cache_swa_v[0], swa_sinks[0], tp, nq)
    o = jnp.concatenate([o_p, o_s], axis=0)
    x1, y2 = _moe(o, x, swa_w_o[0], swa_b_o[0], norm_ffn[0], w_group_router[0], b_group_router[0],
                  w_expert_router[0], b_expert_router[0], w_gate, w_up, w_down, 0)

    x, q, k1, v1, k_rep, kmean, qt, vt = _qkv_rope(
        x1, [(y2, 0), (y2, t)], norm_mix[1], moba_w_qkv[0], jnp.zeros((QKV_DIM,), F32), cos_t, sin_t, True)
    nb = seq // MOBA_BLOCK
    kmean_rep = jnp.tile(kmean[:batch * nb, 0].reshape(batch, nb, N_KV_HEADS, 1, HEAD_DIM), (1, 1, 1, GROUP, 1))
    kmean_rep = kmean_rep.reshape(batch, nb, N_KV_HEADS, GROUP_LANES).transpose(0, 2, 1, 3)
    o_p = _moba_prompt(qt, k_rep, vt, kmean_rep, batch, seq)
    qs = q[tp:].reshape(db, nq, N_HEADS, 1, HEAD_DIM)
    slot = (jnp.arange(N_HEADS) // GROUP)[:, None] == jnp.arange(N_KV_HEADS)[None, :]
    qexp = jnp.where(slot[None, None, :, :, None], qs, 0.0).transpose(0, 2, 1, 3, 4).reshape(db, N_HEADS * nq, KV_DIM)
    o_s = _moba_sample(qexp, k1, v1, cache_moba_k[0], cache_moba_v[0], page_table, tp, nq)
    o_s = o_s.reshape(db, N_HEADS, nq, N_KV_HEADS, HEAD_DIM)
    o_s = jnp.sum(jnp.where(slot[None, :, None, :, None], o_s, 0.0), axis=3).transpose(0, 2, 1, 3)
    o = jnp.concatenate([o_p, o_s.reshape(ts, Q_DIM).astype(BF16)], axis=0)
    x1, y2 = _moe(o, x, moba_w_o[0], jnp.zeros((D_MODEL,), F32), norm_ffn[1], w_group_router[1],
                  b_group_router[1], w_expert_router[1], b_expert_router[1], w_gate, w_up, w_down, 1)

    y_prompt = _final_norm(x1, y2, norm_final, 0, tp).reshape(batch, seq, D_MODEL)
    y_sample = _final_norm(x1, y2, norm_final, tp, ts).reshape(db, nq, D_MODEL)
    win = min(WINDOW, seq)
    kv5 = lambda a, b, s: a.reshape(1, b, s, N_KV_HEADS, HEAD_DIM)
    k0p = k0[:tp].reshape(batch, seq, KV_DIM)[:, seq - win:]
    v0p = v0[:tp].reshape(batch, seq, KV_DIM)[:, seq - win:]
    return (y_prompt, y_sample, kv5(k0p, batch, win), kv5(v0p, batch, win),
            kv5(swa_ks, db, swa_ks.shape[1]), kv5(swa_vs, db, swa_vs.shape[1]),
            kv5(k1[:tp], batch, seq), kv5(v1[:tp], batch, seq), kv5(k1[tp:], db, nq), kv5(v1[tp:], db, nq))
```

```python
import functools

import jax
import jax.numpy as jnp
from jax import lax
from jax.experimental import pallas as pl
from jax.experimental.pallas import tpu as pltpu

D_MODEL = 1024
N_HEADS = 16
N_KV_HEADS = 4
HEAD_DIM = 64
GROUP = N_HEADS // N_KV_HEADS
Q_DIM = N_HEADS * HEAD_DIM
KV_DIM = N_KV_HEADS * HEAD_DIM
QKV_DIM = Q_DIM + 2 * KV_DIM
ROPE_THETA = 10000.0
WINDOW = 128
MOBA_BLOCK = 256
MOBA_TOPK = 3
PAGE_SIZE = 128
N_GROUPS = 8
EXPERTS_PER_GROUP = 8
N_EXPERTS = N_GROUPS * EXPERTS_PER_GROUP
EXPERT_TOPK = 2
EXPERT_FF = D_MODEL // 4
RMS_EPS = 1e-6
NEG_INF = -1e30
BELOW_NEG_INF = -3e38

LANES = 128
GROUP_LANES = GROUP * HEAD_DIM
ROW_TILE = 256
EXPERT_ROWS = 256
PAGES_PER_STEP = 16
VMEM_LIMIT = 56 * 1024 * 1024

F32 = jnp.float32
BF16 = jnp.bfloat16
NT_DIMS = (((1,), (1,)), ((), ()))


def _params(*sem):
    return pltpu.CompilerParams(dimension_semantics=sem, vmem_limit_bytes=VMEM_LIMIT)


def _rms(x, g):
    return x * lax.rsqrt(jnp.mean(x * x, axis=-1, keepdims=True) + RMS_EPS) * g


def _replicate_head(x128, odd):
    swapped = pltpu.roll(x128, HEAD_DIM, axis=1)
    low = lax.broadcasted_iota(jnp.int32, x128.shape, 1) < HEAD_DIM
    return jnp.where(low != odd, x128, swapped)


def _replicate_kv(x256, kvh):
    col = x256[:, (kvh // 2) * LANES:(kvh // 2 + 1) * LANES]
    rep = _replicate_head(col, kvh % 2 == 1)
    return jnp.concatenate([rep, rep], axis=1)


def _expand_heads(q):
    head = lax.shift_right_logical(lax.broadcasted_iota(jnp.int32, q.shape, 1), 6)
    return jnp.concatenate([jnp.where(head == h, q, 0.0) for h in range(GROUP)], axis=0)


def _collapse_heads(o_rep, rows):
    head = lax.shift_right_logical(lax.broadcasted_iota(jnp.int32, (rows, GROUP_LANES), 1), 6)
    out = jnp.zeros((rows, GROUP_LANES), F32)
    for h in range(GROUP):
        out = jnp.where(head == h, o_rep[h * rows:(h + 1) * rows], out)
    return out


def _top3_mask(gate, n_valid):
    lane = lax.broadcasted_iota(jnp.int32, gate.shape, 1)
    lane_f = lane.astype(F32)
    valid = lane < n_valid
    g = jnp.where(valid, gate, NEG_INF)
    sel = jnp.zeros(gate.shape, jnp.bool_)
    for _ in range(MOBA_TOPK):
        mx = jnp.max(g, axis=1, keepdims=True)
        first = jnp.min(jnp.where(g == mx, lane_f, 1e9), axis=1, keepdims=True)
        hit = lane_f == first
        sel = jnp.logical_or(sel, jnp.logical_and(hit, valid))
        g = jnp.where(hit, BELOW_NEG_INF, g)
    return sel


def _qkv_kernel(n_add, moba, *refs):
    refs = list(refs)
    x_ref = refs.pop(0)
    add_refs = [refs.pop(0) for _ in range(n_add)]
    g_ref, w_ref, b_ref, cos_ref, sin_ref = refs[:5]
    outs = refs[5:]
    x = x_ref[...]
    for r in add_refs:
        x = x + r[...]
    if n_add:
        outs.pop(0)[...] = x
    q_ref, k_ref, v_ref, krep_ref = outs[:4]
    h = _rms(x, g_ref[...]).astype(BF16)
    y = jnp.dot(h, w_ref[...], preferred_element_type=F32) + b_ref[...]
    cos = cos_ref[...]
    sin = sin_ref[...]
    low = lax.bitwise_and(lax.broadcasted_iota(jnp.int32, cos.shape, 1), HEAD_DIM - 1) < HEAD_DIM // 2

    def rope(slab):
        partner = jnp.where(low, pltpu.roll(slab, LANES - HEAD_DIM // 2, axis=1),
                            pltpu.roll(slab, HEAD_DIM // 2, axis=1))
        return slab * cos + partner * sin

    scale = HEAD_DIM ** -0.5
    for c in range(Q_DIM // LANES):
        q_ref[:, c * LANES:(c + 1) * LANES] = rope(y[:, c * LANES:(c + 1) * LANES]) * scale
    k = jnp.concatenate([rope(y[:, Q_DIM + c * LANES:Q_DIM + (c + 1) * LANES])
                         for c in range(KV_DIM // LANES)], axis=1)
    v = y[:, Q_DIM + KV_DIM:]
    k_ref[...] = k
    v_ref[...] = v
    for kvh in range(N_KV_HEADS):
        krep_ref[kvh] = _replicate_kv(k, kvh).astype(BF16)
    if not moba:
        for kvh in range(N_KV_HEADS):
            outs[4][kvh] = _replicate_kv(v, kvh).astype(BF16)
    else:
        kmean_ref, qt_ref, vt_ref = outs[4:]
        kmean_ref[0] = jnp.broadcast_to(jnp.mean(k, axis=0, keepdims=True), (8, KV_DIM))
        qt_ref[...] = q_ref[...].T
        vt_ref[:, 0] = v.T.reshape(N_KV_HEADS, HEAD_DIM, ROW_TILE).astype(BF16)


def _qkv_rope(x, adds, g, w, b, cos_t, sin_t, moba):
    t = x.shape[0]
    nt = t // ROW_TILE
    row = lambda i: (i, 0)
    fixed = lambda i: (0, 0)
    in_specs = [pl.BlockSpec((ROW_TILE, D_MODEL), row)]
    args = [x]
    for arr, off in adds:
        in_specs.append(pl.BlockSpec((ROW_TILE, D_MODEL), functools.partial(lambda o, i: (i + o, 0), off // ROW_TILE)))
        args.append(arr)
    in_specs += [pl.BlockSpec((1, D_MODEL), fixed), pl.BlockSpec((D_MODEL, QKV_DIM), fixed),
                 pl.BlockSpec((1, QKV_DIM), fixed), pl.BlockSpec((ROW_TILE, LANES), row),
                 pl.BlockSpec((ROW_TILE, LANES), row)]
    args += [g.reshape(1, D_MODEL), w.astype(BF16), b.reshape(1, QKV_DIM), cos_t, sin_t]
    out_shape, out_specs = [], []
    if adds:
        out_shape.append(jax.ShapeDtypeStruct((t, D_MODEL), F32))
        out_specs.append(pl.BlockSpec((ROW_TILE, D_MODEL), row))
    rep_shape = jax.ShapeDtypeStruct((N_KV_HEADS, t, GROUP_LANES), BF16)
    rep_spec = pl.BlockSpec((N_KV_HEADS, ROW_TILE, GROUP_LANES), lambda i: (0, i, 0))
    out_shape += [jax.ShapeDtypeStruct((t, Q_DIM), F32), jax.ShapeDtypeStruct((t, KV_DIM), F32),
                  jax.ShapeDtypeStruct((t, KV_DIM), F32), rep_shape]
    out_specs += [pl.BlockSpec((ROW_TILE, Q_DIM), row), pl.BlockSpec((ROW_TILE, KV_DIM), row),
                  pl.BlockSpec((ROW_TILE, KV_DIM), row), rep_spec]
    if not moba:
        out_shape.append(rep_shape)
        out_specs.append(rep_spec)
    else:
        out_shape += [jax.ShapeDtypeStruct((nt, 8, KV_DIM), F32), jax.ShapeDtypeStruct((Q_DIM, t), F32),
                      jax.ShapeDtypeStruct((N_KV_HEADS, nt, HEAD_DIM, ROW_TILE), BF16)]
        out_specs += [pl.BlockSpec((1, 8, KV_DIM), lambda i: (i, 0, 0)),
                      pl.BlockSpec((Q_DIM, ROW_TILE), lambda i: (0, i)),
                      pl.BlockSpec((N_KV_HEADS, 1, HEAD_DIM, ROW_TILE), lambda i: (0, i, 0, 0))]
    return pl.pallas_call(
        functools.partial(_qkv_kernel, len(adds), moba),
        grid=(nt,), in_specs=in_specs, out_specs=out_specs, out_shape=out_shape,
        compiler_params=_params("arbitrary"), name="qkv_rope")(*args)


def _sink_softmax_pv(s, ok, sink_col, v):
    s = jnp.where(ok, s, NEG_INF)
    m = jnp.maximum(jnp.max(s, axis=1, keepdims=True), sink_col)
    p = jnp.exp(s - m)
    denom = jnp.sum(p, axis=1, keepdims=True) + jnp.exp(sink_col - m)
    return jnp.dot(p.astype(BF16), v, preferred_element_type=F32) / denom


def _sink_column(sinks_ref, kvh, rows):
    blk = lax.shift_right_logical(lax.broadcasted_iota(jnp.int32, (GROUP * rows, 1), 0), rows.bit_length() - 1)
    col = jnp.zeros((GROUP * rows, 1), F32)
    for h in range(GROUP):
        col = jnp.where(blk == h, sinks_ref[kvh * GROUP + h], col)
    return col


def _swa_prompt_kernel(sinks_ref, q_ref, kprev_ref, kcur_ref, vprev_ref, vcur_ref, o_ref):
    kvh = pl.program_id(1)
    blk = pl.program_id(2)
    rows = WINDOW
    qexp = _expand_heads(q_ref[...]).astype(BF16)
    k = jnp.concatenate([kprev_ref[0], kcur_ref[0]], axis=0)
    v = jnp.concatenate([vprev_ref[0], vcur_ref[0]], axis=0)
    s = lax.dot_general(qexp, k, NT_DIMS, preferred_element_type=F32)
    qi = lax.bitwise_and(lax.broadcasted_iota(jnp.int32, s.shape, 0), rows - 1)
    kj = lax.broadcasted_iota(jnp.int32, s.shape, 1)
    first_key = jnp.where(blk > 0, 0, WINDOW)
    ok = (kj > qi) & (kj <= qi + WINDOW) & (kj >= first_key)
    o_rep = _sink_softmax_pv(s, ok, _sink_column(sinks_ref, kvh, rows), v)
    o_ref[...] = _collapse_heads(o_rep, rows).astype(BF16)


def _swa_prompt(q, k_rep, v_rep, sinks, batch, seq):
    t = batch * seq
    nb = seq // WINDOW
    cur = lambda b, h, i: (h, b * nb + i, 0)
    prev = lambda b, h, i: (h, b * nb + jnp.maximum(i - 1, 0), 0)
    qo = lambda b, h, i: (b * nb + i, h)
    return pl.pallas_call(
        _swa_prompt_kernel,
        grid=(batch, N_KV_HEADS, nb),
        in_specs=[pl.BlockSpec(memory_space=pltpu.SMEM),
                  pl.BlockSpec((WINDOW, GROUP_LANES), qo),
                  pl.BlockSpec((1, WINDOW, GROUP_LANES), prev), pl.BlockSpec((1, WINDOW, GROUP_LANES), cur),
                  pl.BlockSpec((1, WINDOW, GROUP_LANES), prev), pl.BlockSpec((1, WINDOW, GROUP_LANES), cur)],
        out_specs=pl.BlockSpec((WINDOW, GROUP_LANES), qo),
        out_shape=jax.ShapeDtypeStruct((t, Q_DIM), BF16),
        compiler_params=_params("arbitrary", "arbitrary", "arbitrary"), name="swa_prompt",
    )(sinks, q, k_rep, k_rep, v_rep, v_rep)


def _swa_sample_kernel(seqs, nq, sinks_ref, q_ref, kn_ref, vn_ref, ck_ref, cv_ref, o_ref, ko_ref, vo_ref):
    win = ck_ref.shape[1]

    def one_seq(s, carry):
        r0 = pl.multiple_of(s * nq, nq)
        q = q_ref[pl.ds(r0, nq), :]
        kn = kn_ref[pl.ds(r0, nq), :]
        vn = vn_ref[pl.ds(r0, nq), :]
        ck = ck_ref[s]
        cv = cv_ref[s]
        ko_ref[s] = jnp.concatenate([ck[nq:], kn], axis=0)
        vo_ref[s] = jnp.concatenate([cv[nq:], vn], axis=0)
        kk = jnp.concatenate([ck, kn], axis=0)
        vv = jnp.concatenate([cv, vn], axis=0)
        outs = []
        for kvh in range(N_KV_HEADS):
            qexp = _expand_heads(q[:, kvh * GROUP_LANES:(kvh + 1) * GROUP_LANES]).astype(BF16)
            k = _replicate_kv(kk, kvh).astype(BF16)
            v = _replicate_kv(vv, kvh).astype(BF16)
            sc = lax.dot_general(qexp, k, NT_DIMS, preferred_element_type=F32)
            qi = lax.bitwise_and(lax.broadcasted_iota(jnp.int32, sc.shape, 0), nq - 1)
            kj = lax.broadcasted_iota(jnp.int32, sc.shape, 1)
            ok = ((kj < win) & (kj > qi + win - WINDOW)) | ((kj >= win) & (kj - win <= qi))
            o_rep = _sink_softmax_pv(sc, ok, _sink_column(sinks_ref, kvh, nq), v)
            outs.append(_collapse_heads(o_rep, nq))
        o_ref[pl.ds(r0, nq), :] = jnp.concatenate(outs, axis=1).astype(BF16)
        return carry

    lax.fori_loop(0, seqs, one_seq, 0)


def _swa_sample(q, k, v, cache_k, cache_v, sinks, row0, nq):
    db, win = cache_k.shape[0], cache_k.shape[1]
    seqs = 8
    rows = seqs * nq
    tok = lambda g: (row0 // rows + g, 0)
    cache = lambda g: (g, 0, 0)
    return pl.pallas_call(
        functools.partial(_swa_sample_kernel, seqs, nq),
        grid=(db // seqs,),
        in_specs=[pl.BlockSpec(memory_space=pltpu.SMEM),
                  pl.BlockSpec((rows, Q_DIM), tok), pl.BlockSpec((rows, KV_DIM), tok),
                  pl.BlockSpec((rows, KV_DIM), tok),
                  pl.BlockSpec((seqs, win, KV_DIM), cache), pl.BlockSpec((seqs, win, KV_DIM), cache)],
        out_specs=[pl.BlockSpec((rows, Q_DIM), lambda g: (g, 0)),
                   pl.BlockSpec((seqs, win, KV_DIM), cache), pl.BlockSpec((seqs, win, KV_DIM), cache)],
        out_shape=[jax.ShapeDtypeStruct((db * nq, Q_DIM), BF16),
                   jax.ShapeDtypeStruct((db, win, KV_DIM), F32), jax.ShapeDtypeStruct((db, win, KV_DIM), F32)],
        compiler_params=_params("arbitrary"), name="swa_sample",
    )(sinks, q, k, v, cache_k.reshape(db, win, KV_DIM), cache_v.reshape(db, win, KV_DIM))


def _top3_rows(gate, n_valid):
    row = lax.broadcasted_iota(jnp.int32, gate.shape, 0)
    row_f = row.astype(F32)
    valid = row < n_valid
    g = jnp.where(valid, gate, NEG_INF)
    sel = jnp.zeros(gate.shape, F32)
    for _ in range(MOBA_TOPK):
        mx = jnp.max(g, axis=0, keepdims=True)
        first = jnp.min(jnp.where(g == mx, row_f, 1e9), axis=0, keepdims=True)
        hit = row_f == first
        sel = jnp.where(jnp.logical_and(hit, valid), 1.0, sel)
        g = jnp.where(hit, BELOW_NEG_INF, g)
    return sel


def _moba_prompt_kernel(qt_ref, k_ref, vt_ref, kmean_ref, o_ref, sel_ref, acc_ref):
    i = pl.program_id(2)
    qt = qt_ref[...]
    row_head = lax.shift_right_logical(lax.broadcasted_iota(jnp.int32, qt.shape, 0), 6)
    qexp32 = jnp.concatenate([jnp.where(row_head == h, qt, 0.0) for h in range(GROUP)], axis=1)
    qexp = qexp32.astype(BF16)
    gate = jnp.dot(kmean_ref[0, 0], qexp32, preferred_element_type=F32, precision=lax.Precision.HIGHEST)
    sel_ref[...] = _top3_rows(gate, i)

    def scores(j):
        start = pl.multiple_of(j * MOBA_BLOCK, MOBA_BLOCK)
        return jnp.dot(k_ref[0, pl.ds(start, MOBA_BLOCK), :], qexp, preferred_element_type=F32)

    s = scores(i)
    key = lax.broadcasted_iota(jnp.int32, s.shape, 0)
    qry = lax.bitwise_and(lax.broadcasted_iota(jnp.int32, s.shape, 1), MOBA_BLOCK - 1)
    s = jnp.where(key <= qry, s, NEG_INF)
    m = jnp.max(s, axis=0, keepdims=True)
    p = jnp.exp(s - m)
    l = jnp.sum(p, axis=0, keepdims=True)
    acc_ref[...] = jnp.dot(vt_ref[0, i], p.astype(BF16), preferred_element_type=F32)

    def past_block(j, m_old, l_old):
        s = scores(j)
        chosen = sel_ref[pl.ds(j, 1), :] > 0.0
        m_new = jnp.where(chosen, jnp.maximum(m_old, jnp.max(s, axis=0, keepdims=True)), m_old)
        alpha = jnp.exp(m_old - m_new)
        p = jnp.exp(s - jnp.where(chosen, m_new, -NEG_INF))
        acc_ref[...] = alpha * acc_ref[...] + jnp.dot(vt_ref[0, j], p.astype(BF16), preferred_element_type=F32)
        return m_new, alpha * l_old + jnp.sum(p, axis=0, keepdims=True)

    def past_pair(t, carry):
        return past_block(2 * t + 1, *past_block(2 * t, *carry))

    m, l = lax.fori_loop(0, lax.shift_right_logical(i + 1, 1), past_pair, (m, l))
    ot = acc_ref[...] / l
    ot = jnp.concatenate([ot[:, h * MOBA_BLOCK:(h + 1) * MOBA_BLOCK] for h in range(GROUP)], axis=0)
    o_ref[...] = ot.T.astype(BF16)


def _moba_prompt(qt, k_rep, vt, kmean_rep, batch, seq):
    nb = seq // MOBA_BLOCK
    cols = GROUP * MOBA_BLOCK
    return pl.pallas_call(
        _moba_prompt_kernel,
        grid=(batch, N_KV_HEADS, nb),
        in_specs=[pl.BlockSpec((GROUP_LANES, MOBA_BLOCK), lambda b, h, i: (h, b * nb + i)),
                  pl.BlockSpec((1, seq, GROUP_LANES), lambda b, h, i: (h, b, 0)),
                  pl.BlockSpec((1, nb, HEAD_DIM, MOBA_BLOCK), lambda b, h, i: (h, b, 0, 0)),
                  pl.BlockSpec((1, 1, nb, GROUP_LANES), lambda b, h, i: (b, h, 0, 0))],
        out_specs=pl.BlockSpec((MOBA_BLOCK, GROUP_LANES), lambda b, h, i: (b * nb + i, h)),
        out_shape=jax.ShapeDtypeStruct((batch * seq, Q_DIM), BF16),
        scratch_shapes=[pltpu.VMEM((nb, cols), F32), pltpu.VMEM((HEAD_DIM, cols), F32)],
        compiler_params=_params("arbitrary", "arbitrary", "arbitrary"), name="moba_prompt",
    )(qt, k_rep, vt, kmean_rep)


def _moba_sample_kernel(nq, n_chunks, n_seq, pt_ref, qexp_ref, kn_ref, vn_ref, expand_ref, pk_hbm, pv_hbm,
                        o_ref, cbuf, s_ref, sem):
    b = pl.program_id(0)
    pages = PAGES_PER_STEP
    keys = pages * PAGE_SIZE
    blocks_per_chunk = keys // MOBA_BLOCK
    n_blocks = n_chunks * blocks_per_chunk
    n_stream = 2 * n_chunks

    def chunk_copies(seq, j, lookup):
        pool = pk_hbm if j < n_chunks else pv_hbm
        first = (j % n_chunks) * pages
        return [pltpu.make_async_copy(pool.at[pt_ref[seq, first + p] if lookup else 0],
                                      cbuf.at[j % 2, :, pl.ds(p * PAGE_SIZE, PAGE_SIZE)], sem.at[j % 2])
                for p in range(pages)]

    def start(seq, j):
        for cp in chunk_copies(seq, j, True):
            cp.start()

    @pl.when(b == 0)
    def _():
        start(b, 0)

    qexp32 = qexp_ref[0]
    qexp = qexp32.astype(BF16)
    blk_lane = lax.broadcasted_iota(jnp.int32, (KV_DIM, LANES), 1)
    ksum = jnp.zeros((KV_DIM, LANES), F32)
    for j in range(n_stream):
        if j + 1 < n_stream:
            start(b, j + 1)
        else:
            @pl.when(b + 1 < n_seq)
            def _():
                start(b + 1, 0)
        for cp in chunk_copies(b, j, False):
            cp.wait()
        cols = slice((j % n_chunks) * keys, (j % n_chunks + 1) * keys)
        if j < n_chunks:
            kt = cbuf[j % 2]
            s_ref[:, cols] = jnp.dot(qexp, kt.astype(BF16), preferred_element_type=F32)
            for n in range(blocks_per_chunk):
                blk_sum = jnp.sum(kt[:, n * MOBA_BLOCK:(n + 1) * MOBA_BLOCK], axis=1, keepdims=True)
                ksum = jnp.where(blk_lane == j * blocks_per_chunk + n, blk_sum, ksum)
        if j == n_chunks - 1:
            gate = jnp.dot(qexp32, ksum * (1.0 / MOBA_BLOCK), preferred_element_type=F32,
                           precision=lax.Precision.HIGHEST)
            sel_bias = jnp.where(_top3_mask(gate, n_blocks), 0.0, NEG_INF).astype(BF16)
            s_own = lax.dot_general(qexp, kn_ref[...].astype(BF16), NT_DIMS, preferred_element_type=F32)
            qi = lax.bitwise_and(lax.broadcasted_iota(jnp.int32, s_own.shape, 0), nq - 1)
            kj = lax.broadcasted_iota(jnp.int32, s_own.shape, 1)
            s_own = jnp.where(kj <= qi, s_own, NEG_INF)
            m_tile = jnp.full((N_HEADS * nq, MOBA_BLOCK), NEG_INF, F32)
            for c in range(n_chunks):
                cc = slice(c * keys, (c + 1) * keys)
                sm = s_ref[:, cc] + jnp.dot(sel_bias, expand_ref[:, cc], preferred_element_type=F32)
                s_ref[:, cc] = sm
                for n in range(blocks_per_chunk):
                    m_tile = jnp.maximum(m_tile, sm[:, n * MOBA_BLOCK:(n + 1) * MOBA_BLOCK])
            m = jnp.maximum(jnp.max(m_tile, axis=1, keepdims=True), jnp.max(s_own, axis=1, keepdims=True))
            p_own = jnp.exp(s_own - m)
            l_own = jnp.sum(p_own, axis=1, keepdims=True)
            acc = jnp.dot(p_own.astype(BF16), vn_ref[...].astype(BF16), preferred_element_type=F32)
            l_tile = jnp.zeros((N_HEADS * nq, MOBA_BLOCK), F32)
        if j >= n_chunks:
            p = jnp.exp(s_ref[:, cols] - m)
            for n in range(blocks_per_chunk):
                l_tile = l_tile + p[:, n * MOBA_BLOCK:(n + 1) * MOBA_BLOCK]
            acc = acc + lax.dot_general(p.astype(BF16), cbuf[j % 2].astype(BF16), NT_DIMS,
                                        preferred_element_type=F32)
    o_ref[0] = acc / (l_own + jnp.sum(l_tile, axis=1, keepdims=True))


def _moba_sample(qexp, k, v, pool_k, pool_v, page_table, row0, nq):
    db, n_pages = page_table.shape
    n_pool = pool_k.shape[0]
    pages = PAGES_PER_STEP
    n_chunks = n_pages // pages
    rows = N_HEADS * nq
    pk = pool_k.transpose(0, 2, 3, 1).reshape(n_pool, KV_DIM, PAGE_SIZE)
    pv = pool_v.transpose(0, 2, 3, 1).reshape(n_pool, KV_DIM, PAGE_SIZE)
    n_keys = n_pages * PAGE_SIZE
    expand = (jnp.arange(LANES)[:, None] == (jnp.arange(n_keys) // MOBA_BLOCK)[None, :]).astype(BF16)
    new = lambda b, pt: (row0 // nq + b, 0)
    grid_spec = pltpu.PrefetchScalarGridSpec(
        num_scalar_prefetch=1,
        grid=(db,),
        in_specs=[pl.BlockSpec((1, rows, KV_DIM), lambda b, pt: (b, 0, 0)),
                  pl.BlockSpec((nq, KV_DIM), new), pl.BlockSpec((nq, KV_DIM), new),
                  pl.BlockSpec((LANES, n_keys), lambda b, pt: (0, 0)),
                  pl.BlockSpec(memory_space=pl.ANY), pl.BlockSpec(memory_space=pl.ANY)],
        out_specs=pl.BlockSpec((1, rows, KV_DIM), lambda b, pt: (b, 0, 0)),
        scratch_shapes=[pltpu.VMEM((2, KV_DIM, pages * PAGE_SIZE), F32),
                        pltpu.VMEM((rows, n_keys), F32),
                        pltpu.SemaphoreType.DMA((2,))])
    return pl.pallas_call(
        functools.partial(_moba_sample_kernel, nq, n_chunks, db),
        grid_spec=grid_spec,
        out_shape=jax.ShapeDtypeStruct((db, rows, KV_DIM), F32),
        compiler_params=_params("arbitrary"), name="moba_sample",
    )(page_table, qexp, k, v, expand, pk, pv)


def _proj_route_kernel(o_ref, x_ref, wo_ref, bo_ref, g_ref, wr_ref, br_ref, x1_ref, h_ref, route_ref):
    x1 = x_ref[...] + jnp.dot(o_ref[...], wo_ref[...], preferred_element_type=F32) + bo_ref[...]
    x1_ref[...] = x1
    h = _rms(x1, g_ref[...])
    h_ref[...] = h
    lg = jnp.dot(h, wr_ref[...], preferred_element_type=F32, precision=lax.Precision.HIGHEST) + br_ref[...]
    lane = lax.broadcasted_iota(jnp.int32, lg.shape, 1)
    lane_f = lane.astype(F32)
    is_group = lane < N_GROUPS
    gl = jnp.where(is_group, lg, NEG_INF)
    gmax = jnp.max(gl, axis=1, keepdims=True)
    grp = jnp.min(jnp.where(gl == gmax, lane_f, 1e9), axis=1, keepdims=True)
    g_w = 1.0 / jnp.sum(jnp.where(is_group, jnp.exp(gl - gmax), 0.0), axis=1, keepdims=True)
    lane_grp = lax.shift_right_logical(lane - N_GROUPS, 3).astype(F32)
    in_grp = (lane >= N_GROUPS) & (lane < N_GROUPS + N_EXPERTS) & (lane_grp == grp)
    el = jnp.where(in_grp, lg, NEG_INF)
    v1 = jnp.max(el, axis=1, keepdims=True)
    i1 = jnp.min(jnp.where(el == v1, lane_f, 1e9), axis=1, keepdims=True)
    el2 = jnp.where(lane_f == i1, BELOW_NEG_INF, el)
    v2 = jnp.max(el2, axis=1, keepdims=True)
    i2 = jnp.min(jnp.where(el2 == v2, lane_f, 1e9), axis=1, keepdims=True)
    e2w = jnp.exp(v2 - v1)
    w1 = g_w / (1.0 + e2w)
    w2 = g_w * e2w / (1.0 + e2w)
    route = jnp.where(lane == 0, i1 - N_GROUPS, 0.0)
    route = jnp.where(lane == 1, i2 - N_GROUPS, route)
    route = jnp.where(lane == 2, w1, route)
    route = jnp.where(lane == 3, w2, route)
    route_ref[...] = route


def _proj_route(o, x, wo, bo, g, w_gr, b_gr, w_er, b_er):
    t = x.shape[0]
    row = lambda i: (i, 0)
    fixed = lambda i: (0, 0)
    pad = LANES - N_GROUPS - N_EXPERTS
    wr = jnp.concatenate([w_gr, w_er, jnp.zeros((D_MODEL, pad), F32)], axis=1)
    br = jnp.concatenate([b_gr, b_er, jnp.zeros((pad,), F32)]).reshape(1, LANES)
    return pl.pallas_call(
        _proj_route_kernel,
        grid=(t // ROW_TILE,),
        in_specs=[pl.BlockSpec((ROW_TILE, Q_DIM), row), pl.BlockSpec((ROW_TILE, D_MODEL), row),
                  pl.BlockSpec((Q_DIM, D_MODEL), fixed), pl.BlockSpec((1, D_MODEL), fixed),
                  pl.BlockSpec((1, D_MODEL), fixed), pl.BlockSpec((D_MODEL, LANES), fixed),
                  pl.BlockSpec((1, LANES), fixed)],
        out_specs=[pl.BlockSpec((ROW_TILE, D_MODEL), row), pl.BlockSpec((ROW_TILE, D_MODEL), row),
                   pl.BlockSpec((ROW_TILE, LANES), row)],
        out_shape=[jax.ShapeDtypeStruct((t, D_MODEL), F32), jax.ShapeDtypeStruct((t, D_MODEL), F32),
                   jax.ShapeDtypeStruct((t, LANES), F32)],
        compiler_params=_params("arbitrary"), name="proj_route",
    )(o, x, wo.astype(BF16), bo.reshape(1, D_MODEL), g.reshape(1, D_MODEL), wr, br)


def _expert_kernel(n_steps, blk_e_ref, nvalid_ref, src_ref, src_next_ref, dst_ref, wrow_ref, h_hbm,
                   wg_ref, wu_ref, wd_ref, y_hbm, xbuf, ybuf, wg_bf, wu_bf, wd_bf, gsem, ssem):
    i = pl.program_id(0)
    slot = lax.rem(i, 2)
    rows = EXPERT_ROWS
    nvalid = nvalid_ref[i]
    nvalid_next = jnp.where(i + 1 < n_steps, nvalid_ref[jnp.minimum(i + 1, n_steps - 1)], 0)
    nvalid_prev2 = jnp.where(i >= 2, nvalid_ref[jnp.maximum(i - 2, 0)], 0)

    def row_copy_in(idx_ref, r, s):
        return pltpu.make_async_copy(h_hbm.at[pl.ds(idx_ref[0, r], 1)], xbuf.at[s, pl.ds(r, 1)], gsem.at[s])

    def row_copy_out(r, s):
        return pltpu.make_async_copy(ybuf.at[s, pl.ds(r, 1)], y_hbm.at[pl.ds(dst_ref[0, r], 1)], ssem.at[s])

    def for_rows(fn, n=None):
        def body(r, carry):
            if n is None:
                fn(r)
            else:
                pl.when(r < n)(lambda: fn(r))
            return carry
        lax.fori_loop(0, rows, body, 0, unroll=8)

    @pl.when((i == 0) & (nvalid > 0))
    def _():
        for_rows(lambda r: row_copy_in(src_ref, r, 0).start())

    @pl.when(nvalid_next > 0)
    def _():
        for_rows(lambda r: row_copy_in(src_next_ref, r, 1 - slot).start())

    @pl.when(nvalid_prev2 > 0)
    def _():
        for_rows(lambda r: row_copy_out(r, slot).wait(), nvalid_prev2)

    @pl.when(nvalid > 0)
    def _():
        e_changed = jnp.logical_or(i == 0, blk_e_ref[i] != blk_e_ref[jnp.maximum(i - 1, 0)])

        @pl.when(e_changed)
        def _():
            wg_bf[...] = wg_ref[0, 0].astype(BF16)
            wu_bf[...] = wu_ref[0, 0].astype(BF16)
            wd_bf[...] = wd_ref[0, 0].astype(BF16)

        for_rows(lambda r: row_copy_in(src_ref, r, slot).wait())
        x = xbuf[slot].astype(BF16)
        gate = jnp.dot(x, wg_bf[...], preferred_element_type=F32)
        up = jnp.dot(x, wu_bf[...], preferred_element_type=F32)
        hid = (gate * jax.nn.sigmoid(gate) * up).astype(BF16)
        y = jnp.dot(hid, wd_bf[...], preferred_element_type=F32)
        ybuf[slot] = y * wrow_ref[...]
        for_rows(lambda r: row_copy_out(r, slot).start(), nvalid)


def _expert_ffn(h, e_idx, e_w, w_gate, w_up, w_down, layer):
    t = h.shape[0]
    a = t * EXPERT_TOPK
    rows = EXPERT_ROWS
    n_blk = -(-a // rows) + N_EXPERTS
    n_steps = n_blk + 2
    n_rows = n_steps * rows
    flat_e = e_idx.reshape(a)
    order = jnp.argsort(flat_e).astype(jnp.int32)
    counts = jnp.sum(flat_e[:, None] == jnp.arange(N_EXPERTS, dtype=jnp.int32)[None, :], axis=0, dtype=jnp.int32)
    padded = (counts + rows - 1) // rows * rows
    pad_end = jnp.cumsum(padded)
    pad_start = pad_end - padded
    start = jnp.cumsum(counts) - counts
    blk_first = jnp.arange(n_steps, dtype=jnp.int32) * rows
    blk_e = jnp.minimum(jnp.sum(blk_first[:, None] >= pad_end[None, :], axis=1), N_EXPERTS - 1).astype(jnp.int32)
    rank0 = blk_first - pad_start[blk_e]
    nvalid = jnp.clip(counts[blk_e] - rank0, 0, rows).astype(jnp.int32)
    within = jnp.arange(rows, dtype=jnp.int32)[None, :]
    valid = within < nvalid[:, None]
    assign = order[jnp.clip((start[blk_e] + rank0)[:, None] + within, 0, a - 1)]
    tok = assign // EXPERT_TOPK
    src = jnp.where(valid, tok, 0)
    dst = jnp.where(valid, (assign % EXPERT_TOPK) * t + tok, 0)
    wrow = jnp.where(valid, e_w.reshape(a)[assign], 0.0)
    idx_blk = lambda i, be, na: (i, 0, 0)
    idx_next = lambda i, be, na: (jnp.minimum(i + 1, n_steps - 1), 0, 0)
    w_in = lambda i, be, na: (layer, be[i], 0, 0)
    grid_spec = pltpu.PrefetchScalarGridSpec(
        num_scalar_prefetch=2,
        grid=(n_steps,),
        in_specs=[pl.BlockSpec((1, 1, rows), idx_blk, memory_space=pltpu.SMEM),
                  pl.BlockSpec((1, 1, rows), idx_next, memory_space=pltpu.SMEM),
                  pl.BlockSpec((1, 1, rows), idx_blk, memory_space=pltpu.SMEM),
                  pl.BlockSpec((rows, 1), lambda i, be, na: (i, 0)),
                  pl.BlockSpec(memory_space=pl.ANY),
                  pl.BlockSpec((1, 1, D_MODEL, EXPERT_FF), w_in), pl.BlockSpec((1, 1, D_MODEL, EXPERT_FF), w_in),
                  pl.BlockSpec((1, 1, EXPERT_FF, D_MODEL), w_in)],
        out_specs=pl.BlockSpec(memory_space=pl.ANY),
        scratch_shapes=[pltpu.VMEM((2, rows, D_MODEL), F32), pltpu.VMEM((2, rows, D_MODEL), F32),
                        pltpu.VMEM((D_MODEL, EXPERT_FF), BF16), pltpu.VMEM((D_MODEL, EXPERT_FF), BF16),
                        pltpu.VMEM((EXPERT_FF, D_MODEL), BF16),
                        pltpu.SemaphoreType.DMA((2,)), pltpu.SemaphoreType.DMA((2,))])

    def body(blk_e_ref, nvalid_ref, src_ref, src_next_ref, dst_ref, *rest):
        _expert_kernel(n_steps, blk_e_ref, nvalid_ref, src_ref.at[0], src_next_ref.at[0], dst_ref.at[0], *rest)

    return pl.pallas_call(
        body, grid_spec=grid_spec,
        out_shape=jax.ShapeDtypeStruct((EXPERT_TOPK * t, D_MODEL), F32),
        compiler_params=_params("arbitrary"), name="expert_ffn",
    )(blk_e, nvalid, src.reshape(n_steps, 1, rows), src.reshape(n_steps, 1, rows),
      dst.reshape(n_steps, 1, rows), wrow.reshape(n_rows, 1), h, w_gate, w_up, w_down)


def _final_kernel(x_ref, y0_ref, y1_ref, g_ref, o_ref):
    o_ref[...] = _rms(x_ref[...] + y0_ref[...] + y1_ref[...], g_ref[...])


def _final_norm(x, y2, g, row0, n_rows):
    t = x.shape[0]
    off = row0 // ROW_TILE
    return pl.pallas_call(
        _final_kernel,
        grid=(n_rows // ROW_TILE,),
        in_specs=[pl.BlockSpec((ROW_TILE, D_MODEL), lambda i: (i + off, 0)),
                  pl.BlockSpec((ROW_TILE, D_MODEL), lambda i: (i + off, 0)),
                  pl.BlockSpec((ROW_TILE, D_MODEL), lambda i: (i + off + t // ROW_TILE, 0)),
                  pl.BlockSpec((1, D_MODEL), lambda i: (0, 0))],
        out_specs=pl.BlockSpec((ROW_TILE, D_MODEL), lambda i: (i, 0)),
        out_shape=jax.ShapeDtypeStruct((n_rows, D_MODEL), F32),
        compiler_params=_params("arbitrary"), name="final_norm",
    )(x, y2, y2, g.reshape(1, D_MODEL))


def _rope_tables(pos):
    half = HEAD_DIM // 2
    inv = ROPE_THETA ** (-jnp.arange(half, dtype=F32) / half)
    ang = pos.astype(F32)[:, None] * inv[None, :]
    cos = jnp.cos(ang)
    sin = jnp.sin(ang)
    reps = LANES // HEAD_DIM
    return jnp.tile(jnp.concatenate([cos, cos], axis=1), (1, reps)), jnp.tile(jnp.concatenate([-sin, sin], axis=1), (1, reps))


def _moe(o, x, wo, bo, g, w_gr, b_gr, w_er, b_er, w_gate, w_up, w_down, layer):
    x1, h, route = _proj_route(o, x, wo, bo, g, w_gr, b_gr, w_er, b_er)
    e_idx = jnp.clip(route[:, :EXPERT_TOPK].astype(jnp.int32), 0, N_EXPERTS - 1)
    e_w = route[:, EXPERT_TOPK:2 * EXPERT_TOPK]
    return x1, _expert_ffn(h, e_idx, e_w, w_gate, w_up, w_down, layer)


def kernel(x_prompt, x_sample, cache_swa_k, cache_swa_v, cache_moba_k, cache_moba_v, page_table, norm_mix, norm_ffn, norm_final, swa_w_qkv, swa_b_qkv, swa_sinks, swa_w_o, swa_b_o, moba_w_qkv, moba_w_o, w_group_router, b_group_router, w_expert_router, b_expert_router, w_gate, w_up, w_down):
    batch, seq, _ = x_prompt.shape
    db, nq, _ = x_sample.shape
    past_len = page_table.shape[1] * PAGE_SIZE
    tp = batch * seq
    ts = db * nq
    t = tp + ts
    x = jnp.concatenate([x_prompt.reshape(tp, D_MODEL), x_sample.reshape(ts, D_MODEL)], axis=0)
    pos = jnp.concatenate([jnp.tile(jnp.arange(seq), batch), jnp.tile(past_len + jnp.arange(nq), db)])
    cos_t, sin_t = _rope_tables(pos)

    q, k0, v0, k_rep, v_rep = _qkv_rope(x, [], norm_mix[0], swa_w_qkv[0], swa_b_qkv[0], cos_t, sin_t, False)
    o_p = _swa_prompt(q, k_rep, v_rep, swa_sinks[0], batch, seq)
    o_s, swa_ks, swa_vs = _swa_sample(q, k0, v0, cache_swa_k[0], cache_swa_v[0], swa_sinks[0], tp, nq)
    o = jnp.concatenate([o_p, o_s], axis=0)
    x1, y2 = _moe(o, x, swa_w_o[0], swa_b_o[0], norm_ffn[0], w_group_router[0], b_group_router[0],
                  w_expert_router[0], b_expert_router[0], w_gate, w_up, w_down, 0)

    x, q, k1, v1, k_rep, kmean, qt, vt = _qkv_rope(
        x1, [(y2, 0), (y2, t)], norm_mix[1], moba_w_qkv[0], jnp.zeros((QKV_DIM,), F32), cos_t, sin_t, True)
    nb = seq // MOBA_BLOCK
    kmean_rep = jnp.tile(kmean[:batch * nb, 0].reshape(batch, nb, N_KV_HEADS, 1, HEAD_DIM), (1, 1, 1, GROUP, 1))
    kmean_rep = kmean_rep.reshape(batch, nb, N_KV_HEADS, GROUP_LANES).transpose(0, 2, 1, 3)
    o_p = _moba_prompt(qt, k_rep, vt, kmean_rep, batch, seq)
    qs = q[tp:].reshape(db, nq, N_HEADS, 1, HEAD_DIM)
    slot = (jnp.arange(N_HEADS) // GROUP)[:, None] == jnp.arange(N_KV_HEADS)[None, :]
    qexp = jnp.where(slot[None, None, :, :, None], qs, 0.0).transpose(0, 2, 1, 3, 4).reshape(db, N_HEADS * nq, KV_DIM)
    o_s = _moba_sample(qexp, k1, v1, cache_moba_k[0], cache_moba_v[0], page_table, tp, nq)
    o_s = o_s.reshape(db, N_HEADS, nq, N_KV_HEADS, HEAD_DIM)
    o_s = jnp.sum(jnp.where(slot[None, :, None, :, None], o_s, 0.0), axis=3).transpose(0, 2, 1, 3)
    o = jnp.concatenate([o_p, o_s.reshape(ts, Q_DIM).astype(BF16)], axis=0)
    x1, y2 = _moe(o, x, moba_w_o[0], jnp.zeros((D_MODEL,), F32), norm_ffn[1], w_group_router[1],
                  b_group_router[1], w_expert_router[1], b_expert_router[1], w_gate, w_up, w_down, 1)

    y_prompt = _final_norm(x1, y2, norm_final, 0, tp).reshape(batch, seq, D_MODEL)
    y_sample = _final_norm(x1, y2, norm_final, tp, ts).reshape(db, nq, D_MODEL)
    win = min(WINDOW, seq)
    kv5 = lambda a, b, s: a.reshape(1, b, s, N_KV_HEADS, HEAD_DIM)
    k0p = k0[:tp].reshape(batch, seq, KV_DIM)[:, seq - win:]
    v0p = v0[:tp].reshape(batch, seq, KV_DIM)[:, seq - win:]
    return (y_prompt, y_sample, kv5(k0p, batch, win), kv5(v0p, batch, win),
            kv5(swa_ks, db, swa_ks.shape[1]), kv5(swa_vs, db, swa_vs.shape[1]),
            kv5(k1[:tp], batch, seq), kv5(v1[:tp], batch, seq), kv5(k1[tp:], db, nq), kv5(v1[tp:], db, nq))
```

```python
import functools

import jax
import jax.numpy as jnp
from jax import lax
from jax.experimental import pallas as pl
from jax.experimental.pallas import tpu as pltpu

D_MODEL = 1024
N_HEADS = 16
N_KV_HEADS = 4
HEAD_DIM = 64
GROUP = N_HEADS // N_KV_HEADS
Q_DIM = N_HEADS * HEAD_DIM
KV_DIM = N_KV_HEADS * HEAD_DIM
QKV_DIM = Q_DIM + 2 * KV_DIM
ROPE_THETA = 10000.0
WINDOW = 128
MOBA_BLOCK = 256
MOBA_TOPK = 3
PAGE_SIZE = 128
N_GROUPS = 8
EXPERTS_PER_GROUP = 8
N_EXPERTS = N_GROUPS * EXPERTS_PER_GROUP
EXPERT_TOPK = 2
EXPERT_FF = D_MODEL // 4
RMS_EPS = 1e-6
NEG_INF = -1e30
BELOW_NEG_INF = -3e38
LOG2_E = 1.4426950408889634
VT_ROWS = HEAD_DIM + 16

LANES = 128
GROUP_LANES = GROUP * HEAD_DIM
ROW_TILE = 256
EXPERT_ROWS = 256
PAGES_PER_STEP = 16
VMEM_LIMIT = 56 * 1024 * 1024

F32 = jnp.float32
BF16 = jnp.bfloat16
NT_DIMS = (((1,), (1,)), ((), ()))


def _params(*sem):
    return pltpu.CompilerParams(dimension_semantics=sem, vmem_limit_bytes=VMEM_LIMIT)


def _rms(x, g):
    return x * lax.rsqrt(jnp.mean(x * x, axis=-1, keepdims=True) + RMS_EPS) * g


def _replicate_head(x128, odd):
    swapped = pltpu.roll(x128, HEAD_DIM, axis=1)
    low = lax.broadcasted_iota(jnp.int32, x128.shape, 1) < HEAD_DIM
    return jnp.where(low != odd, x128, swapped)


def _replicate_kv(x256, kvh):
    col = x256[:, (kvh // 2) * LANES:(kvh // 2 + 1) * LANES]
    rep = _replicate_head(col, kvh % 2 == 1)
    return jnp.concatenate([rep, rep], axis=1)


def _expand_heads(q):
    head = lax.shift_right_logical(lax.broadcasted_iota(jnp.int32, q.shape, 1), 6)
    return jnp.concatenate([jnp.where(head == h, q, 0.0) for h in range(GROUP)], axis=0)


def _collapse_heads(o_rep, rows):
    head = lax.shift_right_logical(lax.broadcasted_iota(jnp.int32, (rows, GROUP_LANES), 1), 6)
    out = jnp.zeros((rows, GROUP_LANES), F32)
    for h in range(GROUP):
        out = jnp.where(head == h, o_rep[h * rows:(h + 1) * rows], out)
    return out


def _top3_mask(gate, n_valid):
    lane = lax.broadcasted_iota(jnp.int32, gate.shape, 1)
    lane_f = lane.astype(F32)
    valid = lane < n_valid
    g = jnp.where(valid, gate, NEG_INF)
    sel = jnp.zeros(gate.shape, jnp.bool_)
    for _ in range(MOBA_TOPK):
        mx = jnp.max(g, axis=1, keepdims=True)
        first = jnp.min(jnp.where(g == mx, lane_f, 1e9), axis=1, keepdims=True)
        hit = lane_f == first
        sel = jnp.logical_or(sel, jnp.logical_and(hit, valid))
        g = jnp.where(hit, BELOW_NEG_INF, g)
    return sel


def _qkv_kernel(n_add, moba, *refs):
    refs = list(refs)
    x_ref = refs.pop(0)
    add_refs = [refs.pop(0) for _ in range(n_add)]
    g_ref, w_ref, b_ref, cos_ref, sin_ref = refs[:5]
    outs = refs[5:]
    x = x_ref[...]
    for r in add_refs:
        x = x + r[...]
    if n_add:
        outs.pop(0)[...] = x
    q_ref, k_ref, v_ref, krep_ref = outs[:4]
    h = _rms(x, g_ref[...]).astype(BF16)
    y = jnp.dot(h, w_ref[...], preferred_element_type=F32) + b_ref[...]
    cos = cos_ref[...]
    sin = sin_ref[...]
    low = lax.bitwise_and(lax.broadcasted_iota(jnp.int32, cos.shape, 1), HEAD_DIM - 1) < HEAD_DIM // 2

    def rope(slab):
        partner = jnp.where(low, pltpu.roll(slab, LANES - HEAD_DIM // 2, axis=1),
                            pltpu.roll(slab, HEAD_DIM // 2, axis=1))
        return slab * cos + partner * sin

    scale = HEAD_DIM ** -0.5 * (LOG2_E if moba else 1.0)
    for c in range(Q_DIM // LANES):
        q_ref[:, c * LANES:(c + 1) * LANES] = rope(y[:, c * LANES:(c + 1) * LANES]) * scale
    k = jnp.concatenate([rope(y[:, Q_DIM + c * LANES:Q_DIM + (c + 1) * LANES])
                         for c in range(KV_DIM // LANES)], axis=1)
    v = y[:, Q_DIM + KV_DIM:]
    k_ref[...] = k
    v_ref[...] = v
    for kvh in range(N_KV_HEADS):
        krep_ref[kvh] = _replicate_kv(k, kvh).astype(BF16)
    if not moba:
        for kvh in range(N_KV_HEADS):
            outs[4][kvh] = _replicate_kv(v, kvh).astype(BF16)
    else:
        kmean_ref, qt_ref, vt_ref = outs[4:]
        kmean_ref[0] = jnp.broadcast_to(jnp.mean(k, axis=0, keepdims=True), (8, KV_DIM))
        qt_ref[...] = q_ref[...].T
        vt = v.T.reshape(N_KV_HEADS, HEAD_DIM, ROW_TILE).astype(BF16)
        vt_ref[:, 0] = jnp.concatenate([vt, jnp.ones((N_KV_HEADS, VT_ROWS - HEAD_DIM, ROW_TILE), BF16)], axis=1)


def _qkv_rope(x, adds, g, w, b, cos_t, sin_t, moba):
    t = x.shape[0]
    nt = t // ROW_TILE
    row = lambda i: (i, 0)
    fixed = lambda i: (0, 0)
    in_specs = [pl.BlockSpec((ROW_TILE, D_MODEL), row)]
    args = [x]
    for arr, off in adds:
        in_specs.append(pl.BlockSpec((ROW_TILE, D_MODEL), functools.partial(lambda o, i: (i + o, 0), off // ROW_TILE)))
        args.append(arr)
    in_specs += [pl.BlockSpec((1, D_MODEL), fixed), pl.BlockSpec((D_MODEL, QKV_DIM), fixed),
                 pl.BlockSpec((1, QKV_DIM), fixed), pl.BlockSpec((ROW_TILE, LANES), row),
                 pl.BlockSpec((ROW_TILE, LANES), row)]
    args += [g.reshape(1, D_MODEL), w.astype(BF16), b.reshape(1, QKV_DIM), cos_t, sin_t]
    out_shape, out_specs = [], []
    if adds:
        out_shape.append(jax.ShapeDtypeStruct((t, D_MODEL), F32))
        out_specs.append(pl.BlockSpec((ROW_TILE, D_MODEL), row))
    rep_shape = jax.ShapeDtypeStruct((N_KV_HEADS, t, GROUP_LANES), BF16)
    rep_spec = pl.BlockSpec((N_KV_HEADS, ROW_TILE, GROUP_LANES), lambda i: (0, i, 0))
    out_shape += [jax.ShapeDtypeStruct((t, Q_DIM), F32), jax.ShapeDtypeStruct((t, KV_DIM), F32),
                  jax.ShapeDtypeStruct((t, KV_DIM), F32), rep_shape]
    out_specs += [pl.BlockSpec((ROW_TILE, Q_DIM), row), pl.BlockSpec((ROW_TILE, KV_DIM), row),
                  pl.BlockSpec((ROW_TILE, KV_DIM), row), rep_spec]
    if not moba:
        out_shape.append(rep_shape)
        out_specs.append(rep_spec)
    else:
        out_shape += [jax.ShapeDtypeStruct((nt, 8, KV_DIM), F32), jax.ShapeDtypeStruct((Q_DIM, t), F32),
                      jax.ShapeDtypeStruct((N_KV_HEADS, nt, VT_ROWS, ROW_TILE), BF16)]
        out_specs += [pl.BlockSpec((1, 8, KV_DIM), lambda i: (i, 0, 0)),
                      pl.BlockSpec((Q_DIM, ROW_TILE), lambda i: (0, i)),
                      pl.BlockSpec((N_KV_HEADS, 1, VT_ROWS, ROW_TILE), lambda i: (0, i, 0, 0))]
    return pl.pallas_call(
        functools.partial(_qkv_kernel, len(adds), moba),
        grid=(nt,), in_specs=in_specs, out_specs=out_specs, out_shape=out_shape,
        compiler_params=_params("arbitrary"), name="qkv_rope")(*args)


def _sink_softmax_pv(s, ok, sink_col, v):
    s = jnp.where(ok, s, NEG_INF)
    m = jnp.maximum(jnp.max(s, axis=1, keepdims=True), sink_col)
    p = jnp.exp(s - m)
    denom = jnp.sum(p, axis=1, keepdims=True) + jnp.exp(sink_col - m)
    return jnp.dot(p.astype(BF16), v, preferred_element_type=F32) / denom


def _sink_column(sinks_ref, kvh, rows):
    blk = lax.shift_right_logical(lax.broadcasted_iota(jnp.int32, (GROUP * rows, 1), 0), rows.bit_length() - 1)
    col = jnp.zeros((GROUP * rows, 1), F32)
    for h in range(GROUP):
        col = jnp.where(blk == h, sinks_ref[kvh * GROUP + h], col)
    return col


def _swa_prompt_kernel(sinks_ref, q_ref, kprev_ref, kcur_ref, vprev_ref, vcur_ref, o_ref):
    kvh = pl.program_id(1)
    blk = pl.program_id(2)
    rows = WINDOW
    qexp = _expand_heads(q_ref[...]).astype(BF16)
    k = jnp.concatenate([kprev_ref[0], kcur_ref[0]], axis=0)
    v = jnp.concatenate([vprev_ref[0], vcur_ref[0]], axis=0)
    s = lax.dot_general(qexp, k, NT_DIMS, preferred_element_type=F32)
    qi = lax.bitwise_and(lax.broadcasted_iota(jnp.int32, s.shape, 0), rows - 1)
    kj = lax.broadcasted_iota(jnp.int32, s.shape, 1)
    first_key = jnp.where(blk > 0, 0, WINDOW)
    ok = (kj > qi) & (kj <= qi + WINDOW) & (kj >= first_key)
    o_rep = _sink_softmax_pv(s, ok, _sink_column(sinks_ref, kvh, rows), v)
    o_ref[...] = _collapse_heads(o_rep, rows).astype(BF16)


def _swa_prompt(q, k_rep, v_rep, sinks, batch, seq):
    t = batch * seq
    nb = seq // WINDOW
    cur = lambda b, h, i: (h, b * nb + i, 0)
    prev = lambda b, h, i: (h, b * nb + jnp.maximum(i - 1, 0), 0)
    qo = lambda b, h, i: (b * nb + i, h)
    return pl.pallas_call(
        _swa_prompt_kernel,
        grid=(batch, N_KV_HEADS, nb),
        in_specs=[pl.BlockSpec(memory_space=pltpu.SMEM),
                  pl.BlockSpec((WINDOW, GROUP_LANES), qo),
                  pl.BlockSpec((1, WINDOW, GROUP_LANES), prev), pl.BlockSpec((1, WINDOW, GROUP_LANES), cur),
                  pl.BlockSpec((1, WINDOW, GROUP_LANES), prev), pl.BlockSpec((1, WINDOW, GROUP_LANES), cur)],
        out_specs=pl.BlockSpec((WINDOW, GROUP_LANES), qo),
        out_shape=jax.ShapeDtypeStruct((t, Q_DIM), BF16),
        compiler_params=_params("arbitrary", "arbitrary", "arbitrary"), name="swa_prompt",
    )(sinks, q, k_rep, k_rep, v_rep, v_rep)


def _swa_sample_kernel(seqs, nq, sinks_ref, q_ref, kn_ref, vn_ref, ck_ref, cv_ref, o_ref, ko_ref, vo_ref):
    win = ck_ref.shape[1]

    def one_seq(s, carry):
        r0 = pl.multiple_of(s * nq, nq)
        q = q_ref[pl.ds(r0, nq), :]
        kn = kn_ref[pl.ds(r0, nq), :]
        vn = vn_ref[pl.ds(r0, nq), :]
        ck = ck_ref[s]
        cv = cv_ref[s]
        ko_ref[s] = jnp.concatenate([ck[nq:], kn], axis=0)
        vo_ref[s] = jnp.concatenate([cv[nq:], vn], axis=0)
        kk = jnp.concatenate([ck, kn], axis=0)
        vv = jnp.concatenate([cv, vn], axis=0)
        outs = []
        for kvh in range(N_KV_HEADS):
            qexp = _expand_heads(q[:, kvh * GROUP_LANES:(kvh + 1) * GROUP_LANES]).astype(BF16)
            k = _replicate_kv(kk, kvh).astype(BF16)
            v = _replicate_kv(vv, kvh).astype(BF16)
            sc = lax.dot_general(qexp, k, NT_DIMS, preferred_element_type=F32)
            qi = lax.bitwise_and(lax.broadcasted_iota(jnp.int32, sc.shape, 0), nq - 1)
            kj = lax.broadcasted_iota(jnp.int32, sc.shape, 1)
            ok = ((kj < win) & (kj > qi + win - WINDOW)) | ((kj >= win) & (kj - win <= qi))
            o_rep = _sink_softmax_pv(sc, ok, _sink_column(sinks_ref, kvh, nq), v)
            outs.append(_collapse_heads(o_rep, nq))
        o_ref[pl.ds(r0, nq), :] = jnp.concatenate(outs, axis=1).astype(BF16)
        return carry

    lax.fori_loop(0, seqs, one_seq, 0)


def _swa_sample(q, k, v, cache_k, cache_v, sinks, row0, nq):
    db, win = cache_k.shape[0], cache_k.shape[1]
    seqs = 8
    rows = seqs * nq
    tok = lambda g: (row0 // rows + g, 0)
    cache = lambda g: (g, 0, 0)
    return pl.pallas_call(
        functools.partial(_swa_sample_kernel, seqs, nq),
        grid=(db // seqs,),
        in_specs=[pl.BlockSpec(memory_space=pltpu.SMEM),
                  pl.BlockSpec((rows, Q_DIM), tok), pl.BlockSpec((rows, KV_DIM), tok),
                  pl.BlockSpec((rows, KV_DIM), tok),
                  pl.BlockSpec((seqs, win, KV_DIM), cache), pl.BlockSpec((seqs, win, KV_DIM), cache)],
        out_specs=[pl.BlockSpec((rows, Q_DIM), lambda g: (g, 0)),
                   pl.BlockSpec((seqs, win, KV_DIM), cache), pl.BlockSpec((seqs, win, KV_DIM), cache)],
        out_shape=[jax.ShapeDtypeStruct((db * nq, Q_DIM), BF16),
                   jax.ShapeDtypeStruct((db, win, KV_DIM), F32), jax.ShapeDtypeStruct((db, win, KV_DIM), F32)],
        compiler_params=_params("arbitrary"), name="swa_sample",
    )(sinks, q, k, v, cache_k.reshape(db, win, KV_DIM), cache_v.reshape(db, win, KV_DIM))


def _top3_rows(gate, n_valid):
    row = lax.broadcasted_iota(jnp.int32, gate.shape, 0)
    row_f = row.astype(F32)
    valid = row < n_valid
    g = jnp.where(valid, gate, NEG_INF)
    sel = jnp.zeros(gate.shape, F32)
    for _ in range(MOBA_TOPK):
        mx = jnp.max(g, axis=0, keepdims=True)
        first = jnp.min(jnp.where(g == mx, row_f, 1e9), axis=0, keepdims=True)
        hit = row_f == first
        sel = jnp.where(jnp.logical_and(hit, valid), 1.0, sel)
        g = jnp.where(hit, BELOW_NEG_INF, g)
    return sel


def _moba_prompt_kernel(qt_ref, k_ref, vt_ref, kmean_ref, o_ref, sel_ref, acc_ref):
    i = pl.program_id(2)
    qt = qt_ref[...]
    row_head = lax.shift_right_logical(lax.broadcasted_iota(jnp.int32, qt.shape, 0), 6)
    qexp32 = jnp.concatenate([jnp.where(row_head == h, qt, 0.0) for h in range(GROUP)], axis=1)
    qexp = qexp32.astype(BF16)
    gate = jnp.dot(kmean_ref[0, 0], qexp32, preferred_element_type=F32, precision=lax.Precision.HIGHEST)
    sel_ref[...] = _top3_rows(gate, i)

    def scores(j):
        start = pl.multiple_of(j * MOBA_BLOCK, MOBA_BLOCK)
        return jnp.dot(k_ref[0, pl.ds(start, MOBA_BLOCK), :], qexp, preferred_element_type=F32)

    s = scores(i)
    key = lax.broadcasted_iota(jnp.int32, s.shape, 0)
    qry = lax.bitwise_and(lax.broadcasted_iota(jnp.int32, s.shape, 1), MOBA_BLOCK - 1)
    s = jnp.where(key <= qry, s, NEG_INF)
    m = jnp.max(s, axis=0, keepdims=True)
    acc_ref[...] = jnp.dot(vt_ref[0, i], jnp.exp2(s - m).astype(BF16), preferred_element_type=F32)

    def past_block(j, m_old):
        s = scores(j)
        chosen = sel_ref[pl.ds(j, 1), :] > 0.0
        m_new = jnp.where(chosen, jnp.maximum(m_old, jnp.max(s, axis=0, keepdims=True)), m_old)
        alpha = jnp.exp2(m_old - m_new)
        p = jnp.exp2(s - jnp.where(chosen, m_new, -NEG_INF))
        acc_ref[...] = alpha * acc_ref[...] + jnp.dot(vt_ref[0, j], p.astype(BF16), preferred_element_type=F32)
        return m_new

    def past_pair(t, m_old):
        return past_block(2 * t + 1, past_block(2 * t, m_old))

    lax.fori_loop(0, lax.shift_right_logical(i + 1, 1), past_pair, m)
    ot = acc_ref[:HEAD_DIM] / acc_ref[HEAD_DIM:HEAD_DIM + 1]
    ot = jnp.concatenate([ot[:, h * MOBA_BLOCK:(h + 1) * MOBA_BLOCK] for h in range(GROUP)], axis=0)
    o_ref[...] = ot.T.astype(BF16)


def _moba_prompt(qt, k_rep, vt, kmean_rep, batch, seq):
    nb = seq // MOBA_BLOCK
    cols = GROUP * MOBA_BLOCK
    return pl.pallas_call(
        _moba_prompt_kernel,
        grid=(batch, N_KV_HEADS, nb),
        in_specs=[pl.BlockSpec((GROUP_LANES, MOBA_BLOCK), lambda b, h, i: (h, b * nb + i)),
                  pl.BlockSpec((1, seq, GROUP_LANES), lambda b, h, i: (h, b, 0)),
                  pl.BlockSpec((1, nb, VT_ROWS, MOBA_BLOCK), lambda b, h, i: (h, b, 0, 0)),
                  pl.BlockSpec((1, 1, nb, GROUP_LANES), lambda b, h, i: (b, h, 0, 0))],
        out_specs=pl.BlockSpec((MOBA_BLOCK, GROUP_LANES), lambda b, h, i: (b * nb + i, h)),
        out_shape=jax.ShapeDtypeStruct((batch * seq, Q_DIM), BF16),
        scratch_shapes=[pltpu.VMEM((nb, cols), F32), pltpu.VMEM((VT_ROWS, cols), F32)],
        compiler_params=_params("arbitrary", "arbitrary", "arbitrary"), name="moba_prompt",
    )(qt, k_rep, vt, kmean_rep)


def _moba_sample_kernel(nq, n_chunks, n_seq, pt_ref, qexp_ref, kn_ref, vn_ref, expand_ref, pk_hbm, pv_hbm,
                        o_ref, cbuf, s_ref, sem):
    b = pl.program_id(0)
    pages = PAGES_PER_STEP
    keys = pages * PAGE_SIZE
    blocks_per_chunk = keys // MOBA_BLOCK
    n_blocks = n_chunks * blocks_per_chunk
    n_stream = 2 * n_chunks

    def chunk_copies(seq, j, lookup):
        pool = pk_hbm if j < n_chunks else pv_hbm
        first = (j % n_chunks) * pages
        return [pltpu.make_async_copy(pool.at[pt_ref[seq, first + p] if lookup else 0],
                                      cbuf.at[j % 2, p], sem.at[j % 2])
                for p in range(pages)]

    def chunk(j):
        return jnp.concatenate([cbuf[j % 2, p] for p in range(pages)], axis=1)

    def start(seq, j):
        for cp in chunk_copies(seq, j, True):
            cp.start()

    @pl.when(b == 0)
    def _():
        start(b, 0)

    qexp32 = qexp_ref[0]
    qexp = qexp32.astype(BF16)
    blk_lane = lax.broadcasted_iota(jnp.int32, (KV_DIM, LANES), 1)
    ksum = jnp.zeros((KV_DIM, LANES), F32)
    for j in range(n_stream):
        if j + 1 < n_stream:
            start(b, j + 1)
        else:
            @pl.when(b + 1 < n_seq)
            def _():
                start(b + 1, 0)
        for cp in chunk_copies(b, j, False):
            cp.wait()
        cols = slice((j % n_chunks) * keys, (j % n_chunks + 1) * keys)
        if j < n_chunks:
            kt = chunk(j)
            s_ref[:, cols] = jnp.dot(qexp, kt.astype(BF16), preferred_element_type=F32)
            for n in range(blocks_per_chunk):
                blk_sum = jnp.sum(kt[:, n * MOBA_BLOCK:(n + 1) * MOBA_BLOCK], axis=1, keepdims=True)
                ksum = jnp.where(blk_lane == j * blocks_per_chunk + n, blk_sum, ksum)
        if j == n_chunks - 1:
            gate = jnp.dot(qexp32, ksum * (1.0 / MOBA_BLOCK), preferred_element_type=F32,
                           precision=lax.Precision.HIGHEST)
            sel_bias = jnp.where(_top3_mask(gate, n_blocks), 0.0, NEG_INF).astype(BF16)
            s_own = lax.dot_general(qexp, kn_ref[...].astype(BF16), NT_DIMS, preferred_element_type=F32)
            qi = lax.bitwise_and(lax.broadcasted_iota(jnp.int32, s_own.shape, 0), nq - 1)
            kj = lax.broadcasted_iota(jnp.int32, s_own.shape, 1)
            s_own = jnp.where(kj <= qi, s_own, NEG_INF)
            m_tile = jnp.full((N_HEADS * nq, MOBA_BLOCK), NEG_INF, F32)
            for c in range(n_chunks):
                cc = slice(c * keys, (c + 1) * keys)
                sm = s_ref[:, cc] + jnp.dot(sel_bias, expand_ref[:, cc], preferred_element_type=F32)
                s_ref[:, cc] = sm
                for n in range(blocks_per_chunk):
                    m_tile = jnp.maximum(m_tile, sm[:, n * MOBA_BLOCK:(n + 1) * MOBA_BLOCK])
            m = jnp.maximum(jnp.max(m_tile, axis=1, keepdims=True), jnp.max(s_own, axis=1, keepdims=True))
            p_own = jnp.exp2(s_own - m)
            l_own = jnp.sum(p_own, axis=1, keepdims=True)
            acc = jnp.dot(p_own.astype(BF16), vn_ref[...].astype(BF16), preferred_element_type=F32)
            l_tile = jnp.zeros((N_HEADS * nq, MOBA_BLOCK), F32)
        if j >= n_chunks:
            p = jnp.exp2(s_ref[:, cols] - m)
            for n in range(blocks_per_chunk):
                l_tile = l_tile + p[:, n * MOBA_BLOCK:(n + 1) * MOBA_BLOCK]
            acc = acc + lax.dot_general(p.astype(BF16), chunk(j).astype(BF16), NT_DIMS,
                                        preferred_element_type=F32)
    o_ref[0] = acc / (l_own + jnp.sum(l_tile, axis=1, keepdims=True))


def _moba_sample(qexp, k, v, pool_k, pool_v, page_table, row0, nq):
    db, n_pages = page_table.shape
    n_pool = pool_k.shape[0]
    pages = PAGES_PER_STEP
    n_chunks = n_pages // pages
    rows = N_HEADS * nq
    pk = pool_k.transpose(0, 2, 3, 1).reshape(n_pool, KV_DIM, PAGE_SIZE)
    pv = pool_v.transpose(0, 2, 3, 1).reshape(n_pool, KV_DIM, PAGE_SIZE)
    n_keys = n_pages * PAGE_SIZE
    expand = (jnp.arange(LANES)[:, None] == (jnp.arange(n_keys) // MOBA_BLOCK)[None, :]).astype(BF16)
    new = lambda b, pt: (row0 // nq + b, 0)
    grid_spec = pltpu.PrefetchScalarGridSpec(
        num_scalar_prefetch=1,
        grid=(db,),
        in_specs=[pl.BlockSpec((1, rows, KV_DIM), lambda b, pt: (b, 0, 0)),
                  pl.BlockSpec((nq, KV_DIM), new), pl.BlockSpec((nq, KV_DIM), new),
                  pl.BlockSpec((LANES, n_keys), lambda b, pt: (0, 0)),
                  pl.BlockSpec(memory_space=pl.ANY), pl.BlockSpec(memory_space=pl.ANY)],
        out_specs=pl.BlockSpec((1, rows, KV_DIM), lambda b, pt: (b, 0, 0)),
        scratch_shapes=[pltpu.VMEM((2, pages, KV_DIM, PAGE_SIZE), F32),
                        pltpu.VMEM((rows, n_keys), F32),
                        pltpu.SemaphoreType.DMA((2,))])
    return pl.pallas_call(
        functools.partial(_moba_sample_kernel, nq, n_chunks, db),
        grid_spec=grid_spec,
        out_shape=jax.ShapeDtypeStruct((db, rows, KV_DIM), F32),
        compiler_params=_params("arbitrary"), name="moba_sample",
    )(page_table, qexp, k, v, expand, pk, pv)


def _proj_route_kernel(o_ref, x_ref, wo_ref, bo_ref, g_ref, wr_ref, br_ref, x1_ref, h_ref, route_ref):
    x1 = x_ref[...] + jnp.dot(o_ref[...], wo_ref[...], preferred_element_type=F32) + bo_ref[...]
    x1_ref[...] = x1
    h = _rms(x1, g_ref[...])
    h_ref[...] = h
    lg = jnp.dot(h, wr_ref[...], preferred_element_type=F32, precision=lax.Precision.HIGHEST) + br_ref[...]
    lane = lax.broadcasted_iota(jnp.int32, lg.shape, 1)
    lane_f = lane.astype(F32)
    is_group = lane < N_GROUPS
    gl = jnp.where(is_group, lg, NEG_INF)
    gmax = jnp.max(gl, axis=1, keepdims=True)
    grp = jnp.min(jnp.where(gl == gmax, lane_f, 1e9), axis=1, keepdims=True)
    g_w = 1.0 / jnp.sum(jnp.where(is_group, jnp.exp(gl - gmax), 0.0), axis=1, keepdims=True)
    lane_grp = lax.shift_right_logical(lane - N_GROUPS, 3).astype(F32)
    in_grp = (lane >= N_GROUPS) & (lane < N_GROUPS + N_EXPERTS) & (lane_grp == grp)
    el = jnp.where(in_grp, lg, NEG_INF)
    v1 = jnp.max(el, axis=1, keepdims=True)
    i1 = jnp.min(jnp.where(el == v1, lane_f, 1e9), axis=1, keepdims=True)
    el2 = jnp.where(lane_f == i1, BELOW_NEG_INF, el)
    v2 = jnp.max(el2, axis=1, keepdims=True)
    i2 = jnp.min(jnp.where(el2 == v2, lane_f, 1e9), axis=1, keepdims=True)
    e2w = jnp.exp(v2 - v1)
    w1 = g_w / (1.0 + e2w)
    w2 = g_w * e2w / (1.0 + e2w)
    route = jnp.where(lane == 0, i1 - N_GROUPS, 0.0)
    route = jnp.where(lane == 1, i2 - N_GROUPS, route)
    route = jnp.where(lane == 2, w1, route)
    route = jnp.where(lane == 3, w2, route)
    route_ref[...] = route


def _proj_route(o, x, wo, bo, g, w_gr, b_gr, w_er, b_er):
    t = x.shape[0]
    row = lambda i: (i, 0)
    fixed = lambda i: (0, 0)
    pad = LANES - N_GROUPS - N_EXPERTS
    wr = jnp.concatenate([w_gr, w_er, jnp.zeros((D_MODEL, pad), F32)], axis=1)
    br = jnp.concatenate([b_gr, b_er, jnp.zeros((pad,), F32)]).reshape(1, LANES)
    return pl.pallas_call(
        _proj_route_kernel,
        grid=(t // ROW_TILE,),
        in_specs=[pl.BlockSpec((ROW_TILE, Q_DIM), row), pl.BlockSpec((ROW_TILE, D_MODEL), row),
                  pl.BlockSpec((Q_DIM, D_MODEL), fixed), pl.BlockSpec((1, D_MODEL), fixed),
                  pl.BlockSpec((1, D_MODEL), fixed), pl.BlockSpec((D_MODEL, LANES), fixed),
                  pl.BlockSpec((1, LANES), fixed)],
        out_specs=[pl.BlockSpec((ROW_TILE, D_MODEL), row), pl.BlockSpec((ROW_TILE, D_MODEL), row),
                   pl.BlockSpec((ROW_TILE, LANES), row)],
        out_shape=[jax.ShapeDtypeStruct((t, D_MODEL), F32), jax.ShapeDtypeStruct((t, D_MODEL), F32),
                   jax.ShapeDtypeStruct((t, LANES), F32)],
        compiler_params=_params("arbitrary"), name="proj_route",
    )(o, x, wo.astype(BF16), bo.reshape(1, D_MODEL), g.reshape(1, D_MODEL), wr, br)


def _expert_kernel(n_steps, blk_e_ref, nvalid_ref, src_ref, src_next_ref, dst_ref, wrow_ref, h_hbm,
                   wg_ref, wu_ref, wd_ref, y_hbm, xbuf, ybuf, wg_bf, wu_bf, wd_bf, gsem, ssem):
    i = pl.program_id(0)
    slot = lax.rem(i, 2)
    rows = EXPERT_ROWS
    nvalid = nvalid_ref[i]
    nvalid_next = jnp.where(i + 1 < n_steps, nvalid_ref[jnp.minimum(i + 1, n_steps - 1)], 0)
    nvalid_prev2 = jnp.where(i >= 2, nvalid_ref[jnp.maximum(i - 2, 0)], 0)

    def row_copy_in(idx_ref, r, s):
        return pltpu.make_async_copy(h_hbm.at[pl.ds(idx_ref[0, r], 1)], xbuf.at[s, pl.ds(r, 1)], gsem.at[s])

    def row_copy_out(r, s):
        return pltpu.make_async_copy(ybuf.at[s, pl.ds(r, 1)], y_hbm.at[pl.ds(dst_ref[0, r], 1)], ssem.at[s])

    def for_rows(fn, n=None):
        def body(g, carry):
            for u in range(8):
                r = g * 8 + u
                if n is None:
                    fn(r, u)
                else:
                    pl.when(r < n)(functools.partial(fn, r, u))
            return carry
        lax.fori_loop(0, rows // 8, body, 0)

    @pl.when((i == 0) & (nvalid > 0))
    def _():
        for_rows(lambda r, u: row_copy_in(src_ref, r, 0).start(priority=u % 2))

    @pl.when(nvalid_next > 0)
    def _():
        for_rows(lambda r, u: row_copy_in(src_next_ref, r, 1 - slot).start(priority=u % 2))

    @pl.when(nvalid_prev2 > 0)
    def _():
        for_rows(lambda r, u: row_copy_out(r, slot).wait(), nvalid_prev2)

    @pl.when(nvalid > 0)
    def _():
        e_changed = jnp.logical_or(i == 0, blk_e_ref[i] != blk_e_ref[jnp.maximum(i - 1, 0)])

        @pl.when(e_changed)
        def _():
            wg_bf[...] = wg_ref[0, 0].astype(BF16)
            wu_bf[...] = wu_ref[0, 0].astype(BF16)
            wd_bf[...] = wd_ref[0, 0].astype(BF16)

        for_rows(lambda r, u: row_copy_in(src_ref, r, slot).wait())
        x = xbuf[slot].astype(BF16)
        gate = jnp.dot(x, wg_bf[...], preferred_element_type=F32)
        up = jnp.dot(x, wu_bf[...], preferred_element_type=F32)
        hid = (gate * jax.nn.sigmoid(gate) * up).astype(BF16)
        y = jnp.dot(hid, wd_bf[...], preferred_element_type=F32)
        ybuf[slot] = y * wrow_ref[...]
        for_rows(lambda r, u: row_copy_out(r, slot).start(priority=u % 2), nvalid)


def _expert_ffn(h, e_idx, e_w, w_gate, w_up, w_down, layer):
    t = h.shape[0]
    a = t * EXPERT_TOPK
    rows = EXPERT_ROWS
    n_blk = -(-a // rows) + N_EXPERTS
    n_steps = n_blk + 2
    n_rows = n_steps * rows
    flat_e = e_idx.reshape(a)
    order = jnp.argsort(flat_e).astype(jnp.int32)
    counts = jnp.sum(flat_e[:, None] == jnp.arange(N_EXPERTS, dtype=jnp.int32)[None, :], axis=0, dtype=jnp.int32)
    padded = (counts + rows - 1) // rows * rows
    pad_end = jnp.cumsum(padded)
    pad_start = pad_end - padded
    start = jnp.cumsum(counts) - counts
    blk_first = jnp.arange(n_steps, dtype=jnp.int32) * rows
    blk_e = jnp.minimum(jnp.sum(blk_first[:, None] >= pad_end[None, :], axis=1), N_EXPERTS - 1).astype(jnp.int32)
    rank0 = blk_first - pad_start[blk_e]
    nvalid = jnp.clip(counts[blk_e] - rank0, 0, rows).astype(jnp.int32)
    within = jnp.arange(rows, dtype=jnp.int32)[None, :]
    valid = within < nvalid[:, None]
    assign = order[jnp.clip((start[blk_e] + rank0)[:, None] + within, 0, a - 1)]
    tok = assign // EXPERT_TOPK
    src = jnp.where(valid, tok, 0)
    dst = jnp.where(valid, (assign % EXPERT_TOPK) * t + tok, 0)
    wrow = jnp.where(valid, e_w.reshape(a)[assign], 0.0)
    idx_blk = lambda i, be, na: (i, 0, 0)
    idx_next = lambda i, be, na: (jnp.minimum(i + 1, n_steps - 1), 0, 0)
    w_in = lambda i, be, na: (layer, be[i], 0, 0)
    grid_spec = pltpu.PrefetchScalarGridSpec(
        num_scalar_prefetch=2,
        grid=(n_steps,),
        in_specs=[pl.BlockSpec((1, 1, rows), idx_blk, memory_space=pltpu.SMEM),
                  pl.BlockSpec((1, 1, rows), idx_next, memory_space=pltpu.SMEM),
                  pl.BlockSpec((1, 1, rows), idx_blk, memory_space=pltpu.SMEM),
                  pl.BlockSpec((rows, 1), lambda i, be, na: (i, 0)),
                  pl.BlockSpec(memory_space=pl.ANY),
                  pl.BlockSpec((1, 1, D_MODEL, EXPERT_FF), w_in), pl.BlockSpec((1, 1, D_MODEL, EXPERT_FF), w_in),
                  pl.BlockSpec((1, 1, EXPERT_FF, D_MODEL), w_in)],
        out_specs=pl.BlockSpec(memory_space=pl.ANY),
        scratch_shapes=[pltpu.VMEM((2, rows, D_MODEL), F32), pltpu.VMEM((2, rows, D_MODEL), F32),
                        pltpu.VMEM((D_MODEL, EXPERT_FF), BF16), pltpu.VMEM((D_MODEL, EXPERT_FF), BF16),
                        pltpu.VMEM((EXPERT_FF, D_MODEL), BF16),
                        pltpu.SemaphoreType.DMA((2,)), pltpu.SemaphoreType.DMA((2,))])

    def body(blk_e_ref, nvalid_ref, src_ref, src_next_ref, dst_ref, *rest):
        _expert_kernel(n_steps, blk_e_ref, nvalid_ref, src_ref.at[0], src_next_ref.at[0], dst_ref.at[0], *rest)

    return pl.pallas_call(
        body, grid_spec=grid_spec,
        out_shape=jax.ShapeDtypeStruct((EXPERT_TOPK * t, D_MODEL), F32),
        compiler_params=_params("arbitrary"), name="expert_ffn",
    )(blk_e, nvalid, src.reshape(n_steps, 1, rows), src.reshape(n_steps, 1, rows),
      dst.reshape(n_steps, 1, rows), wrow.reshape(n_rows, 1), h, w_gate, w_up, w_down)


def _final_kernel(x_ref, y0_ref, y1_ref, g_ref, o_ref):
    o_ref[...] = _rms(x_ref[...] + y0_ref[...] + y1_ref[...], g_ref[...])


def _final_norm(x, y2, g, row0, n_rows):
    t = x.shape[0]
    off = row0 // ROW_TILE
    return pl.pallas_call(
        _final_kernel,
        grid=(n_rows // ROW_TILE,),
        in_specs=[pl.BlockSpec((ROW_TILE, D_MODEL), lambda i: (i + off, 0)),
                  pl.BlockSpec((ROW_TILE, D_MODEL), lambda i: (i + off, 0)),
                  pl.BlockSpec((ROW_TILE, D_MODEL), lambda i: (i + off + t // ROW_TILE, 0)),
                  pl.BlockSpec((1, D_MODEL), lambda i: (0, 0))],
        out_specs=pl.BlockSpec((ROW_TILE, D_MODEL), lambda i: (i, 0)),
        out_shape=jax.ShapeDtypeStruct((n_rows, D_MODEL), F32),
        compiler_params=_params("arbitrary"), name="final_norm",
    )(x, y2, y2, g.reshape(1, D_MODEL))


def _rope_tables(pos):
    half = HEAD_DIM // 2
    inv = ROPE_THETA ** (-jnp.arange(half, dtype=F32) / half)
    ang = pos.astype(F32)[:, None] * inv[None, :]
    cos = jnp.cos(ang)
    sin = jnp.sin(ang)
    reps = LANES // HEAD_DIM
    return jnp.tile(jnp.concatenate([cos, cos], axis=1), (1, reps)), jnp.tile(jnp.concatenate([-sin, sin], axis=1), (1, reps))


def _moe(o, x, wo, bo, g, w_gr, b_gr, w_er, b_er, w_gate, w_up, w_down, layer):
    x1, h, route = _proj_route(o, x, wo, bo, g, w_gr, b_gr, w_er, b_er)
    e_idx = jnp.clip(route[:, :EXPERT_TOPK].astype(jnp.int32), 0, N_EXPERTS - 1)
    e_w = route[:, EXPERT_TOPK:2 * EXPERT_TOPK]
    return x1, _expert_ffn(h, e_idx, e_w, w_gate, w_up, w_down, layer)


def kernel(x_prompt, x_sample, cache_swa_k, cache_swa_v, cache_moba_k, cache_moba_v, page_table, norm_mix, norm_ffn, norm_final, swa_w_qkv, swa_b_qkv, swa_sinks, swa_w_o, swa_b_o, moba_w_qkv, moba_w_o, w_group_router, b_group_router, w_expert_router, b_expert_router, w_gate, w_up, w_down):
    batch, seq, _ = x_prompt.shape
    db, nq, _ = x_sample.shape
    past_len = page_table.shape[1] * PAGE_SIZE
    tp = batch * seq
    ts = db * nq
    t = tp + ts
    x = jnp.concatenate([x_prompt.reshape(tp, D_MODEL), x_sample.reshape(ts, D_MODEL)], axis=0)
    pos = jnp.concatenate([jnp.tile(jnp.arange(seq), batch), jnp.tile(past_len + jnp.arange(nq), db)])
    cos_t, sin_t = _rope_tables(pos)

    q, k0, v0, k_rep, v_rep = _qkv_rope(x, [], norm_mix[0], swa_w_qkv[0], swa_b_qkv[0], cos_t, sin_t, False)
    o_p = _swa_prompt(q, k_rep, v_rep, swa_sinks[0], batch, seq)
    o_s, swa_ks, swa_vs = _swa_sample(q, k0, v0, cache_swa_k[0], cache_swa_v[0], swa_sinks[0], tp, nq)
    o = jnp.concatenate([o_p, o_s], axis=0)
    x1, y2 = _moe(o, x, swa_w_o[0], swa_b_o[0], norm_ffn[0], w_group_router[0], b_group_router[0],
                  w_expert_router[0], b_expert_router[0], w_gate, w_up, w_down, 0)

    x, q, k1, v1, k_rep, kmean, qt, vt = _qkv_rope(
        x1, [(y2, 0), (y2, t)], norm_mix[1], moba_w_qkv[0], jnp.zeros((QKV_DIM,), F32), cos_t, sin_t, True)
    nb = seq // MOBA_BLOCK
    kmean_rep = jnp.tile(kmean[:batch * nb, 0].reshape(batch, nb, N_KV_HEADS, 1, HEAD_DIM), (1, 1, 1, GROUP, 1))
    kmean_rep = kmean_rep.reshape(batch, nb, N_KV_HEADS, GROUP_LANES).transpose(0, 2, 1, 3)
    o_p = _moba_prompt(qt, k_rep, vt, kmean_rep, batch, seq)
    qs = q[tp:].reshape(db, nq, N_HEADS, 1, HEAD_DIM)
    slot = (jnp.arange(N_HEADS) // GROUP)[:, None] == jnp.arange(N_KV_HEADS)[None, :]
    qexp = jnp.where(slot[None, None, :, :, None], qs, 0.0).transpose(0, 2, 1, 3, 4).reshape(db, N_HEADS * nq, KV_DIM)
    o_s = _moba_sample(qexp, k1, v1, cache_moba_k[0], cache_moba_v[0], page_table, tp, nq)
    o_s = o_s.reshape(db, N_HEADS, nq, N_KV_HEADS, HEAD_DIM)
    o_s = jnp.sum(jnp.where(slot[None, :, None, :, None], o_s, 0.0), axis=3).transpose(0, 2, 1, 3)
    o = jnp.concatenate([o_p, o_s.reshape(ts, Q_DIM).astype(BF16)], axis=0)
    x1, y2 = _moe(o, x, moba_w_o[0], jnp.zeros((D_MODEL,), F32), norm_ffn[1], w_group_router[1],
                  b_group_router[1], w_expert_router[1], b_expert_router[1], w_gate, w_up, w_down, 1)

    y_prompt = _final_norm(x1, y2, norm_final, 0, tp).reshape(batch, seq, D_MODEL)
    y_sample = _final_norm(x1, y2, norm_final, tp, ts).reshape(db, nq, D_MODEL)
    win = min(WINDOW, seq)
    kv5 = lambda a, b, s: a.reshape(1, b, s, N_KV_HEADS, HEAD_DIM)
    k0p = k0[:tp].reshape(batch, seq, KV_DIM)[:, seq - win:]
    v0p = v0[:tp].reshape(batch, seq, KV_DIM)[:, seq - win:]
    return (y_prompt, y_sample, kv5(k0p, batch, win), kv5(v0p, batch, win),
            kv5(swa_ks, db, swa_ks.shape[1]), kv5(swa_vs, db, swa_vs.shape[1]),
            kv5(k1[:tp], batch, seq), kv5(v1[:tp], batch, seq), kv5(k1[tp:], db, nq), kv5(v1[tp:], db, nq))
```

```python
import functools

import jax
import jax.numpy as jnp
from jax import lax
from jax.experimental import pallas as pl
from jax.experimental.pallas import tpu as pltpu

D_MODEL = 1024
N_HEADS = 16
N_KV_HEADS = 4
HEAD_DIM = 64
GROUP = N_HEADS // N_KV_HEADS
Q_DIM = N_HEADS * HEAD_DIM
KV_DIM = N_KV_HEADS * HEAD_DIM
QKV_DIM = Q_DIM + 2 * KV_DIM
ROPE_THETA = 10000.0
WINDOW = 128
MOBA_BLOCK = 256
MOBA_TOPK = 3
PAGE_SIZE = 128
N_GROUPS = 8
EXPERTS_PER_GROUP = 8
N_EXPERTS = N_GROUPS * EXPERTS_PER_GROUP
EXPERT_TOPK = 2
EXPERT_FF = D_MODEL // 4
RMS_EPS = 1e-6
NEG_INF = -1e30
BELOW_NEG_INF = -3e38
LOG2_E = 1.4426950408889634
VT_ROWS = HEAD_DIM + 16

LANES = 128
GROUP_LANES = GROUP * HEAD_DIM
ROW_TILE = 256
EXPERT_ROWS = 256
PAGES_PER_STEP = 16
VMEM_LIMIT = 56 * 1024 * 1024

F32 = jnp.float32
BF16 = jnp.bfloat16
NT_DIMS = (((1,), (1,)), ((), ()))


def _params(*sem):
    return pltpu.CompilerParams(dimension_semantics=sem, vmem_limit_bytes=VMEM_LIMIT)


def _rms(x, g):
    return x * lax.rsqrt(jnp.mean(x * x, axis=-1, keepdims=True) + RMS_EPS) * g


TOKEN_TILE = D_MODEL // LANES


def _load_token_tiles(ref, rows):
    return jnp.concatenate([ref[pl.ds(c, rows, stride=TOKEN_TILE), :] for c in range(TOKEN_TILE)], axis=1)


def _store_token_tiles(ref, x):
    for c in range(TOKEN_TILE):
        ref[pl.ds(c, x.shape[0], stride=TOKEN_TILE), :] = x[:, c * LANES:(c + 1) * LANES]


def _replicate_head(x128, odd):
    swapped = pltpu.roll(x128, HEAD_DIM, axis=1)
    low = lax.broadcasted_iota(jnp.int32, x128.shape, 1) < HEAD_DIM
    return jnp.where(low != odd, x128, swapped)


def _replicate_kv(x256, kvh):
    col = x256[:, (kvh // 2) * LANES:(kvh // 2 + 1) * LANES]
    rep = _replicate_head(col, kvh % 2 == 1)
    return jnp.concatenate([rep, rep], axis=1)


def _expand_heads(q):
    head = lax.shift_right_logical(lax.broadcasted_iota(jnp.int32, q.shape, 1), 6)
    return jnp.concatenate([jnp.where(head == h, q, 0.0) for h in range(GROUP)], axis=0)


def _collapse_heads(o_rep, rows):
    head = lax.shift_right_logical(lax.broadcasted_iota(jnp.int32, (rows, GROUP_LANES), 1), 6)
    out = jnp.zeros((rows, GROUP_LANES), F32)
    for h in range(GROUP):
        out = jnp.where(head == h, o_rep[h * rows:(h + 1) * rows], out)
    return out


def _top3_mask(gate, n_valid):
    lane = lax.broadcasted_iota(jnp.int32, gate.shape, 1)
    lane_f = lane.astype(F32)
    valid = lane < n_valid
    g = jnp.where(valid, gate, NEG_INF)
    sel = jnp.zeros(gate.shape, jnp.bool_)
    for _ in range(MOBA_TOPK):
        mx = jnp.max(g, axis=1, keepdims=True)
        first = jnp.min(jnp.where(g == mx, lane_f, 1e9), axis=1, keepdims=True)
        hit = lane_f == first
        sel = jnp.logical_or(sel, jnp.logical_and(hit, valid))
        g = jnp.where(hit, BELOW_NEG_INF, g)
    return sel


def _qkv_kernel(n_add, moba, *refs):
    refs = list(refs)
    x_ref = refs.pop(0)
    add_refs = [refs.pop(0) for _ in range(n_add)]
    g_ref, w_ref, b_ref, cos_ref, sin_ref = refs[:5]
    outs = refs[5:]
    x = x_ref[...]
    for r in add_refs:
        x = x + _load_token_tiles(r, ROW_TILE)
    if n_add:
        outs.pop(0)[...] = x
    q_ref, k_ref, v_ref, krep_ref = outs[:4]
    h = _rms(x, g_ref[...]).astype(BF16)
    y = jnp.dot(h, w_ref[...], preferred_element_type=F32) + b_ref[...]
    cos = cos_ref[...]
    sin = sin_ref[...]
    low = lax.bitwise_and(lax.broadcasted_iota(jnp.int32, cos.shape, 1), HEAD_DIM - 1) < HEAD_DIM // 2

    def rope(slab):
        partner = jnp.where(low, pltpu.roll(slab, LANES - HEAD_DIM // 2, axis=1),
                            pltpu.roll(slab, HEAD_DIM // 2, axis=1))
        return slab * cos + partner * sin

    scale = HEAD_DIM ** -0.5 * (LOG2_E if moba else 1.0)
    for c in range(Q_DIM // LANES):
        q_ref[:, c * LANES:(c + 1) * LANES] = rope(y[:, c * LANES:(c + 1) * LANES]) * scale
    k = jnp.concatenate([rope(y[:, Q_DIM + c * LANES:Q_DIM + (c + 1) * LANES])
                         for c in range(KV_DIM // LANES)], axis=1)
    v = y[:, Q_DIM + KV_DIM:]
    k_ref[...] = k
    v_ref[...] = v
    for kvh in range(N_KV_HEADS):
        krep_ref[kvh] = _replicate_kv(k, kvh).astype(BF16)
    if not moba:
        for kvh in range(N_KV_HEADS):
            outs[4][kvh] = _replicate_kv(v, kvh).astype(BF16)
    else:
        kmean_ref, qt_ref, vt_ref = outs[4:]
        kmean_ref[0] = jnp.broadcast_to(jnp.mean(k, axis=0, keepdims=True), (8, KV_DIM))
        qt_ref[...] = q_ref[...].T
        vt = v.T.reshape(N_KV_HEADS, HEAD_DIM, ROW_TILE).astype(BF16)
        vt_ref[:, 0] = jnp.concatenate([vt, jnp.ones((N_KV_HEADS, VT_ROWS - HEAD_DIM, ROW_TILE), BF16)], axis=1)


def _qkv_rope(x, adds, g, w, b, cos_t, sin_t, moba):
    t = x.shape[0]
    nt = t // ROW_TILE
    row = lambda i: (i, 0)
    fixed = lambda i: (0, 0)
    in_specs = [pl.BlockSpec((ROW_TILE, D_MODEL), row)]
    args = [x]
    for arr, off in adds:
        in_specs.append(pl.BlockSpec((ROW_TILE * TOKEN_TILE, LANES),
                                     functools.partial(lambda o, i: (i + o, 0), off // ROW_TILE)))
        args.append(arr)
    in_specs += [pl.BlockSpec((1, D_MODEL), fixed), pl.BlockSpec((D_MODEL, QKV_DIM), fixed),
                 pl.BlockSpec((1, QKV_DIM), fixed), pl.BlockSpec((ROW_TILE, LANES), row),
                 pl.BlockSpec((ROW_TILE, LANES), row)]
    args += [g.reshape(1, D_MODEL), w.astype(BF16), b.reshape(1, QKV_DIM), cos_t, sin_t]
    out_shape, out_specs = [], []
    if adds:
        out_shape.append(jax.ShapeDtypeStruct((t, D_MODEL), F32))
        out_specs.append(pl.BlockSpec((ROW_TILE, D_MODEL), row))
    rep_shape = jax.ShapeDtypeStruct((N_KV_HEADS, t, GROUP_LANES), BF16)
    rep_spec = pl.BlockSpec((N_KV_HEADS, ROW_TILE, GROUP_LANES), lambda i: (0, i, 0))
    out_shape += [jax.ShapeDtypeStruct((t, Q_DIM), F32), jax.ShapeDtypeStruct((t, KV_DIM), F32),
                  jax.ShapeDtypeStruct((t, KV_DIM), F32), rep_shape]
    out_specs += [pl.BlockSpec((ROW_TILE, Q_DIM), row), pl.BlockSpec((ROW_TILE, KV_DIM), row),
                  pl.BlockSpec((ROW_TILE, KV_DIM), row), rep_spec]
    if not moba:
        out_shape.append(rep_shape)
        out_specs.append(rep_spec)
    else:
        out_shape += [jax.ShapeDtypeStruct((nt, 8, KV_DIM), F32), jax.ShapeDtypeStruct((Q_DIM, t), F32),
                      jax.ShapeDtypeStruct((N_KV_HEADS, nt, VT_ROWS, ROW_TILE), BF16)]
        out_specs += [pl.BlockSpec((1, 8, KV_DIM), lambda i: (i, 0, 0)),
                      pl.BlockSpec((Q_DIM, ROW_TILE), lambda i: (0, i)),
                      pl.BlockSpec((N_KV_HEADS, 1, VT_ROWS, ROW_TILE), lambda i: (0, i, 0, 0))]
    return pl.pallas_call(
        functools.partial(_qkv_kernel, len(adds), moba),
        grid=(nt,), in_specs=in_specs, out_specs=out_specs, out_shape=out_shape,
        compiler_params=_params("arbitrary"), name="qkv_rope")(*args)


def _sink_softmax_pv(s, ok, sink_col, v):
    s = jnp.where(ok, s, NEG_INF)
    m = jnp.maximum(jnp.max(s, axis=1, keepdims=True), sink_col)
    p = jnp.exp(s - m)
    denom = jnp.sum(p, axis=1, keepdims=True) + jnp.exp(sink_col - m)
    return jnp.dot(p.astype(BF16), v, preferred_element_type=F32) / denom


def _sink_column(sinks_ref, kvh, rows):
    blk = lax.shift_right_logical(lax.broadcasted_iota(jnp.int32, (GROUP * rows, 1), 0), rows.bit_length() - 1)
    col = jnp.zeros((GROUP * rows, 1), F32)
    for h in range(GROUP):
        col = jnp.where(blk == h, sinks_ref[kvh * GROUP + h], col)
    return col


def _swa_prompt_kernel(sinks_ref, q_ref, kprev_ref, kcur_ref, vprev_ref, vcur_ref, o_ref):
    kvh = pl.program_id(1)
    blk = pl.program_id(2)
    rows = WINDOW
    qexp = _expand_heads(q_ref[...]).astype(BF16)
    k = jnp.concatenate([kprev_ref[0], kcur_ref[0]], axis=0)
    v = jnp.concatenate([vprev_ref[0], vcur_ref[0]], axis=0)
    s = lax.dot_general(qexp, k, NT_DIMS, preferred_element_type=F32)
    qi = lax.bitwise_and(lax.broadcasted_iota(jnp.int32, s.shape, 0), rows - 1)
    kj = lax.broadcasted_iota(jnp.int32, s.shape, 1)
    first_key = jnp.where(blk > 0, 0, WINDOW)
    ok = (kj > qi) & (kj <= qi + WINDOW) & (kj >= first_key)
    o_rep = _sink_softmax_pv(s, ok, _sink_column(sinks_ref, kvh, rows), v)
    o_ref[...] = _collapse_heads(o_rep, rows).astype(BF16)


def _swa_prompt(q, k_rep, v_rep, sinks, batch, seq):
    t = batch * seq
    nb = seq // WINDOW
    cur = lambda b, h, i: (h, b * nb + i, 0)
    prev = lambda b, h, i: (h, b * nb + jnp.maximum(i - 1, 0), 0)
    qo = lambda b, h, i: (b * nb + i, h)
    return pl.pallas_call(
        _swa_prompt_kernel,
        grid=(batch, N_KV_HEADS, nb),
        in_specs=[pl.BlockSpec(memory_space=pltpu.SMEM),
                  pl.BlockSpec((WINDOW, GROUP_LANES), qo),
                  pl.BlockSpec((1, WINDOW, GROUP_LANES), prev), pl.BlockSpec((1, WINDOW, GROUP_LANES), cur),
                  pl.BlockSpec((1, WINDOW, GROUP_LANES), prev), pl.BlockSpec((1, WINDOW, GROUP_LANES), cur)],
        out_specs=pl.BlockSpec((WINDOW, GROUP_LANES), qo),
        out_shape=jax.ShapeDtypeStruct((t, Q_DIM), BF16),
        compiler_params=_params("arbitrary", "arbitrary", "arbitrary"), name="swa_prompt",
    )(sinks, q, k_rep, k_rep, v_rep, v_rep)


def _swa_sample_kernel(seqs, nq, sinks_ref, q_ref, kn_ref, vn_ref, ck_ref, cv_ref, o_ref, ko_ref, vo_ref):
    win = ck_ref.shape[1]

    def one_seq(s, carry):
        r0 = pl.multiple_of(s * nq, nq)
        q = q_ref[pl.ds(r0, nq), :]
        kn = kn_ref[pl.ds(r0, nq), :]
        vn = vn_ref[pl.ds(r0, nq), :]
        ck = ck_ref[s]
        cv = cv_ref[s]
        ko_ref[s] = jnp.concatenate([ck[nq:], kn], axis=0)
        vo_ref[s] = jnp.concatenate([cv[nq:], vn], axis=0)
        kk = jnp.concatenate([ck, kn], axis=0)
        vv = jnp.concatenate([cv, vn], axis=0)
        outs = []
        for kvh in range(N_KV_HEADS):
            qexp = _expand_heads(q[:, kvh * GROUP_LANES:(kvh + 1) * GROUP_LANES]).astype(BF16)
            k = _replicate_kv(kk, kvh).astype(BF16)
            v = _replicate_kv(vv, kvh).astype(BF16)
            sc = lax.dot_general(qexp, k, NT_DIMS, preferred_element_type=F32)
            qi = lax.bitwise_and(lax.broadcasted_iota(jnp.int32, sc.shape, 0), nq - 1)
            kj = lax.broadcasted_iota(jnp.int32, sc.shape, 1)
            ok = ((kj < win) & (kj > qi + win - WINDOW)) | ((kj >= win) & (kj - win <= qi))
            o_rep = _sink_softmax_pv(sc, ok, _sink_column(sinks_ref, kvh, nq), v)
            outs.append(_collapse_heads(o_rep, nq))
        o_ref[pl.ds(r0, nq), :] = jnp.concatenate(outs, axis=1).astype(BF16)
        return carry

    lax.fori_loop(0, seqs, one_seq, 0)


def _swa_sample(q, k, v, cache_k, cache_v, sinks, row0, nq):
    db, win = cache_k.shape[0], cache_k.shape[1]
    seqs = 8
    rows = seqs * nq
    tok = lambda g: (row0 // rows + g, 0)
    cache = lambda g: (g, 0, 0)
    return pl.pallas_call(
        functools.partial(_swa_sample_kernel, seqs, nq),
        grid=(db // seqs,),
        in_specs=[pl.BlockSpec(memory_space=pltpu.SMEM),
                  pl.BlockSpec((rows, Q_DIM), tok), pl.BlockSpec((rows, KV_DIM), tok),
                  pl.BlockSpec((rows, KV_DIM), tok),
                  pl.BlockSpec((seqs, win, KV_DIM), cache), pl.BlockSpec((seqs, win, KV_DIM), cache)],
        out_specs=[pl.BlockSpec((rows, Q_DIM), lambda g: (g, 0)),
                   pl.BlockSpec((seqs, win, KV_DIM), cache), pl.BlockSpec((seqs, win, KV_DIM), cache)],
        out_shape=[jax.ShapeDtypeStruct((db * nq, Q_DIM), BF16),
                   jax.ShapeDtypeStruct((db, win, KV_DIM), F32), jax.ShapeDtypeStruct((db, win, KV_DIM), F32)],
        compiler_params=_params("arbitrary"), name="swa_sample",
    )(sinks, q, k, v, cache_k.reshape(db, win, KV_DIM), cache_v.reshape(db, win, KV_DIM))


def _top3_rows(gate, n_valid):
    row = lax.broadcasted_iota(jnp.int32, gate.shape, 0)
    row_f = row.astype(F32)
    valid = row < n_valid
    g = jnp.where(valid, gate, NEG_INF)
    sel = jnp.zeros(gate.shape, F32)
    for _ in range(MOBA_TOPK):
        mx = jnp.max(g, axis=0, keepdims=True)
        first = jnp.min(jnp.where(g == mx, row_f, 1e9), axis=0, keepdims=True)
        hit = row_f == first
        sel = jnp.where(jnp.logical_and(hit, valid), 1.0, sel)
        g = jnp.where(hit, BELOW_NEG_INF, g)
    return sel


def _moba_prompt_kernel(qt_ref, k_ref, vt_ref, kmean_ref, o_ref, sel_ref, acc_ref):
    i = pl.program_id(2)
    qt = qt_ref[...]
    row_head = lax.shift_right_logical(lax.broadcasted_iota(jnp.int32, qt.shape, 0), 6)
    qexp32 = jnp.concatenate([jnp.where(row_head == h, qt, 0.0) for h in range(GROUP)], axis=1)
    qexp = qexp32.astype(BF16)
    gate = jnp.dot(kmean_ref[0, 0], qexp32, preferred_element_type=F32, precision=lax.Precision.HIGHEST)
    sel_ref[...] = _top3_rows(gate, i)

    def scores(j):
        start = pl.multiple_of(j * MOBA_BLOCK, MOBA_BLOCK)
        return jnp.dot(k_ref[0, pl.ds(start, MOBA_BLOCK), :], qexp, preferred_element_type=F32)

    s = scores(i)
    key = lax.broadcasted_iota(jnp.int32, s.shape, 0)
    qry = lax.bitwise_and(lax.broadcasted_iota(jnp.int32, s.shape, 1), MOBA_BLOCK - 1)
    s = jnp.where(key <= qry, s, NEG_INF)
    m = jnp.max(s, axis=0, keepdims=True)
    acc_ref[...] = jnp.dot(vt_ref[0, i], jnp.exp2(s - m).astype(BF16), preferred_element_type=F32)

    def past_block(j, m_old):
        s = scores(j)
        chosen = sel_ref[pl.ds(j, 1), :] > 0.0
        m_new = jnp.where(chosen, jnp.maximum(m_old, jnp.max(s, axis=0, keepdims=True)), m_old)
        alpha = jnp.exp2(m_old - m_new)
        p = jnp.exp2(s - jnp.where(chosen, m_new, -NEG_INF))
        acc_ref[...] = alpha * acc_ref[...] + jnp.dot(vt_ref[0, j], p.astype(BF16), preferred_element_type=F32)
        return m_new

    def past_pair(t, m_old):
        return past_block(2 * t + 1, past_block(2 * t, m_old))

    lax.fori_loop(0, lax.shift_right_logical(i + 1, 1), past_pair, m)
    ot = acc_ref[:HEAD_DIM] / acc_ref[HEAD_DIM:HEAD_DIM + 1]
    ot = jnp.concatenate([ot[:, h * MOBA_BLOCK:(h + 1) * MOBA_BLOCK] for h in range(GROUP)], axis=0)
    o_ref[...] = ot.T.astype(BF16)


def _moba_prompt(qt, k_rep, vt, kmean_rep, batch, seq):
    nb = seq // MOBA_BLOCK
    cols = GROUP * MOBA_BLOCK
    return pl.pallas_call(
        _moba_prompt_kernel,
        grid=(batch, N_KV_HEADS, nb),
        in_specs=[pl.BlockSpec((GROUP_LANES, MOBA_BLOCK), lambda b, h, i: (h, b * nb + i)),
                  pl.BlockSpec((1, seq, GROUP_LANES), lambda b, h, i: (h, b, 0)),
                  pl.BlockSpec((1, nb, VT_ROWS, MOBA_BLOCK), lambda b, h, i: (h, b, 0, 0)),
                  pl.BlockSpec((1, 1, nb, GROUP_LANES), lambda b, h, i: (b, h, 0, 0))],
        out_specs=pl.BlockSpec((MOBA_BLOCK, GROUP_LANES), lambda b, h, i: (b * nb + i, h)),
        out_shape=jax.ShapeDtypeStruct((batch * seq, Q_DIM), BF16),
        scratch_shapes=[pltpu.VMEM((nb, cols), F32), pltpu.VMEM((VT_ROWS, cols), F32)],
        compiler_params=_params("arbitrary", "arbitrary", "arbitrary"), name="moba_prompt",
    )(qt, k_rep, vt, kmean_rep)


def _moba_sample_kernel(nq, n_chunks, n_seq, pt_ref, qexp_ref, kn_ref, vn_ref, expand_ref, pk_hbm, pv_hbm,
                        o_ref, cbuf, s_ref, sem):
    b = pl.program_id(0)
    pages = PAGES_PER_STEP
    keys = pages * PAGE_SIZE
    blocks_per_chunk = keys // MOBA_BLOCK
    n_blocks = n_chunks * blocks_per_chunk
    n_stream = 2 * n_chunks

    def chunk_copies(seq, j, lookup):
        pool = pk_hbm if j < n_chunks else pv_hbm
        first = (j % n_chunks) * pages
        return [pltpu.make_async_copy(pool.at[pt_ref[seq, first + p] if lookup else 0],
                                      cbuf.at[j % 2, p], sem.at[j % 2])
                for p in range(pages)]

    def chunk(j):
        return jnp.concatenate([cbuf[j % 2, p] for p in range(pages)], axis=1)

    def start(seq, j):
        for p, cp in enumerate(chunk_copies(seq, j, True)):
            cp.start(priority=p % 2)

    @pl.when(b == 0)
    def _():
        start(b, 0)

    qexp32 = qexp_ref[0]
    qexp = qexp32.astype(BF16)
    blk_lane = lax.broadcasted_iota(jnp.int32, (KV_DIM, LANES), 1)
    ksum = jnp.zeros((KV_DIM, LANES), F32)
    for j in range(n_stream):
        if j + 1 < n_stream:
            start(b, j + 1)
        else:
            @pl.when(b + 1 < n_seq)
            def _():
                start(b + 1, 0)
        for cp in chunk_copies(b, j, False):
            cp.wait()
        cols = slice((j % n_chunks) * keys, (j % n_chunks + 1) * keys)
        if j < n_chunks:
            kt = chunk(j)
            s_ref[:, cols] = jnp.dot(qexp, kt.astype(BF16), preferred_element_type=F32)
            for n in range(blocks_per_chunk):
                blk_sum = jnp.sum(kt[:, n * MOBA_BLOCK:(n + 1) * MOBA_BLOCK], axis=1, keepdims=True)
                ksum = jnp.where(blk_lane == j * blocks_per_chunk + n, blk_sum, ksum)
        if j == n_chunks - 1:
            gate = jnp.dot(qexp32, ksum * (1.0 / MOBA_BLOCK), preferred_element_type=F32,
                           precision=lax.Precision.HIGHEST)
            sel_bias = jnp.where(_top3_mask(gate, n_blocks), 0.0, NEG_INF).astype(BF16)
            s_own = lax.dot_general(qexp, kn_ref[...].astype(BF16), NT_DIMS, preferred_element_type=F32)
            qi = lax.bitwise_and(lax.broadcasted_iota(jnp.int32, s_own.shape, 0), nq - 1)
            kj = lax.broadcasted_iota(jnp.int32, s_own.shape, 1)
            s_own = jnp.where(kj <= qi, s_own, NEG_INF)
            m_tile = jnp.full((N_HEADS * nq, MOBA_BLOCK), NEG_INF, F32)
            for c in range(n_chunks):
                cc = slice(c * keys, (c + 1) * keys)
                sm = s_ref[:, cc] + jnp.dot(sel_bias, expand_ref[:, cc], preferred_element_type=F32)
                s_ref[:, cc] = sm
                for n in range(blocks_per_chunk):
                    m_tile = jnp.maximum(m_tile, sm[:, n * MOBA_BLOCK:(n + 1) * MOBA_BLOCK])
            m = jnp.maximum(jnp.max(m_tile, axis=1, keepdims=True), jnp.max(s_own, axis=1, keepdims=True))
            p_own = jnp.exp2(s_own - m)
            l_own = jnp.sum(p_own, axis=1, keepdims=True)
            acc = jnp.dot(p_own.astype(BF16), vn_ref[...].astype(BF16), preferred_element_type=F32)
            l_tile = jnp.zeros((N_HEADS * nq, MOBA_BLOCK), F32)
        if j >= n_chunks:
            p = jnp.exp2(s_ref[:, cols] - m)
            for n in range(blocks_per_chunk):
                l_tile = l_tile + p[:, n * MOBA_BLOCK:(n + 1) * MOBA_BLOCK]
            acc = acc + lax.dot_general(p.astype(BF16), chunk(j).astype(BF16), NT_DIMS,
                                        preferred_element_type=F32)
    o_ref[0] = acc / (l_own + jnp.sum(l_tile, axis=1, keepdims=True))


def _moba_sample(qexp, k, v, pool_k, pool_v, page_table, row0, nq):
    db, n_pages = page_table.shape
    n_pool = pool_k.shape[0]
    pages = PAGES_PER_STEP
    n_chunks = n_pages // pages
    rows = N_HEADS * nq
    pk = pool_k.transpose(0, 2, 3, 1).reshape(n_pool, KV_DIM, PAGE_SIZE)
    pv = pool_v.transpose(0, 2, 3, 1).reshape(n_pool, KV_DIM, PAGE_SIZE)
    n_keys = n_pages * PAGE_SIZE
    expand = (jnp.arange(LANES)[:, None] == (jnp.arange(n_keys) // MOBA_BLOCK)[None, :]).astype(BF16)
    new = lambda b, pt: (row0 // nq + b, 0)
    grid_spec = pltpu.PrefetchScalarGridSpec(
        num_scalar_prefetch=1,
        grid=(db,),
        in_specs=[pl.BlockSpec((1, rows, KV_DIM), lambda b, pt: (b, 0, 0)),
                  pl.BlockSpec((nq, KV_DIM), new), pl.BlockSpec((nq, KV_DIM), new),
                  pl.BlockSpec((LANES, n_keys), lambda b, pt: (0, 0)),
                  pl.BlockSpec(memory_space=pl.ANY), pl.BlockSpec(memory_space=pl.ANY)],
        out_specs=pl.BlockSpec((1, rows, KV_DIM), lambda b, pt: (b, 0, 0)),
        scratch_shapes=[pltpu.VMEM((2, pages, KV_DIM, PAGE_SIZE), F32),
                        pltpu.VMEM((rows, n_keys), F32),
                        pltpu.SemaphoreType.DMA((2,))])
    return pl.pallas_call(
        functools.partial(_moba_sample_kernel, nq, n_chunks, db),
        grid_spec=grid_spec,
        out_shape=jax.ShapeDtypeStruct((db, rows, KV_DIM), F32),
        compiler_params=_params("arbitrary"), name="moba_sample",
    )(page_table, qexp, k, v, expand, pk, pv)


def _proj_route_kernel(o_ref, x_ref, wo_ref, bo_ref, g_ref, wr_ref, br_ref, x1_ref, h_ref, route_ref):
    x1 = x_ref[...] + jnp.dot(o_ref[...], wo_ref[...], preferred_element_type=F32) + bo_ref[...]
    x1_ref[...] = x1
    h = _rms(x1, g_ref[...])
    _store_token_tiles(h_ref, h)
    lg = jnp.dot(h, wr_ref[...], preferred_element_type=F32, precision=lax.Precision.HIGHEST) + br_ref[...]
    lane = lax.broadcasted_iota(jnp.int32, lg.shape, 1)
    lane_f = lane.astype(F32)
    is_group = lane < N_GROUPS
    gl = jnp.where(is_group, lg, NEG_INF)
    gmax = jnp.max(gl, axis=1, keepdims=True)
    grp = jnp.min(jnp.where(gl == gmax, lane_f, 1e9), axis=1, keepdims=True)
    g_w = 1.0 / jnp.sum(jnp.where(is_group, jnp.exp(gl - gmax), 0.0), axis=1, keepdims=True)
    lane_grp = lax.shift_right_logical(lane - N_GROUPS, 3).astype(F32)
    in_grp = (lane >= N_GROUPS) & (lane < N_GROUPS + N_EXPERTS) & (lane_grp == grp)
    el = jnp.where(in_grp, lg, NEG_INF)
    v1 = jnp.max(el, axis=1, keepdims=True)
    i1 = jnp.min(jnp.where(el == v1, lane_f, 1e9), axis=1, keepdims=True)
    el2 = jnp.where(lane_f == i1, BELOW_NEG_INF, el)
    v2 = jnp.max(el2, axis=1, keepdims=True)
    i2 = jnp.min(jnp.where(el2 == v2, lane_f, 1e9), axis=1, keepdims=True)
    e2w = jnp.exp(v2 - v1)
    w1 = g_w / (1.0 + e2w)
    w2 = g_w * e2w / (1.0 + e2w)
    route = jnp.where(lane == 0, i1 - N_GROUPS, 0.0)
    route = jnp.where(lane == 1, i2 - N_GROUPS, route)
    route = jnp.where(lane == 2, w1, route)
    route = jnp.where(lane == 3, w2, route)
    route_ref[...] = route


def _proj_route(o, x, wo, bo, g, w_gr, b_gr, w_er, b_er):
    t = x.shape[0]
    row = lambda i: (i, 0)
    fixed = lambda i: (0, 0)
    pad = LANES - N_GROUPS - N_EXPERTS
    wr = jnp.concatenate([w_gr, w_er, jnp.zeros((D_MODEL, pad), F32)], axis=1)
    br = jnp.concatenate([b_gr, b_er, jnp.zeros((pad,), F32)]).reshape(1, LANES)
    return pl.pallas_call(
        _proj_route_kernel,
        grid=(t // ROW_TILE,),
        in_specs=[pl.BlockSpec((ROW_TILE, Q_DIM), row), pl.BlockSpec((ROW_TILE, D_MODEL), row),
                  pl.BlockSpec((Q_DIM, D_MODEL), fixed), pl.BlockSpec((1, D_MODEL), fixed),
                  pl.BlockSpec((1, D_MODEL), fixed), pl.BlockSpec((D_MODEL, LANES), fixed),
                  pl.BlockSpec((1, LANES), fixed)],
        out_specs=[pl.BlockSpec((ROW_TILE, D_MODEL), row), pl.BlockSpec((ROW_TILE * TOKEN_TILE, LANES), row),
                   pl.BlockSpec((ROW_TILE, LANES), row)],
        out_shape=[jax.ShapeDtypeStruct((t, D_MODEL), F32), jax.ShapeDtypeStruct((t * TOKEN_TILE, LANES), F32),
                   jax.ShapeDtypeStruct((t, LANES), F32)],
        compiler_params=_params("arbitrary"), name="proj_route",
    )(o, x, wo.astype(BF16), bo.reshape(1, D_MODEL), g.reshape(1, D_MODEL), wr, br)


def _expert_kernel(n_steps, blk_e_ref, nvalid_ref, src_ref, src_next_ref, dst_ref, wrow_ref, h_hbm,
                   wg_ref, wu_ref, wd_ref, y_hbm, xbuf, ybuf, wg_bf, wu_bf, wd_bf, gsem, ssem):
    i = pl.program_id(0)
    slot = lax.rem(i, 2)
    rows = EXPERT_ROWS
    nvalid = nvalid_ref[i]
    nvalid_next = jnp.where(i + 1 < n_steps, nvalid_ref[jnp.minimum(i + 1, n_steps - 1)], 0)
    nvalid_prev2 = jnp.where(i >= 2, nvalid_ref[jnp.maximum(i - 2, 0)], 0)

    def tile(ref, first):
        return ref.at[pl.ds(pl.multiple_of(first, TOKEN_TILE), TOKEN_TILE)]

    def row_copy_in(idx_ref, r, s):
        return pltpu.make_async_copy(tile(h_hbm, idx_ref[0, r]), tile(xbuf.at[s], r * TOKEN_TILE), gsem.at[s])

    def row_copy_out(r, s):
        return pltpu.make_async_copy(tile(ybuf.at[s], r * TOKEN_TILE), tile(y_hbm, dst_ref[0, r]), ssem.at[s])

    def for_rows(fn, n=None):
        def body(g, carry):
            for u in range(8):
                r = g * 8 + u
                if n is None:
                    fn(r, u)
                else:
                    pl.when(r < n)(functools.partial(fn, r, u))
            return carry
        lax.fori_loop(0, rows // 8, body, 0)

    @pl.when((i == 0) & (nvalid > 0))
    def _():
        for_rows(lambda r, u: row_copy_in(src_ref, r, 0).start(priority=u % 2))

    @pl.when(nvalid_next > 0)
    def _():
        for_rows(lambda r, u: row_copy_in(src_next_ref, r, 1 - slot).start(priority=u % 2))

    @pl.when(nvalid_prev2 > 0)
    def _():
        for_rows(lambda r, u: row_copy_out(r, slot).wait(), nvalid_prev2)

    @pl.when(nvalid > 0)
    def _():
        e_changed = jnp.logical_or(i == 0, blk_e_ref[i] != blk_e_ref[jnp.maximum(i - 1, 0)])

        @pl.when(e_changed)
        def _():
            wg_bf[...] = wg_ref[0, 0].astype(BF16)
            wu_bf[...] = wu_ref[0, 0].astype(BF16)
            wd_bf[...] = wd_ref[0, 0].astype(BF16)

        for_rows(lambda r, u: row_copy_in(src_ref, r, slot).wait())
        x = _load_token_tiles(xbuf.at[slot], rows).astype(BF16)
        gate = jnp.dot(x, wg_bf[...], preferred_element_type=F32)
        up = jnp.dot(x, wu_bf[...], preferred_element_type=F32)
        hid = (gate * jax.nn.sigmoid(gate) * up).astype(BF16)
        y = jnp.dot(hid, wd_bf[...], preferred_element_type=F32)
        _store_token_tiles(ybuf.at[slot], y * wrow_ref[...])
        for_rows(lambda r, u: row_copy_out(r, slot).start(priority=u % 2), nvalid)


def _expert_ffn(h, e_idx, e_w, w_gate, w_up, w_down, layer):
    t = h.shape[0] // TOKEN_TILE
    a = t * EXPERT_TOPK
    rows = EXPERT_ROWS
    n_blk = -(-a // rows) + N_EXPERTS
    n_steps = n_blk + 2
    n_rows = n_steps * rows
    flat_e = e_idx.reshape(a)
    order = jnp.argsort(flat_e).astype(jnp.int32)
    counts = jnp.sum(flat_e[:, None] == jnp.arange(N_EXPERTS, dtype=jnp.int32)[None, :], axis=0, dtype=jnp.int32)
    padded = (counts + rows - 1) // rows * rows
    pad_end = jnp.cumsum(padded)
    pad_start = pad_end - padded
    start = jnp.cumsum(counts) - counts
    blk_first = jnp.arange(n_steps, dtype=jnp.int32) * rows
    blk_e = jnp.minimum(jnp.sum(blk_first[:, None] >= pad_end[None, :], axis=1), N_EXPERTS - 1).astype(jnp.int32)
    rank0 = blk_first - pad_start[blk_e]
    nvalid = jnp.clip(counts[blk_e] - rank0, 0, rows).astype(jnp.int32)
    within = jnp.arange(rows, dtype=jnp.int32)[None, :]
    valid = within < nvalid[:, None]
    assign = order[jnp.clip((start[blk_e] + rank0)[:, None] + within, 0, a - 1)]
    tok = assign // EXPERT_TOPK
    src = jnp.where(valid, tok, 0) * TOKEN_TILE
    dst = jnp.where(valid, (assign % EXPERT_TOPK) * t + tok, 0) * TOKEN_TILE
    wrow = jnp.where(valid, e_w.reshape(a)[assign], 0.0)
    idx_blk = lambda i, be, na: (i, 0, 0)
    idx_next = lambda i, be, na: (jnp.minimum(i + 1, n_steps - 1), 0, 0)
    w_in = lambda i, be, na: (layer, be[i], 0, 0)
    grid_spec = pltpu.PrefetchScalarGridSpec(
        num_scalar_prefetch=2,
        grid=(n_steps,),
        in_specs=[pl.BlockSpec((1, 1, rows), idx_blk, memory_space=pltpu.SMEM),
                  pl.BlockSpec((1, 1, rows), idx_next, memory_space=pltpu.SMEM),
                  pl.BlockSpec((1, 1, rows), idx_blk, memory_space=pltpu.SMEM),
                  pl.BlockSpec((rows, 1), lambda i, be, na: (i, 0)),
                  pl.BlockSpec(memory_space=pl.ANY),
                  pl.BlockSpec((1, 1, D_MODEL, EXPERT_FF), w_in), pl.BlockSpec((1, 1, D_MODEL, EXPERT_FF), w_in),
                  pl.BlockSpec((1, 1, EXPERT_FF, D_MODEL), w_in)],
        out_specs=pl.BlockSpec(memory_space=pl.ANY),
        scratch_shapes=[pltpu.VMEM((2, rows * TOKEN_TILE, LANES), F32), pltpu.VMEM((2, rows * TOKEN_TILE, LANES), F32),
                        pltpu.VMEM((D_MODEL, EXPERT_FF), BF16), pltpu.VMEM((D_MODEL, EXPERT_FF), BF16),
                        pltpu.VMEM((EXPERT_FF, D_MODEL), BF16),
                        pltpu.SemaphoreType.DMA((2,)), pltpu.SemaphoreType.DMA((2,))])

    def body(blk_e_ref, nvalid_ref, src_ref, src_next_ref, dst_ref, *rest):
        _expert_kernel(n_steps, blk_e_ref, nvalid_ref, src_ref.at[0], src_next_ref.at[0], dst_ref.at[0], *rest)

    return pl.pallas_call(
        body, grid_spec=grid_spec,
        out_shape=jax.ShapeDtypeStruct((EXPERT_TOPK * t * TOKEN_TILE, LANES), F32),
        compiler_params=_params("arbitrary"), name="expert_ffn",
    )(blk_e, nvalid, src.reshape(n_steps, 1, rows), src.reshape(n_steps, 1, rows),
      dst.reshape(n_steps, 1, rows), wrow.reshape(n_rows, 1), h, w_gate, w_up, w_down)


def _final_kernel(x_ref, y0_ref, y1_ref, g_ref, o_ref):
    y = _load_token_tiles(y0_ref, ROW_TILE) + _load_token_tiles(y1_ref, ROW_TILE)
    o_ref[...] = _rms(x_ref[...] + y, g_ref[...])


def _final_norm(x, y2, g, row0, n_rows):
    t = x.shape[0]
    off = row0 // ROW_TILE
    return pl.pallas_call(
        _final_kernel,
        grid=(n_rows // ROW_TILE,),
        in_specs=[pl.BlockSpec((ROW_TILE, D_MODEL), lambda i: (i + off, 0)),
                  pl.BlockSpec((ROW_TILE * TOKEN_TILE, LANES), lambda i: (i + off, 0)),
                  pl.BlockSpec((ROW_TILE * TOKEN_TILE, LANES), lambda i: (i + off + t // ROW_TILE, 0)),
                  pl.BlockSpec((1, D_MODEL), lambda i: (0, 0))],
        out_specs=pl.BlockSpec((ROW_TILE, D_MODEL), lambda i: (i, 0)),
        out_shape=jax.ShapeDtypeStruct((n_rows, D_MODEL), F32),
        compiler_params=_params("arbitrary"), name="final_norm",
    )(x, y2, y2, g.reshape(1, D_MODEL))


def _rope_tables(pos):
    half = HEAD_DIM // 2
    inv = ROPE_THETA ** (-jnp.arange(half, dtype=F32) / half)
    ang = pos.astype(F32)[:, None] * inv[None, :]
    cos = jnp.cos(ang)
    sin = jnp.sin(ang)
    reps = LANES // HEAD_DIM
    return jnp.tile(jnp.concatenate([cos, cos], axis=1), (1, reps)), jnp.tile(jnp.concatenate([-sin, sin], axis=1), (1, reps))


def _moe(o, x, wo, bo, g, w_gr, b_gr, w_er, b_er, w_gate, w_up, w_down, layer):
    x1, h, route = _proj_route(o, x, wo, bo, g, w_gr, b_gr, w_er, b_er)
    e_idx = jnp.clip(route[:, :EXPERT_TOPK].astype(jnp.int32), 0, N_EXPERTS - 1)
    e_w = route[:, EXPERT_TOPK:2 * EXPERT_TOPK]
    return x1, _expert_ffn(h, e_idx, e_w, w_gate, w_up, w_down, layer)


def kernel(x_prompt, x_sample, cache_swa_k, cache_swa_v, cache_moba_k, cache_moba_v, page_table, norm_mix, norm_ffn, norm_final, swa_w_qkv, swa_b_qkv, swa_sinks, swa_w_o, swa_b_o, moba_w_qkv, moba_w_o, w_group_router, b_group_router, w_expert_router, b_expert_router, w_gate, w_up, w_down):
    batch, seq, _ = x_prompt.shape
    db, nq, _ = x_sample.shape
    past_len = page_table.shape[1] * PAGE_SIZE
    tp = batch * seq
    ts = db * nq
    t = tp + ts
    x = jnp.concatenate([x_prompt.reshape(tp, D_MODEL), x_sample.reshape(ts, D_MODEL)], axis=0)
    pos = jnp.concatenate([jnp.tile(jnp.arange(seq), batch), jnp.tile(past_len + jnp.arange(nq), db)])
    cos_t, sin_t = _rope_tables(pos)

    q, k0, v0, k_rep, v_rep = _qkv_rope(x, [], norm_mix[0], swa_w_qkv[0], swa_b_qkv[0], cos_t, sin_t, False)
    o_p = _swa_prompt(q, k_rep, v_rep, swa_sinks[0], batch, seq)
    o_s, swa_ks, swa_vs = _swa_sample(q, k0, v0, cache_swa_k[0], cache_swa_v[0], swa_sinks[0], tp, nq)
    o = jnp.concatenate([o_p, o_s], axis=0)
    x1, y2 = _moe(o, x, swa_w_o[0], swa_b_o[0], norm_ffn[0], w_group_router[0], b_group_router[0],
                  w_expert_router[0], b_expert_router[0], w_gate, w_up, w_down, 0)

    x, q, k1, v1, k_rep, kmean, qt, vt = _qkv_rope(
        x1, [(y2, 0), (y2, t)], norm_mix[1], moba_w_qkv[0], jnp.zeros((QKV_DIM,), F32), cos_t, sin_t, True)
    nb = seq // MOBA_BLOCK
    kmean_rep = jnp.tile(kmean[:batch * nb, 0].reshape(batch, nb, N_KV_HEADS, 1, HEAD_DIM), (1, 1, 1, GROUP, 1))
    kmean_rep = kmean_rep.reshape(batch, nb, N_KV_HEADS, GROUP_LANES).transpose(0, 2, 1, 3)
    o_p = _moba_prompt(qt, k_rep, vt, kmean_rep, batch, seq)
    qs = q[tp:].reshape(db, nq, N_HEADS, 1, HEAD_DIM)
    slot = (jnp.arange(N_HEADS) // GROUP)[:, None] == jnp.arange(N_KV_HEADS)[None, :]
    qexp = jnp.where(slot[None, None, :, :, None], qs, 0.0).transpose(0, 2, 1, 3, 4).reshape(db, N_HEADS * nq, KV_DIM)
    o_s = _moba_sample(qexp, k1, v1, cache_moba_k[0], cache_moba_v[0], page_table, tp, nq)
    o_s = o_s.reshape(db, N_HEADS, nq, N_KV_HEADS, HEAD_DIM)
    o_s = jnp.sum(jnp.where(slot[None, :, None, :, None], o_s, 0.0), axis=3).transpose(0, 2, 1, 3)
    o = jnp.concatenate([o_p, o_s.reshape(ts, Q_DIM).astype(BF16)], axis=0)
    x1, y2 = _moe(o, x, moba_w_o[0], jnp.zeros((D_MODEL,), F32), norm_ffn[1], w_group_router[1],
                  b_group_router[1], w_expert_router[1], b_expert_router[1], w_gate, w_up, w_down, 1)

    y_prompt = _final_norm(x1, y2, norm_final, 0, tp).reshape(batch, seq, D_MODEL)
    y_sample = _final_norm(x1, y2, norm_final, tp, ts).reshape(db, nq, D_MODEL)
    win = min(WINDOW, seq)
    kv5 = lambda a, b, s: a.reshape(1, b, s, N_KV_HEADS, HEAD_DIM)
    k0p = k0[:tp].reshape(batch, seq, KV_DIM)[:, seq - win:]
    v0p = v0[:tp].reshape(batch, seq, KV_DIM)[:, seq - win:]
    return (y_prompt, y_sample, kv5(k0p, batch, win), kv5(v0p, batch, win),
            kv5(swa_ks, db, swa_ks.shape[1]), kv5(swa_vs, db, swa_vs.shape[1]),
            kv5(k1[:tp], batch, seq), kv5(v1[:tp], batch, seq), kv5(k1[tp:], db, nq), kv5(v1[tp:], db, nq))
```

```python
import functools

import jax
import jax.numpy as jnp
from jax import lax
from jax.experimental import pallas as pl
from jax.experimental.pallas import tpu as pltpu

D_MODEL = 1024
N_HEADS = 16
N_KV_HEADS = 4
HEAD_DIM = 64
GROUP = N_HEADS // N_KV_HEADS
Q_DIM = N_HEADS * HEAD_DIM
KV_DIM = N_KV_HEADS * HEAD_DIM
QKV_DIM = Q_DIM + 2 * KV_DIM
ROPE_THETA = 10000.0
WINDOW = 128
MOBA_BLOCK = 256
MOBA_TOPK = 3
PAGE_SIZE = 128
N_GROUPS = 8
EXPERTS_PER_GROUP = 8
N_EXPERTS = N_GROUPS * EXPERTS_PER_GROUP
EXPERT_TOPK = 2
EXPERT_FF = D_MODEL // 4
RMS_EPS = 1e-6
NEG_INF = -1e30
BELOW_NEG_INF = -3e38
LOG2_E = 1.4426950408889634
VT_ROWS = HEAD_DIM + 16

LANES = 128
GROUP_LANES = GROUP * HEAD_DIM
ROW_TILE = 256
EXPERT_ROWS = 256
PAGES_PER_STEP = 16
VMEM_LIMIT = 56 * 1024 * 1024

F32 = jnp.float32
BF16 = jnp.bfloat16
NT_DIMS = (((1,), (1,)), ((), ()))


def _params(*sem):
    return pltpu.CompilerParams(dimension_semantics=sem, vmem_limit_bytes=VMEM_LIMIT)


def _rms(x, g):
    return x * lax.rsqrt(jnp.mean(x * x, axis=-1, keepdims=True) + RMS_EPS) * g


TOKEN_TILE = D_MODEL // LANES


def _load_token_tiles(ref, rows):
    return jnp.concatenate([ref[pl.ds(c, rows, stride=TOKEN_TILE), :] for c in range(TOKEN_TILE)], axis=1)


def _store_token_tiles(ref, x):
    for c in range(TOKEN_TILE):
        ref[pl.ds(c, x.shape[0], stride=TOKEN_TILE), :] = x[:, c * LANES:(c + 1) * LANES]


def _replicate_head(x128, odd):
    swapped = pltpu.roll(x128, HEAD_DIM, axis=1)
    low = lax.broadcasted_iota(jnp.int32, x128.shape, 1) < HEAD_DIM
    return jnp.where(low != odd, x128, swapped)


def _replicate_kv(x256, kvh):
    col = x256[:, (kvh // 2) * LANES:(kvh // 2 + 1) * LANES]
    rep = _replicate_head(col, kvh % 2 == 1)
    return jnp.concatenate([rep, rep], axis=1)


def _expand_heads(q):
    head = lax.shift_right_logical(lax.broadcasted_iota(jnp.int32, q.shape, 1), 6)
    return jnp.concatenate([jnp.where(head == h, q, 0.0) for h in range(GROUP)], axis=0)


def _collapse_heads(o_rep, rows):
    head = lax.shift_right_logical(lax.broadcasted_iota(jnp.int32, (rows, GROUP_LANES), 1), 6)
    out = jnp.zeros((rows, GROUP_LANES), F32)
    for h in range(GROUP):
        out = jnp.where(head == h, o_rep[h * rows:(h + 1) * rows], out)
    return out


def _top3_mask(gate, n_valid):
    lane = lax.broadcasted_iota(jnp.int32, gate.shape, 1)
    lane_f = lane.astype(F32)
    valid = lane < n_valid
    g = jnp.where(valid, gate, NEG_INF)
    sel = jnp.zeros(gate.shape, jnp.bool_)
    for _ in range(MOBA_TOPK):
        mx = jnp.max(g, axis=1, keepdims=True)
        first = jnp.min(jnp.where(g == mx, lane_f, 1e9), axis=1, keepdims=True)
        hit = lane_f == first
        sel = jnp.logical_or(sel, jnp.logical_and(hit, valid))
        g = jnp.where(hit, BELOW_NEG_INF, g)
    return sel


def _qkv_kernel(n_add, *refs):
    refs = list(refs)
    x_ref = refs.pop(0)
    add_refs = [refs.pop(0) for _ in range(n_add)]
    g_ref, w_ref, b_ref, cos_ref, sin_ref = refs[:5]
    outs = refs[5:]
    x = x_ref[...]
    for r in add_refs:
        x = x + _load_token_tiles(r, ROW_TILE)
    if n_add:
        outs.pop(0)[...] = x
    q_ref, k_ref, v_ref, krep_ref = outs[:4]
    h = _rms(x, g_ref[...]).astype(BF16)
    y = jnp.dot(h, w_ref[...], preferred_element_type=F32) + b_ref[...]
    cos = cos_ref[...]
    sin = sin_ref[...]
    low = lax.bitwise_and(lax.broadcasted_iota(jnp.int32, cos.shape, 1), HEAD_DIM - 1) < HEAD_DIM // 2

    def rope(slab):
        partner = jnp.where(low, pltpu.roll(slab, LANES - HEAD_DIM // 2, axis=1),
                            pltpu.roll(slab, HEAD_DIM // 2, axis=1))
        return slab * cos + partner * sin

    scale = HEAD_DIM ** -0.5 * LOG2_E
    for c in range(Q_DIM // LANES):
        q_ref[:, c * LANES:(c + 1) * LANES] = rope(y[:, c * LANES:(c + 1) * LANES]) * scale
    k = jnp.concatenate([rope(y[:, Q_DIM + c * LANES:Q_DIM + (c + 1) * LANES])
                         for c in range(KV_DIM // LANES)], axis=1)
    v = y[:, Q_DIM + KV_DIM:]
    k_ref[...] = k
    v_ref[...] = v
    for kvh in range(N_KV_HEADS):
        krep_ref[kvh] = _replicate_kv(k, kvh).astype(BF16)
    kmean_ref, qt_ref, vt_ref = outs[4:]
    kmean_ref[0] = jnp.broadcast_to(jnp.mean(k, axis=0, keepdims=True), (8, KV_DIM))
    qt_ref[...] = q_ref[...].T
    vt = v.T.reshape(N_KV_HEADS, HEAD_DIM, ROW_TILE).astype(BF16)
    vt_ref[:, 0] = jnp.concatenate([vt, jnp.ones((N_KV_HEADS, VT_ROWS - HEAD_DIM, ROW_TILE), BF16)], axis=1)


def _qkv_rope(x, adds, g, w, b, cos_t, sin_t):
    t = x.shape[0]
    nt = t // ROW_TILE
    row = lambda i: (i, 0)
    fixed = lambda i: (0, 0)
    in_specs = [pl.BlockSpec((ROW_TILE, D_MODEL), row)]
    args = [x]
    for arr, off in adds:
        in_specs.append(pl.BlockSpec((ROW_TILE * TOKEN_TILE, LANES),
                                     functools.partial(lambda o, i: (i + o, 0), off // ROW_TILE)))
        args.append(arr)
    in_specs += [pl.BlockSpec((1, D_MODEL), fixed), pl.BlockSpec((D_MODEL, QKV_DIM), fixed),
                 pl.BlockSpec((1, QKV_DIM), fixed), pl.BlockSpec((ROW_TILE, LANES), row),
                 pl.BlockSpec((ROW_TILE, LANES), row)]
    args += [g.reshape(1, D_MODEL), w.astype(BF16), b.reshape(1, QKV_DIM), cos_t, sin_t]
    out_shape, out_specs = [], []
    if adds:
        out_shape.append(jax.ShapeDtypeStruct((t, D_MODEL), F32))
        out_specs.append(pl.BlockSpec((ROW_TILE, D_MODEL), row))
    rep_shape = jax.ShapeDtypeStruct((N_KV_HEADS, t, GROUP_LANES), BF16)
    rep_spec = pl.BlockSpec((N_KV_HEADS, ROW_TILE, GROUP_LANES), lambda i: (0, i, 0))
    out_shape += [jax.ShapeDtypeStruct((t, Q_DIM), F32), jax.ShapeDtypeStruct((t, KV_DIM), F32),
                  jax.ShapeDtypeStruct((t, KV_DIM), F32), rep_shape]
    out_specs += [pl.BlockSpec((ROW_TILE, Q_DIM), row), pl.BlockSpec((ROW_TILE, KV_DIM), row),
                  pl.BlockSpec((ROW_TILE, KV_DIM), row), rep_spec]
    out_shape += [jax.ShapeDtypeStruct((nt, 8, KV_DIM), F32), jax.ShapeDtypeStruct((Q_DIM, t), F32),
                  jax.ShapeDtypeStruct((N_KV_HEADS, nt, VT_ROWS, ROW_TILE), BF16)]
    out_specs += [pl.BlockSpec((1, 8, KV_DIM), lambda i: (i, 0, 0)),
                  pl.BlockSpec((Q_DIM, ROW_TILE), lambda i: (0, i)),
                  pl.BlockSpec((N_KV_HEADS, 1, VT_ROWS, ROW_TILE), lambda i: (0, i, 0, 0))]
    return pl.pallas_call(
        functools.partial(_qkv_kernel, len(adds)),
        grid=(nt,), in_specs=in_specs, out_specs=out_specs, out_shape=out_shape,
        compiler_params=_params("arbitrary"), name="qkv_rope")(*args)


def _sink_softmax_pv(s, ok, sink_col, v):
    s = jnp.where(ok, s, NEG_INF)
    m = jnp.maximum(jnp.max(s, axis=1, keepdims=True), sink_col)
    p = jnp.exp2(s - m)
    denom = jnp.sum(p, axis=1, keepdims=True) + jnp.exp2(sink_col - m)
    return jnp.dot(p.astype(BF16), v, preferred_element_type=F32) / denom


def _sink_column(sinks_ref, kvh, rows):
    blk = lax.shift_right_logical(lax.broadcasted_iota(jnp.int32, (GROUP * rows, 1), 0), rows.bit_length() - 1)
    col = jnp.zeros((GROUP * rows, 1), F32)
    for h in range(GROUP):
        col = jnp.where(blk == h, sinks_ref[kvh * GROUP + h] * LOG2_E, col)
    return col


def _swa_prompt_kernel(sinks_ref, qt_ref, kprev_ref, kcur_ref, vprev_ref, vcur_ref, o_ref):
    i = pl.program_id(1)
    tile = ROW_TILE
    n_keys = WINDOW + tile
    key = lax.broadcasted_iota(jnp.int32, (n_keys, tile), 0)
    qry = lax.broadcasted_iota(jnp.int32, (n_keys, tile), 1)
    first_key = jnp.where(i > 0, 0, WINDOW)
    ok = (key > qry) & (key <= qry + WINDOW) & (key >= first_key)
    slabs = []
    for kvh in range(N_KV_HEADS):
        k = jnp.concatenate([kprev_ref[kvh, :, :2 * HEAD_DIM], kcur_ref[kvh, :, :2 * HEAD_DIM]], axis=0)
        vt = jnp.concatenate([vprev_ref[kvh, 0], vcur_ref[kvh, 0]], axis=1)
        outs = []
        for g in range(GROUP):
            h = kvh * GROUP + g
            qh = jnp.concatenate([qt_ref[h * HEAD_DIM:(h + 1) * HEAD_DIM, :].astype(BF16),
                                  jnp.zeros((HEAD_DIM, tile), BF16)], axis=0)
            s = jnp.where(ok, jnp.dot(k, qh, preferred_element_type=F32), NEG_INF)
            sink = sinks_ref[h] * LOG2_E
            m = jnp.maximum(jnp.max(s, axis=0, keepdims=True), sink)
            pv = jnp.dot(vt, jnp.exp2(s - m).astype(BF16), preferred_element_type=F32)
            denom = pv[HEAD_DIM:HEAD_DIM + 1] + jnp.exp2(sink - m)
            outs.append(pv[:HEAD_DIM] / denom)
        slabs.append(jnp.concatenate(outs, axis=0).T.astype(BF16))
    o_ref[...] = jnp.concatenate(slabs, axis=1)


def _swa_prompt(qt, k_rep, vt, sinks, batch, seq):
    nt = seq // ROW_TILE
    per_tile = ROW_TILE // WINDOW
    return pl.pallas_call(
        _swa_prompt_kernel,
        grid=(batch, nt),
        in_specs=[pl.BlockSpec(memory_space=pltpu.SMEM),
                  pl.BlockSpec((Q_DIM, ROW_TILE), lambda b, i: (0, b * nt + i)),
                  pl.BlockSpec((N_KV_HEADS, WINDOW, GROUP_LANES),
                               lambda b, i: (0, jnp.maximum((b * nt + i) * per_tile - 1, 0), 0)),
                  pl.BlockSpec((N_KV_HEADS, ROW_TILE, GROUP_LANES), lambda b, i: (0, b * nt + i, 0)),
                  pl.BlockSpec((N_KV_HEADS, 1, VT_ROWS, WINDOW),
                               lambda b, i: (0, jnp.maximum(b * nt + i - 1, 0), 0, per_tile - 1)),
                  pl.BlockSpec((N_KV_HEADS, 1, VT_ROWS, ROW_TILE), lambda b, i: (0, b * nt + i, 0, 0))],
        out_specs=pl.BlockSpec((ROW_TILE, Q_DIM), lambda b, i: (b * nt + i, 0)),
        out_shape=jax.ShapeDtypeStruct((batch * seq, Q_DIM), BF16),
        compiler_params=_params("arbitrary", "arbitrary"), name="swa_prompt",
    )(sinks, qt, k_rep, k_rep, vt, vt)


def _swa_sample_kernel(seqs, nq, sinks_ref, q_ref, kn_ref, vn_ref, ck_ref, cv_ref, o_ref, ko_ref, vo_ref):
    win = ck_ref.shape[1]

    def one_seq(s, carry):
        r0 = pl.multiple_of(s * nq, nq)
        q = q_ref[pl.ds(r0, nq), :]
        kn = kn_ref[pl.ds(r0, nq), :]
        vn = vn_ref[pl.ds(r0, nq), :]
        ck = ck_ref[s]
        cv = cv_ref[s]
        ko_ref[s] = jnp.concatenate([ck[nq:], kn], axis=0)
        vo_ref[s] = jnp.concatenate([cv[nq:], vn], axis=0)
        kk = jnp.concatenate([ck, kn], axis=0)
        vv = jnp.concatenate([cv, vn], axis=0)
        outs = []
        for kvh in range(N_KV_HEADS):
            qexp = _expand_heads(q[:, kvh * GROUP_LANES:(kvh + 1) * GROUP_LANES]).astype(BF16)
            k = _replicate_kv(kk, kvh).astype(BF16)
            v = _replicate_kv(vv, kvh).astype(BF16)
            sc = lax.dot_general(qexp, k, NT_DIMS, preferred_element_type=F32)
            qi = lax.bitwise_and(lax.broadcasted_iota(jnp.int32, sc.shape, 0), nq - 1)
            kj = lax.broadcasted_iota(jnp.int32, sc.shape, 1)
            ok = ((kj < win) & (kj > qi + win - WINDOW)) | ((kj >= win) & (kj - win <= qi))
            o_rep = _sink_softmax_pv(sc, ok, _sink_column(sinks_ref, kvh, nq), v)
            outs.append(_collapse_heads(o_rep, nq))
        o_ref[pl.ds(r0, nq), :] = jnp.concatenate(outs, axis=1).astype(BF16)
        return carry

    lax.fori_loop(0, seqs, one_seq, 0)


def _swa_sample(q, k, v, cache_k, cache_v, sinks, row0, nq):
    db, win = cache_k.shape[0], cache_k.shape[1]
    seqs = 8
    rows = seqs * nq
    tok = lambda g: (row0 // rows + g, 0)
    cache = lambda g: (g, 0, 0)
    return pl.pallas_call(
        functools.partial(_swa_sample_kernel, seqs, nq),
        grid=(db // seqs,),
        in_specs=[pl.BlockSpec(memory_space=pltpu.SMEM),
                  pl.BlockSpec((rows, Q_DIM), tok), pl.BlockSpec((rows, KV_DIM), tok),
                  pl.BlockSpec((rows, KV_DIM), tok),
                  pl.BlockSpec((seqs, win, KV_DIM), cache), pl.BlockSpec((seqs, win, KV_DIM), cache)],
        out_specs=[pl.BlockSpec((rows, Q_DIM), lambda g: (g, 0)),
                   pl.BlockSpec((seqs, win, KV_DIM), cache), pl.BlockSpec((seqs, win, KV_DIM), cache)],
        out_shape=[jax.ShapeDtypeStruct((db * nq, Q_DIM), BF16),
                   jax.ShapeDtypeStruct((db, win, KV_DIM), F32), jax.ShapeDtypeStruct((db, win, KV_DIM), F32)],
        compiler_params=_params("arbitrary"), name="swa_sample",
    )(sinks, q, k, v, cache_k.reshape(db, win, KV_DIM), cache_v.reshape(db, win, KV_DIM))


def _top3_rows(gate, n_valid):
    row = lax.broadcasted_iota(jnp.int32, gate.shape, 0)
    row_f = row.astype(F32)
    valid = row < n_valid
    g = jnp.where(valid, gate, NEG_INF)
    sel = jnp.zeros(gate.shape, F32)
    for _ in range(MOBA_TOPK):
        mx = jnp.max(g, axis=0, keepdims=True)
        first = jnp.min(jnp.where(g == mx, row_f, 1e9), axis=0, keepdims=True)
        hit = row_f == first
        sel = jnp.where(jnp.logical_and(hit, valid), 1.0, sel)
        g = jnp.where(hit, BELOW_NEG_INF, g)
    return sel


def _moba_prompt_kernel(qt_ref, k_ref, vt_ref, kmean_ref, o_ref, sel_ref, acc_ref):
    i = pl.program_id(2)
    qt = qt_ref[...]
    row_head = lax.shift_right_logical(lax.broadcasted_iota(jnp.int32, qt.shape, 0), 6)
    qexp32 = jnp.concatenate([jnp.where(row_head == h, qt, 0.0) for h in range(GROUP)], axis=1)
    qexp = qexp32.astype(BF16)
    gate = jnp.dot(kmean_ref[0, 0], qexp32, preferred_element_type=F32, precision=lax.Precision.HIGHEST)
    sel_ref[...] = _top3_rows(gate, i)

    q_heads = [jnp.concatenate([qexp[h * HEAD_DIM:(h + 1) * HEAD_DIM, h * MOBA_BLOCK:(h + 1) * MOBA_BLOCK],
                                jnp.zeros((HEAD_DIM, MOBA_BLOCK), BF16)], axis=0) for h in range(GROUP)]
    heads = [slice(h * MOBA_BLOCK, (h + 1) * MOBA_BLOCK) for h in range(GROUP)]

    def head_scores(j):
        start = pl.multiple_of(j * MOBA_BLOCK, MOBA_BLOCK)
        kj = k_ref[0, pl.ds(start, MOBA_BLOCK), :2 * HEAD_DIM]
        return [jnp.dot(kj, q_heads[h], preferred_element_type=F32) for h in range(GROUP)]

    m = []
    for h, s in enumerate(head_scores(i)):
        key = lax.broadcasted_iota(jnp.int32, s.shape, 0)
        qry = lax.broadcasted_iota(jnp.int32, s.shape, 1)
        s = jnp.where(key <= qry, s, NEG_INF)
        m.append(jnp.max(s, axis=0, keepdims=True))
        acc_ref[:, heads[h]] = jnp.dot(vt_ref[0, i], jnp.exp2(s - m[h]).astype(BF16), preferred_element_type=F32)

    def past_block(j, m_old):
        vtj = vt_ref[0, j]
        m_out = []
        for h, s in enumerate(head_scores(j)):
            chosen = sel_ref[pl.ds(j, 1), heads[h]] > 0.0
            m_new = jnp.where(chosen, jnp.maximum(m_old[h], jnp.max(s, axis=0, keepdims=True)), m_old[h])
            alpha = jnp.exp2(m_old[h] - m_new)
            p = jnp.exp2(s - jnp.where(chosen, m_new, -NEG_INF))
            acc_ref[:, heads[h]] = alpha * acc_ref[:, heads[h]] + jnp.dot(vtj, p.astype(BF16),
                                                                         preferred_element_type=F32)
            m_out.append(m_new)
        return tuple(m_out)

    def past_pair(t, m_old):
        return past_block(2 * t + 1, past_block(2 * t, m_old))

    lax.fori_loop(0, lax.shift_right_logical(i + 1, 1), past_pair, tuple(m))
    ot = acc_ref[:HEAD_DIM] / acc_ref[HEAD_DIM:HEAD_DIM + 1]
    ot = jnp.concatenate([ot[:, h * MOBA_BLOCK:(h + 1) * MOBA_BLOCK] for h in range(GROUP)], axis=0)
    o_ref[...] = ot.T.astype(BF16)


def _moba_prompt(qt, k_rep, vt, kmean_rep, batch, seq):
    nb = seq // MOBA_BLOCK
    cols = GROUP * MOBA_BLOCK
    return pl.pallas_call(
        _moba_prompt_kernel,
        grid=(batch, N_KV_HEADS, nb),
        in_specs=[pl.BlockSpec((GROUP_LANES, MOBA_BLOCK), lambda b, h, i: (h, b * nb + i)),
                  pl.BlockSpec((1, seq, GROUP_LANES), lambda b, h, i: (h, b, 0)),
                  pl.BlockSpec((1, nb, VT_ROWS, MOBA_BLOCK), lambda b, h, i: (h, b, 0, 0)),
                  pl.BlockSpec((1, 1, nb, GROUP_LANES), lambda b, h, i: (b, h, 0, 0))],
        out_specs=pl.BlockSpec((MOBA_BLOCK, GROUP_LANES), lambda b, h, i: (b * nb + i, h)),
        out_shape=jax.ShapeDtypeStruct((batch * seq, Q_DIM), BF16),
        scratch_shapes=[pltpu.VMEM((nb, cols), F32), pltpu.VMEM((VT_ROWS, cols), F32)],
        compiler_params=_params("arbitrary", "arbitrary", "arbitrary"), name="moba_prompt",
    )(qt, k_rep, vt, kmean_rep)


def _moba_sample_kernel(nq, n_chunks, n_seq, pt_ref, qexp_ref, kn_ref, vn_ref, expand_ref, pk_hbm, pv_hbm,
                        o_ref, cbuf, s_ref, sem):
    b = pl.program_id(0)
    pages = PAGES_PER_STEP
    keys = pages * PAGE_SIZE
    blocks_per_chunk = keys // MOBA_BLOCK
    n_blocks = n_chunks * blocks_per_chunk
    n_stream = 2 * n_chunks

    def chunk_copies(seq, j, lookup):
        pool = pk_hbm if j < n_chunks else pv_hbm
        first = (j % n_chunks) * pages
        return [pltpu.make_async_copy(pool.at[pt_ref[seq, first + p] if lookup else 0],
                                      cbuf.at[j % 2, p], sem.at[j % 2])
                for p in range(pages)]

    def chunk(j):
        return jnp.concatenate([cbuf[j % 2, p] for p in range(pages)], axis=1)

    def start(seq, j):
        for p, cp in enumerate(chunk_copies(seq, j, True)):
            cp.start(priority=p % 2)

    @pl.when(b == 0)
    def _():
        start(b, 0)

    qexp32 = qexp_ref[0]
    qexp = qexp32.astype(BF16)
    blk_lane = lax.broadcasted_iota(jnp.int32, (KV_DIM, LANES), 1)
    ksum = jnp.zeros((KV_DIM, LANES), F32)
    for j in range(n_stream):
        if j + 1 < n_stream:
            start(b, j + 1)
        else:
            @pl.when(b + 1 < n_seq)
            def _():
                start(b + 1, 0)
        for cp in chunk_copies(b, j, False):
            cp.wait()
        cols = slice((j % n_chunks) * keys, (j % n_chunks + 1) * keys)
        if j < n_chunks:
            kt = chunk(j)
            s_ref[:, cols] = jnp.dot(qexp, kt.astype(BF16), preferred_element_type=F32)
            for n in range(blocks_per_chunk):
                blk_sum = jnp.sum(kt[:, n * MOBA_BLOCK:(n + 1) * MOBA_BLOCK], axis=1, keepdims=True)
                ksum = jnp.where(blk_lane == j * blocks_per_chunk + n, blk_sum, ksum)
        if j == n_chunks - 1:
            gate = jnp.dot(qexp32, ksum * (1.0 / MOBA_BLOCK), preferred_element_type=F32,
                           precision=lax.Precision.HIGHEST)
            sel_bias = jnp.where(_top3_mask(gate, n_blocks), 0.0, NEG_INF).astype(BF16)
            s_own = lax.dot_general(qexp, kn_ref[...].astype(BF16), NT_DIMS, preferred_element_type=F32)
            qi = lax.bitwise_and(lax.broadcasted_iota(jnp.int32, s_own.shape, 0), nq - 1)
            kj = lax.broadcasted_iota(jnp.int32, s_own.shape, 1)
            s_own = jnp.where(kj <= qi, s_own, NEG_INF)
            m_tile = jnp.full((N_HEADS * nq, MOBA_BLOCK), NEG_INF, F32)
            for c in range(n_chunks):
                cc = slice(c * keys, (c + 1) * keys)
                sm = s_ref[:, cc] + jnp.dot(sel_bias, expand_ref[:, cc], preferred_element_type=F32)
                s_ref[:, cc] = sm
                for n in range(blocks_per_chunk):
                    m_tile = jnp.maximum(m_tile, sm[:, n * MOBA_BLOCK:(n + 1) * MOBA_BLOCK])
            m = jnp.maximum(jnp.max(m_tile, axis=1, keepdims=True), jnp.max(s_own, axis=1, keepdims=True))
            p_own = jnp.exp2(s_own - m)
            l_own = jnp.sum(p_own, axis=1, keepdims=True)
            acc = jnp.dot(p_own.astype(BF16), vn_ref[...].astype(BF16), preferred_element_type=F32)
            l_tile = jnp.zeros((N_HEADS * nq, MOBA_BLOCK), F32)
        if j >= n_chunks:
            p = jnp.exp2(s_ref[:, cols] - m)
            for n in range(blocks_per_chunk):
                l_tile = l_tile + p[:, n * MOBA_BLOCK:(n + 1) * MOBA_BLOCK]
            acc = acc + lax.dot_general(p.astype(BF16), chunk(j).astype(BF16), NT_DIMS,
                                        preferred_element_type=F32)
    o_ref[0] = acc / (l_own + jnp.sum(l_tile, axis=1, keepdims=True))


def _moba_sample(qexp, k, v, pool_k, pool_v, page_table, row0, nq):
    db, n_pages = page_table.shape
    n_pool = pool_k.shape[0]
    pages = PAGES_PER_STEP
    n_chunks = n_pages // pages
    rows = N_HEADS * nq
    pk = pool_k.transpose(0, 2, 3, 1).reshape(n_pool, KV_DIM, PAGE_SIZE)
    pv = pool_v.transpose(0, 2, 3, 1).reshape(n_pool, KV_DIM, PAGE_SIZE)
    n_keys = n_pages * PAGE_SIZE
    expand = (jnp.arange(LANES)[:, None] == (jnp.arange(n_keys) // MOBA_BLOCK)[None, :]).astype(BF16)
    new = lambda b, pt: (row0 // nq + b, 0)
    grid_spec = pltpu.PrefetchScalarGridSpec(
        num_scalar_prefetch=1,
        grid=(db,),
        in_specs=[pl.BlockSpec((1, rows, KV_DIM), lambda b, pt: (b, 0, 0)),
                  pl.BlockSpec((nq, KV_DIM), new), pl.BlockSpec((nq, KV_DIM), new),
                  pl.BlockSpec((LANES, n_keys), lambda b, pt: (0, 0)),
                  pl.BlockSpec(memory_space=pl.ANY), pl.BlockSpec(memory_space=pl.ANY)],
        out_specs=pl.BlockSpec((1, rows, KV_DIM), lambda b, pt: (b, 0, 0)),
        scratch_shapes=[pltpu.VMEM((2, pages, KV_DIM, PAGE_SIZE), F32),
                        pltpu.VMEM((rows, n_keys), F32),
                        pltpu.SemaphoreType.DMA((2,))])
    return pl.pallas_call(
        functools.partial(_moba_sample_kernel, nq, n_chunks, db),
        grid_spec=grid_spec,
        out_shape=jax.ShapeDtypeStruct((db, rows, KV_DIM), F32),
        compiler_params=_params("arbitrary"), name="moba_sample",
    )(page_table, qexp, k, v, expand, pk, pv)


def _proj_route_kernel(o_ref, x_ref, wo_ref, bo_ref, g_ref, wr_ref, br_ref, x1_ref, h_ref, route_ref):
    x1 = x_ref[...] + jnp.dot(o_ref[...], wo_ref[...], preferred_element_type=F32) + bo_ref[...]
    x1_ref[...] = x1
    h = _rms(x1, g_ref[...])
    _store_token_tiles(h_ref, h)
    lg = jnp.dot(h, wr_ref[...], preferred_element_type=F32, precision=lax.Precision.HIGHEST) + br_ref[...]
    lane = lax.broadcasted_iota(jnp.int32, lg.shape, 1)
    lane_f = lane.astype(F32)
    is_group = lane < N_GROUPS
    gl = jnp.where(is_group, lg, NEG_INF)
    gmax = jnp.max(gl, axis=1, keepdims=True)
    grp = jnp.min(jnp.where(gl == gmax, lane_f, 1e9), axis=1, keepdims=True)
    g_w = 1.0 / jnp.sum(jnp.where(is_group, jnp.exp(gl - gmax), 0.0), axis=1, keepdims=True)
    lane_grp = lax.shift_right_logical(lane - N_GROUPS, 3).astype(F32)
    in_grp = (lane >= N_GROUPS) & (lane < N_GROUPS + N_EXPERTS) & (lane_grp == grp)
    el = jnp.where(in_grp, lg, NEG_INF)
    v1 = jnp.max(el, axis=1, keepdims=True)
    i1 = jnp.min(jnp.where(el == v1, lane_f, 1e9), axis=1, keepdims=True)
    el2 = jnp.where(lane_f == i1, BELOW_NEG_INF, el)
    v2 = jnp.max(el2, axis=1, keepdims=True)
    i2 = jnp.min(jnp.where(el2 == v2, lane_f, 1e9), axis=1, keepdims=True)
    e2w = jnp.exp(v2 - v1)
    w1 = g_w / (1.0 + e2w)
    w2 = g_w * e2w / (1.0 + e2w)
    route = jnp.where(lane == 0, i1 - N_GROUPS, 0.0)
    route = jnp.where(lane == 1, i2 - N_GROUPS, route)
    route = jnp.where(lane == 2, w1, route)
    route = jnp.where(lane == 3, w2, route)
    route_ref[...] = route


def _proj_route(o, x, wo, bo, g, w_gr, b_gr, w_er, b_er):
    t = x.shape[0]
    row = lambda i: (i, 0)
    fixed = lambda i: (0, 0)
    pad = LANES - N_GROUPS - N_EXPERTS
    wr = jnp.concatenate([w_gr, w_er, jnp.zeros((D_MODEL, pad), F32)], axis=1)
    br = jnp.concatenate([b_gr, b_er, jnp.zeros((pad,), F32)]).reshape(1, LANES)
    return pl.pallas_call(
        _proj_route_kernel,
        grid=(t // ROW_TILE,),
        in_specs=[pl.BlockSpec((ROW_TILE, Q_DIM), row), pl.BlockSpec((ROW_TILE, D_MODEL), row),
                  pl.BlockSpec((Q_DIM, D_MODEL), fixed), pl.BlockSpec((1, D_MODEL), fixed),
                  pl.BlockSpec((1, D_MODEL), fixed), pl.BlockSpec((D_MODEL, LANES), fixed),
                  pl.BlockSpec((1, LANES), fixed)],
        out_specs=[pl.BlockSpec((ROW_TILE, D_MODEL), row), pl.BlockSpec((ROW_TILE * TOKEN_TILE, LANES), row),
                   pl.BlockSpec((ROW_TILE, LANES), row)],
        out_shape=[jax.ShapeDtypeStruct((t, D_MODEL), F32), jax.ShapeDtypeStruct((t * TOKEN_TILE, LANES), F32),
                   jax.ShapeDtypeStruct((t, LANES), F32)],
        compiler_params=_params("arbitrary"), name="proj_route",
    )(o, x, wo.astype(BF16), bo.reshape(1, D_MODEL), g.reshape(1, D_MODEL), wr, br)


def _expert_kernel(n_steps, blk_e_ref, nvalid_ref, src_ref, src_next_ref, dst_ref, wrow_ref, h_hbm,
                   wg_ref, wu_ref, wd_ref, y_hbm, xbuf, ybuf, wg_bf, wu_bf, wd_bf, gsem, ssem):
    i = pl.program_id(0)
    slot = lax.rem(i, 2)
    rows = EXPERT_ROWS
    nvalid = nvalid_ref[i]
    nvalid_next = jnp.where(i + 1 < n_steps, nvalid_ref[jnp.minimum(i + 1, n_steps - 1)], 0)
    nvalid_prev2 = jnp.where(i >= 2, nvalid_ref[jnp.maximum(i - 2, 0)], 0)

    def tile(ref, first):
        return ref.at[pl.ds(pl.multiple_of(first, TOKEN_TILE), TOKEN_TILE)]

    def row_copy_in(idx_ref, r, s):
        return pltpu.make_async_copy(tile(h_hbm, idx_ref[0, r]), tile(xbuf.at[s], r * TOKEN_TILE), gsem.at[s])

    def row_copy_out(r, s):
        return pltpu.make_async_copy(tile(ybuf.at[s], r * TOKEN_TILE), tile(y_hbm, dst_ref[0, r]), ssem.at[s])

    def for_rows(fn, n=None):
        def body(g, carry):
            for u in range(8):
                r = g * 8 + u
                if n is None:
                    fn(r, u)
                else:
                    pl.when(r < n)(functools.partial(fn, r, u))
            return carry
        lax.fori_loop(0, rows // 8, body, 0)

    @pl.when((i == 0) & (nvalid > 0))
    def _():
        for_rows(lambda r, u: row_copy_in(src_ref, r, 0).start(priority=u % 2))

    @pl.when(nvalid_next > 0)
    def _():
        for_rows(lambda r, u: row_copy_in(src_next_ref, r, 1 - slot).start(priority=u % 2))

    @pl.when(nvalid_prev2 > 0)
    def _():
        for_rows(lambda r, u: row_copy_out(r, slot).wait(), nvalid_prev2)

    @pl.when(nvalid > 0)
    def _():
        e_changed = jnp.logical_or(i == 0, blk_e_ref[i] != blk_e_ref[jnp.maximum(i - 1, 0)])

        @pl.when(e_changed)
        def _():
            wg_bf[...] = wg_ref[0, 0].astype(BF16)
            wu_bf[...] = wu_ref[0, 0].astype(BF16)
            wd_bf[...] = wd_ref[0, 0].astype(BF16)

        for_rows(lambda r, u: row_copy_in(src_ref, r, slot).wait())
        x = _load_token_tiles(xbuf.at[slot], rows).astype(BF16)
        gate = jnp.dot(x, wg_bf[...], preferred_element_type=F32)
        up = jnp.dot(x, wu_bf[...], preferred_element_type=F32)
        hid = (gate * jax.nn.sigmoid(gate) * up).astype(BF16)
        y = jnp.dot(hid, wd_bf[...], preferred_element_type=F32)
        _store_token_tiles(ybuf.at[slot], y * wrow_ref[...])
        for_rows(lambda r, u: row_copy_out(r, slot).start(priority=u % 2), nvalid)


def _expert_ffn(h, e_idx, e_w, w_gate, w_up, w_down, layer):
    t = h.shape[0] // TOKEN_TILE
    a = t * EXPERT_TOPK
    rows = EXPERT_ROWS
    n_blk = -(-a // rows) + N_EXPERTS
    n_steps = n_blk + 2
    n_rows = n_steps * rows
    flat_e = e_idx.reshape(a)
    order = jnp.argsort(flat_e).astype(jnp.int32)
    counts = jnp.sum(flat_e[:, None] == jnp.arange(N_EXPERTS, dtype=jnp.int32)[None, :], axis=0, dtype=jnp.int32)
    padded = (counts + rows - 1) // rows * rows
    pad_end = jnp.cumsum(padded)
    pad_start = pad_end - padded
    start = jnp.cumsum(counts) - counts
    blk_first = jnp.arange(n_steps, dtype=jnp.int32) * rows
    blk_e = jnp.minimum(jnp.sum(blk_first[:, None] >= pad_end[None, :], axis=1), N_EXPERTS - 1).astype(jnp.int32)
    rank0 = blk_first - pad_start[blk_e]
    nvalid = jnp.clip(counts[blk_e] - rank0, 0, rows).astype(jnp.int32)
    within = jnp.arange(rows, dtype=jnp.int32)[None, :]
    valid = within < nvalid[:, None]
    assign = order[jnp.clip((start[blk_e] + rank0)[:, None] + within, 0, a - 1)]
    tok = assign // EXPERT_TOPK
    src = jnp.where(valid, tok, 0) * TOKEN_TILE
    dst = jnp.where(valid, (assign % EXPERT_TOPK) * t + tok, 0) * TOKEN_TILE
    wrow = jnp.where(valid, e_w.reshape(a)[assign], 0.0)
    idx_blk = lambda i, be, na: (i, 0, 0)
    idx_next = lambda i, be, na: (jnp.minimum(i + 1, n_steps - 1), 0, 0)
    w_in = lambda i, be, na: (layer, be[i], 0, 0)
    grid_spec = pltpu.PrefetchScalarGridSpec(
        num_scalar_prefetch=2,
        grid=(n_steps,),
        in_specs=[pl.BlockSpec((1, 1, rows), idx_blk, memory_space=pltpu.SMEM),
                  pl.BlockSpec((1, 1, rows), idx_next, memory_space=pltpu.SMEM),
                  pl.BlockSpec((1, 1, rows), idx_blk, memory_space=pltpu.SMEM),
                  pl.BlockSpec((rows, 1), lambda i, be, na: (i, 0)),
                  pl.BlockSpec(memory_space=pl.ANY),
                  pl.BlockSpec((1, 1, D_MODEL, EXPERT_FF), w_in), pl.BlockSpec((1, 1, D_MODEL, EXPERT_FF), w_in),
                  pl.BlockSpec((1, 1, EXPERT_FF, D_MODEL), w_in)],
        out_specs=pl.BlockSpec(memory_space=pl.ANY),
        scratch_shapes=[pltpu.VMEM((2, rows * TOKEN_TILE, LANES), F32), pltpu.VMEM((2, rows * TOKEN_TILE, LANES), F32),
                        pltpu.VMEM((D_MODEL, EXPERT_FF), BF16), pltpu.VMEM((D_MODEL, EXPERT_FF), BF16),
                        pltpu.VMEM((EXPERT_FF, D_MODEL), BF16),
                        pltpu.SemaphoreType.DMA((2,)), pltpu.SemaphoreType.DMA((2,))])

    def body(blk_e_ref, nvalid_ref, src_ref, src_next_ref, dst_ref, *rest):
        _expert_kernel(n_steps, blk_e_ref, nvalid_ref, src_ref.at[0], src_next_ref.at[0], dst_ref.at[0], *rest)

    return pl.pallas_call(
        body, grid_spec=grid_spec,
        out_shape=jax.ShapeDtypeStruct((EXPERT_TOPK * t * TOKEN_TILE, LANES), F32),
        compiler_params=_params("arbitrary"), name="expert_ffn",
    )(blk_e, nvalid, src.reshape(n_steps, 1, rows), src.reshape(n_steps, 1, rows),
      dst.reshape(n_steps, 1, rows), wrow.reshape(n_rows, 1), h, w_gate, w_up, w_down)


def _final_kernel(x_ref, y0_ref, y1_ref, g_ref, o_ref):
    y = _load_token_tiles(y0_ref, ROW_TILE) + _load_token_tiles(y1_ref, ROW_TILE)
    o_ref[...] = _rms(x_ref[...] + y, g_ref[...])


def _final_norm(x, y2, g, row0, n_rows):
    t = x.shape[0]
    off = row0 // ROW_TILE
    return pl.pallas_call(
        _final_kernel,
        grid=(n_rows // ROW_TILE,),
        in_specs=[pl.BlockSpec((ROW_TILE, D_MODEL), lambda i: (i + off, 0)),
                  pl.BlockSpec((ROW_TILE * TOKEN_TILE, LANES), lambda i: (i + off, 0)),
                  pl.BlockSpec((ROW_TILE * TOKEN_TILE, LANES), lambda i: (i + off + t // ROW_TILE, 0)),
                  pl.BlockSpec((1, D_MODEL), lambda i: (0, 0))],
        out_specs=pl.BlockSpec((ROW_TILE, D_MODEL), lambda i: (i, 0)),
        out_shape=jax.ShapeDtypeStruct((n_rows, D_MODEL), F32),
        compiler_params=_params("arbitrary"), name="final_norm",
    )(x, y2, y2, g.reshape(1, D_MODEL))


def _rope_tables(pos):
    half = HEAD_DIM // 2
    inv = ROPE_THETA ** (-jnp.arange(half, dtype=F32) / half)
    ang = pos.astype(F32)[:, None] * inv[None, :]
    cos = jnp.cos(ang)
    sin = jnp.sin(ang)
    reps = LANES // HEAD_DIM
    return jnp.tile(jnp.concatenate([cos, cos], axis=1), (1, reps)), jnp.tile(jnp.concatenate([-sin, sin], axis=1), (1, reps))


def _moe(o, x, wo, bo, g, w_gr, b_gr, w_er, b_er, w_gate, w_up, w_down, layer):
    x1, h, route = _proj_route(o, x, wo, bo, g, w_gr, b_gr, w_er, b_er)
    e_idx = jnp.clip(route[:, :EXPERT_TOPK].astype(jnp.int32), 0, N_EXPERTS - 1)
    e_w = route[:, EXPERT_TOPK:2 * EXPERT_TOPK]
    return x1, _expert_ffn(h, e_idx, e_w, w_gate, w_up, w_down, layer)


def kernel(x_prompt, x_sample, cache_swa_k, cache_swa_v, cache_moba_k, cache_moba_v, page_table, norm_mix, norm_ffn, norm_final, swa_w_qkv, swa_b_qkv, swa_sinks, swa_w_o, swa_b_o, moba_w_qkv, moba_w_o, w_group_router, b_group_router, w_expert_router, b_expert_router, w_gate, w_up, w_down):
    batch, seq, _ = x_prompt.shape
    db, nq, _ = x_sample.shape
    past_len = page_table.shape[1] * PAGE_SIZE
    tp = batch * seq
    ts = db * nq
    t = tp + ts
    x = jnp.concatenate([x_prompt.reshape(tp, D_MODEL), x_sample.reshape(ts, D_MODEL)], axis=0)
    pos = jnp.concatenate([jnp.tile(jnp.arange(seq), batch), jnp.tile(past_len + jnp.arange(nq), db)])
    cos_t, sin_t = _rope_tables(pos)

    q, k0, v0, k_rep, _, qt, vt = _qkv_rope(x, [], norm_mix[0], swa_w_qkv[0], swa_b_qkv[0], cos_t, sin_t)
    o_p = _swa_prompt(qt, k_rep, vt, swa_sinks[0], batch, seq)
    o_s, swa_ks, swa_vs = _swa_sample(q, k0, v0, cache_swa_k[0], cache_swa_v[0], swa_sinks[0], tp, nq)
    o = jnp.concatenate([o_p, o_s], axis=0)
    x1, y2 = _moe(o, x, swa_w_o[0], swa_b_o[0], norm_ffn[0], w_group_router[0], b_group_router[0],
                  w_expert_router[0], b_expert_router[0], w_gate, w_up, w_down, 0)

    x, q, k1, v1, k_rep, kmean, qt, vt = _qkv_rope(
        x1, [(y2, 0), (y2, t)], norm_mix[1], moba_w_qkv[0], jnp.zeros((QKV_DIM,), F32), cos_t, sin_t)
    nb = seq // MOBA_BLOCK
    kmean_rep = jnp.tile(kmean[:batch * nb, 0].reshape(batch, nb, N_KV_HEADS, 1, HEAD_DIM), (1, 1, 1, GROUP, 1))
    kmean_rep = kmean_rep.reshape(batch, nb, N_KV_HEADS, GROUP_LANES).transpose(0, 2, 1, 3)
    o_p = _moba_prompt(qt, k_rep, vt, kmean_rep, batch, seq)
    qs = q[tp:].reshape(db, nq, N_HEADS, 1, HEAD_DIM)
    slot = (jnp.arange(N_HEADS) // GROUP)[:, None] == jnp.arange(N_KV_HEADS)[None, :]
    qexp = jnp.where(slot[None, None, :, :, None], qs, 0.0).transpose(0, 2, 1, 3, 4).reshape(db, N_HEADS * nq, KV_DIM)
    o_s = _moba_sample(qexp, k1, v1, cache_moba_k[0], cache_moba_v[0], page_table, tp, nq)
    o_s = o_s.reshape(db, N_HEADS, nq, N_KV_HEADS, HEAD_DIM)
    o_s = jnp.sum(jnp.where(slot[None, :, None, :, None], o_s, 0.0), axis=3).transpose(0, 2, 1, 3)
    o = jnp.concatenate([o_p, o_s.reshape(ts, Q_DIM).astype(BF16)], axis=0)
    x1, y2 = _moe(o, x, moba_w_o[0], jnp.zeros((D_MODEL,), F32), norm_ffn[1], w_group_router[1],
                  b_group_router[1], w_expert_router[1], b_expert_router[1], w_gate, w_up, w_down, 1)

    y_prompt = _final_norm(x1, y2, norm_final, 0, tp).reshape(batch, seq, D_MODEL)
    y_sample = _final_norm(x1, y2, norm_final, tp, ts).reshape(db, nq, D_MODEL)
    win = min(WINDOW, seq)
    kv5 = lambda a, b, s: a.reshape(1, b, s, N_KV_HEADS, HEAD_DIM)
    k0p = k0[:tp].reshape(batch, seq, KV_DIM)[:, seq - win:]
    v0p = v0[:tp].reshape(batch, seq, KV_DIM)[:, seq - win:]
    return (y_prompt, y_sample, kv5(k0p, batch, win), kv5(v0p, batch, win),
            kv5(swa_ks, db, swa_ks.shape[1]), kv5(swa_vs, db, swa_vs.shape[1]),
            kv5(k1[:tp], batch, seq), kv5(v1[:tp], batch, seq), kv5(k1[tp:], db, nq), kv5(v1[tp:], db, nq))
```

```python
import functools

import jax
import jax.numpy as jnp
from jax import lax
from jax.experimental import pallas as pl
from jax.experimental.pallas import tpu as pltpu

D_MODEL = 1024
N_HEADS = 16
N_KV_HEADS = 4
HEAD_DIM = 64
GROUP = N_HEADS // N_KV_HEADS
Q_DIM = N_HEADS * HEAD_DIM
KV_DIM = N_KV_HEADS * HEAD_DIM
QKV_DIM = Q_DIM + 2 * KV_DIM
ROPE_THETA = 10000.0
WINDOW = 128
MOBA_BLOCK = 256
MOBA_TOPK = 3
PAGE_SIZE = 128
N_GROUPS = 8
EXPERTS_PER_GROUP = 8
N_EXPERTS = N_GROUPS * EXPERTS_PER_GROUP
EXPERT_TOPK = 2
EXPERT_FF = D_MODEL // 4
RMS_EPS = 1e-6
NEG_INF = -1e30
BELOW_NEG_INF = -3e38
LOG2_E = 1.4426950408889634
VT_ROWS = HEAD_DIM + 16

LANES = 128
GROUP_LANES = GROUP * HEAD_DIM
ROW_TILE = 256
EXPERT_ROWS = 256
PAGES_PER_STEP = 16
MOBA_BLOCKS_PER_TRIP = 4
PAGE_SLOTS = 4
VMEM_LIMIT = 56 * 1024 * 1024

F32 = jnp.float32
BF16 = jnp.bfloat16
NT_DIMS = (((1,), (1,)), ((), ()))


def _params(*sem):
    return pltpu.CompilerParams(dimension_semantics=sem, vmem_limit_bytes=VMEM_LIMIT)


def _rms(x, g):
    return x * lax.rsqrt(jnp.mean(x * x, axis=-1, keepdims=True) + RMS_EPS) * g


TOKEN_TILE = D_MODEL // LANES


def _load_token_tiles(ref, rows):
    return jnp.concatenate([ref[pl.ds(c, rows, stride=TOKEN_TILE), :] for c in range(TOKEN_TILE)], axis=1)


def _store_token_tiles(ref, x):
    for c in range(TOKEN_TILE):
        ref[pl.ds(c, x.shape[0], stride=TOKEN_TILE), :] = x[:, c * LANES:(c + 1) * LANES]


def _replicate_head(x128, odd):
    swapped = pltpu.roll(x128, HEAD_DIM, axis=1)
    low = lax.broadcasted_iota(jnp.int32, x128.shape, 1) < HEAD_DIM
    return jnp.where(low != odd, x128, swapped)


def _replicate_kv(x256, kvh):
    col = x256[:, (kvh // 2) * LANES:(kvh // 2 + 1) * LANES]
    rep = _replicate_head(col, kvh % 2 == 1)
    return jnp.concatenate([rep, rep], axis=1)


def _expand_heads(q):
    head = lax.shift_right_logical(lax.broadcasted_iota(jnp.int32, q.shape, 1), 6)
    return jnp.concatenate([jnp.where(head == h, q, 0.0) for h in range(GROUP)], axis=0)


def _collapse_heads(o_rep, rows):
    head = lax.shift_right_logical(lax.broadcasted_iota(jnp.int32, (rows, GROUP_LANES), 1), 6)
    out = jnp.zeros((rows, GROUP_LANES), F32)
    for h in range(GROUP):
        out = jnp.where(head == h, o_rep[h * rows:(h + 1) * rows], out)
    return out


def _top3_mask(gate, n_valid):
    lane = lax.broadcasted_iota(jnp.int32, gate.shape, 1)
    lane_f = lane.astype(F32)
    valid = lane < n_valid
    g = jnp.where(valid, gate, NEG_INF)
    sel = jnp.zeros(gate.shape, jnp.bool_)
    for _ in range(MOBA_TOPK):
        mx = jnp.max(g, axis=1, keepdims=True)
        first = jnp.min(jnp.where(g == mx, lane_f, 1e9), axis=1, keepdims=True)
        hit = lane_f == first
        sel = jnp.logical_or(sel, jnp.logical_and(hit, valid))
        g = jnp.where(hit, BELOW_NEG_INF, g)
    return sel


def _qkv_kernel(n_add, *refs):
    refs = list(refs)
    x_ref = refs.pop(0)
    add_refs = [refs.pop(0) for _ in range(n_add)]
    g_ref, w_ref, b_ref, cos_ref, sin_ref = refs[:5]
    outs = refs[5:]
    x = x_ref[...]
    for r in add_refs:
        x = x + _load_token_tiles(r, ROW_TILE)
    if n_add:
        outs.pop(0)[...] = x
    q_ref, k_ref, v_ref, krep_ref = outs[:4]
    h = _rms(x, g_ref[...]).astype(BF16)
    y = jnp.dot(h, w_ref[...], preferred_element_type=F32) + b_ref[...]
    cos = cos_ref[...]
    sin = sin_ref[...]
    low = lax.bitwise_and(lax.broadcasted_iota(jnp.int32, cos.shape, 1), HEAD_DIM - 1) < HEAD_DIM // 2

    def rope(slab):
        partner = jnp.where(low, pltpu.roll(slab, LANES - HEAD_DIM // 2, axis=1),
                            pltpu.roll(slab, HEAD_DIM // 2, axis=1))
        return slab * cos + partner * sin

    scale = HEAD_DIM ** -0.5 * LOG2_E
    for c in range(Q_DIM // LANES):
        q_ref[:, c * LANES:(c + 1) * LANES] = rope(y[:, c * LANES:(c + 1) * LANES]) * scale
    k = jnp.concatenate([rope(y[:, Q_DIM + c * LANES:Q_DIM + (c + 1) * LANES])
                         for c in range(KV_DIM // LANES)], axis=1)
    v = y[:, Q_DIM + KV_DIM:]
    k_ref[...] = k
    v_ref[...] = v
    for kvh in range(N_KV_HEADS):
        krep_ref[kvh] = _replicate_kv(k, kvh).astype(BF16)
    kmean_ref, qt_ref, vt_ref = outs[4:]
    kmean_ref[0] = jnp.broadcast_to(jnp.mean(k, axis=0, keepdims=True), (8, KV_DIM))
    qt_ref[...] = q_ref[...].T
    vt = v.T.reshape(N_KV_HEADS, HEAD_DIM, ROW_TILE).astype(BF16)
    vt_ref[:, 0] = jnp.concatenate([vt, jnp.ones((N_KV_HEADS, VT_ROWS - HEAD_DIM, ROW_TILE), BF16)], axis=1)


def _qkv_rope(x, adds, g, w, b, cos_t, sin_t):
    t = x.shape[0]
    nt = t // ROW_TILE
    row = lambda i: (i, 0)
    fixed = lambda i: (0, 0)
    in_specs = [pl.BlockSpec((ROW_TILE, D_MODEL), row)]
    args = [x]
    for arr, off in adds:
        in_specs.append(pl.BlockSpec((ROW_TILE * TOKEN_TILE, LANES),
                                     functools.partial(lambda o, i: (i + o, 0), off // ROW_TILE)))
        args.append(arr)
    in_specs += [pl.BlockSpec((1, D_MODEL), fixed), pl.BlockSpec((D_MODEL, QKV_DIM), fixed),
                 pl.BlockSpec((1, QKV_DIM), fixed), pl.BlockSpec((ROW_TILE, LANES), row),
                 pl.BlockSpec((ROW_TILE, LANES), row)]
    args += [g.reshape(1, D_MODEL), w.astype(BF16), b.reshape(1, QKV_DIM), cos_t, sin_t]
    out_shape, out_specs = [], []
    if adds:
        out_shape.append(jax.ShapeDtypeStruct((t, D_MODEL), F32))
        out_specs.append(pl.BlockSpec((ROW_TILE, D_MODEL), row))
    rep_shape = jax.ShapeDtypeStruct((N_KV_HEADS, t, GROUP_LANES), BF16)
    rep_spec = pl.BlockSpec((N_KV_HEADS, ROW_TILE, GROUP_LANES), lambda i: (0, i, 0))
    out_shape += [jax.ShapeDtypeStruct((t, Q_DIM), F32), jax.ShapeDtypeStruct((t, KV_DIM), F32),
                  jax.ShapeDtypeStruct((t, KV_DIM), F32), rep_shape]
    out_specs += [pl.BlockSpec((ROW_TILE, Q_DIM), row), pl.BlockSpec((ROW_TILE, KV_DIM), row),
                  pl.BlockSpec((ROW_TILE, KV_DIM), row), rep_spec]
    out_shape += [jax.ShapeDtypeStruct((nt, 8, KV_DIM), F32), jax.ShapeDtypeStruct((Q_DIM, t), F32),
                  jax.ShapeDtypeStruct((N_KV_HEADS, nt, VT_ROWS, ROW_TILE), BF16)]
    out_specs += [pl.BlockSpec((1, 8, KV_DIM), lambda i: (i, 0, 0)),
                  pl.BlockSpec((Q_DIM, ROW_TILE), lambda i: (0, i)),
                  pl.BlockSpec((N_KV_HEADS, 1, VT_ROWS, ROW_TILE), lambda i: (0, i, 0, 0))]
    return pl.pallas_call(
        functools.partial(_qkv_kernel, len(adds)),
        grid=(nt,), in_specs=in_specs, out_specs=out_specs, out_shape=out_shape,
        compiler_params=_params("arbitrary"), name="qkv_rope")(*args)


def _sink_softmax_pv(s, ok, sink_col, v):
    s = jnp.where(ok, s, NEG_INF)
    m = jnp.maximum(jnp.max(s, axis=1, keepdims=True), sink_col)
    p = jnp.exp2(s - m)
    denom = jnp.sum(p, axis=1, keepdims=True) + jnp.exp2(sink_col - m)
    return jnp.dot(p.astype(BF16), v, preferred_element_type=F32) / denom


def _sink_column(sinks_ref, kvh, rows):
    blk = lax.shift_right_logical(lax.broadcasted_iota(jnp.int32, (GROUP * rows, 1), 0), rows.bit_length() - 1)
    col = jnp.zeros((GROUP * rows, 1), F32)
    for h in range(GROUP):
        col = jnp.where(blk == h, sinks_ref[kvh * GROUP + h] * LOG2_E, col)
    return col


def _swa_prompt_kernel(sinks_ref, qt_ref, kprev_ref, kcur_ref, vprev_ref, vcur_ref, o_ref):
    i = pl.program_id(1)
    tile = ROW_TILE
    n_keys = WINDOW + tile
    key = lax.broadcasted_iota(jnp.int32, (n_keys, tile), 0)
    qry = lax.broadcasted_iota(jnp.int32, (n_keys, tile), 1)
    first_key = jnp.where(i > 0, 0, WINDOW)
    ok = (key > qry) & (key <= qry + WINDOW) & (key >= first_key)
    slabs = []
    for kvh in range(N_KV_HEADS):
        k = jnp.concatenate([kprev_ref[kvh, :, :2 * HEAD_DIM], kcur_ref[kvh, :, :2 * HEAD_DIM]], axis=0)
        vt = jnp.concatenate([vprev_ref[kvh, 0], vcur_ref[kvh, 0]], axis=1)
        outs = []
        for g in range(GROUP):
            h = kvh * GROUP + g
            qh = jnp.concatenate([qt_ref[h * HEAD_DIM:(h + 1) * HEAD_DIM, :].astype(BF16),
                                  jnp.zeros((HEAD_DIM, tile), BF16)], axis=0)
            s = jnp.where(ok, jnp.dot(k, qh, preferred_element_type=F32), NEG_INF)
            sink = sinks_ref[h] * LOG2_E
            m = jnp.maximum(jnp.max(s, axis=0, keepdims=True), sink)
            pv = jnp.dot(vt, jnp.exp2(s - m).astype(BF16), preferred_element_type=F32)
            denom = pv[HEAD_DIM:HEAD_DIM + 1] + jnp.exp2(sink - m)
            outs.append(pv[:HEAD_DIM] / denom)
        slabs.append(jnp.concatenate(outs, axis=0).T.astype(BF16))
    o_ref[...] = jnp.concatenate(slabs, axis=1)


def _swa_prompt(qt, k_rep, vt, sinks, batch, seq):
    nt = seq // ROW_TILE
    per_tile = ROW_TILE // WINDOW
    return pl.pallas_call(
        _swa_prompt_kernel,
        grid=(batch, nt),
        in_specs=[pl.BlockSpec(memory_space=pltpu.SMEM),
                  pl.BlockSpec((Q_DIM, ROW_TILE), lambda b, i: (0, b * nt + i)),
                  pl.BlockSpec((N_KV_HEADS, WINDOW, GROUP_LANES),
                               lambda b, i: (0, jnp.maximum((b * nt + i) * per_tile - 1, 0), 0)),
                  pl.BlockSpec((N_KV_HEADS, ROW_TILE, GROUP_LANES), lambda b, i: (0, b * nt + i, 0)),
                  pl.BlockSpec((N_KV_HEADS, 1, VT_ROWS, WINDOW),
                               lambda b, i: (0, jnp.maximum(b * nt + i - 1, 0), 0, per_tile - 1)),
                  pl.BlockSpec((N_KV_HEADS, 1, VT_ROWS, ROW_TILE), lambda b, i: (0, b * nt + i, 0, 0))],
        out_specs=pl.BlockSpec((ROW_TILE, Q_DIM), lambda b, i: (b * nt + i, 0)),
        out_shape=jax.ShapeDtypeStruct((batch * seq, Q_DIM), BF16),
        compiler_params=_params("arbitrary", "arbitrary"), name="swa_prompt",
    )(sinks, qt, k_rep, k_rep, vt, vt)


def _swa_sample_kernel(seqs, nq, sinks_ref, q_ref, kn_ref, vn_ref, ck_ref, cv_ref, o_ref, ko_ref, vo_ref):
    win = ck_ref.shape[1]

    def one_seq(s, carry):
        r0 = pl.multiple_of(s * nq, nq)
        q = q_ref[pl.ds(r0, nq), :]
        kn = kn_ref[pl.ds(r0, nq), :]
        vn = vn_ref[pl.ds(r0, nq), :]
        ck = ck_ref[s]
        cv = cv_ref[s]
        ko_ref[s] = jnp.concatenate([ck[nq:], kn], axis=0)
        vo_ref[s] = jnp.concatenate([cv[nq:], vn], axis=0)
        kk = jnp.concatenate([ck, kn], axis=0)
        vv = jnp.concatenate([cv, vn], axis=0)
        outs = []
        for kvh in range(N_KV_HEADS):
            qexp = _expand_heads(q[:, kvh * GROUP_LANES:(kvh + 1) * GROUP_LANES]).astype(BF16)
            k = _replicate_kv(kk, kvh).astype(BF16)
            v = _replicate_kv(vv, kvh).astype(BF16)
            sc = lax.dot_general(qexp, k, NT_DIMS, preferred_element_type=F32)
            qi = lax.bitwise_and(lax.broadcasted_iota(jnp.int32, sc.shape, 0), nq - 1)
            kj = lax.broadcasted_iota(jnp.int32, sc.shape, 1)
            ok = ((kj < win) & (kj > qi + win - WINDOW)) | ((kj >= win) & (kj - win <= qi))
            o_rep = _sink_softmax_pv(sc, ok, _sink_column(sinks_ref, kvh, nq), v)
            outs.append(_collapse_heads(o_rep, nq))
        o_ref[pl.ds(r0, nq), :] = jnp.concatenate(outs, axis=1).astype(BF16)
        return carry

    lax.fori_loop(0, seqs, one_seq, 0)


def _swa_sample(q, k, v, cache_k, cache_v, sinks, row0, nq):
    db, win = cache_k.shape[0], cache_k.shape[1]
    seqs = 8
    rows = seqs * nq
    tok = lambda g: (row0 // rows + g, 0)
    cache = lambda g: (g, 0, 0)
    return pl.pallas_call(
        functools.partial(_swa_sample_kernel, seqs, nq),
        grid=(db // seqs,),
        in_specs=[pl.BlockSpec(memory_space=pltpu.SMEM),
                  pl.BlockSpec((rows, Q_DIM), tok), pl.BlockSpec((rows, KV_DIM), tok),
                  pl.BlockSpec((rows, KV_DIM), tok),
                  pl.BlockSpec((seqs, win, KV_DIM), cache), pl.BlockSpec((seqs, win, KV_DIM), cache)],
        out_specs=[pl.BlockSpec((rows, Q_DIM), lambda g: (g, 0)),
                   pl.BlockSpec((seqs, win, KV_DIM), cache), pl.BlockSpec((seqs, win, KV_DIM), cache)],
        out_shape=[jax.ShapeDtypeStruct((db * nq, Q_DIM), BF16),
                   jax.ShapeDtypeStruct((db, win, KV_DIM), F32), jax.ShapeDtypeStruct((db, win, KV_DIM), F32)],
        compiler_params=_params("arbitrary"), name="swa_sample",
    )(sinks, q, k, v, cache_k.reshape(db, win, KV_DIM), cache_v.reshape(db, win, KV_DIM))


def _top3_rows(gate, n_valid):
    row = lax.broadcasted_iota(jnp.int32, gate.shape, 0)
    row_f = row.astype(F32)
    valid = row < n_valid
    g = jnp.where(valid, gate, NEG_INF)
    sel = jnp.zeros(gate.shape, F32)
    for _ in range(MOBA_TOPK):
        mx = jnp.max(g, axis=0, keepdims=True)
        first = jnp.min(jnp.where(g == mx, row_f, 1e9), axis=0, keepdims=True)
        hit = row_f == first
        sel = jnp.where(jnp.logical_and(hit, valid), 1.0, sel)
        g = jnp.where(hit, BELOW_NEG_INF, g)
    return sel


def _moba_prompt_kernel(qt_ref, k_ref, vt_ref, kmean_ref, o_ref, sel_ref, acc_ref):
    i = pl.program_id(2)
    qt = qt_ref[...]
    row_head = lax.shift_right_logical(lax.broadcasted_iota(jnp.int32, qt.shape, 0), 6)
    qexp32 = jnp.concatenate([jnp.where(row_head == h, qt, 0.0) for h in range(GROUP)], axis=1)
    qexp = qexp32.astype(BF16)
    gate = jnp.dot(kmean_ref[0, 0], qexp32, preferred_element_type=F32, precision=lax.Precision.HIGHEST)
    sel_ref[...] = _top3_rows(gate, i)

    q_heads = [jnp.concatenate([qexp[h * HEAD_DIM:(h + 1) * HEAD_DIM, h * MOBA_BLOCK:(h + 1) * MOBA_BLOCK],
                                jnp.zeros((HEAD_DIM, MOBA_BLOCK), BF16)], axis=0) for h in range(GROUP)]
    heads = [slice(h * MOBA_BLOCK, (h + 1) * MOBA_BLOCK) for h in range(GROUP)]

    def head_scores(j):
        start = pl.multiple_of(j * MOBA_BLOCK, MOBA_BLOCK)
        kj = k_ref[0, pl.ds(start, MOBA_BLOCK), :2 * HEAD_DIM]
        return [jnp.dot(kj, q_heads[h], preferred_element_type=F32) for h in range(GROUP)]

    m = []
    for h, s in enumerate(head_scores(i)):
        key = lax.broadcasted_iota(jnp.int32, s.shape, 0)
        qry = lax.broadcasted_iota(jnp.int32, s.shape, 1)
        s = jnp.where(key <= qry, s, NEG_INF)
        m.append(jnp.max(s, axis=0, keepdims=True))
        acc_ref[:, heads[h]] = jnp.dot(vt_ref[0, i], jnp.exp2(s - m[h]).astype(BF16), preferred_element_type=F32)

    def past_block(j, m_old):
        vtj = vt_ref[0, j]
        m_out = []
        for h, s in enumerate(head_scores(j)):
            chosen = sel_ref[pl.ds(j, 1), heads[h]] > 0.0
            m_new = jnp.where(chosen, jnp.maximum(m_old[h], jnp.max(s, axis=0, keepdims=True)), m_old[h])
            alpha = jnp.exp2(m_old[h] - m_new)
            p = jnp.exp2(s - jnp.where(chosen, m_new, -NEG_INF))
            acc_ref[:, heads[h]] = alpha * acc_ref[:, heads[h]] + jnp.dot(vtj, p.astype(BF16),
                                                                         preferred_element_type=F32)
            m_out.append(m_new)
        return tuple(m_out)

    def past_group(t, m_old):
        for u in range(MOBA_BLOCKS_PER_TRIP):
            m_old = past_block(MOBA_BLOCKS_PER_TRIP * t + u, m_old)
        return m_old

    lax.fori_loop(0, (i + MOBA_BLOCKS_PER_TRIP - 1) // MOBA_BLOCKS_PER_TRIP, past_group, tuple(m))
    ot = acc_ref[:HEAD_DIM] / acc_ref[HEAD_DIM:HEAD_DIM + 1]
    ot = jnp.concatenate([ot[:, h * MOBA_BLOCK:(h + 1) * MOBA_BLOCK] for h in range(GROUP)], axis=0)
    o_ref[...] = ot.T.astype(BF16)


def _moba_prompt(qt, k_rep, vt, kmean_rep, batch, seq):
    nb = seq // MOBA_BLOCK
    cols = GROUP * MOBA_BLOCK
    return pl.pallas_call(
        _moba_prompt_kernel,
        grid=(batch, N_KV_HEADS, nb),
        in_specs=[pl.BlockSpec((GROUP_LANES, MOBA_BLOCK), lambda b, h, i: (h, b * nb + i)),
                  pl.BlockSpec((1, seq, GROUP_LANES), lambda b, h, i: (h, b, 0)),
                  pl.BlockSpec((1, nb, VT_ROWS, MOBA_BLOCK), lambda b, h, i: (h, b, 0, 0)),
                  pl.BlockSpec((1, 1, nb, GROUP_LANES), lambda b, h, i: (b, h, 0, 0))],
        out_specs=pl.BlockSpec((MOBA_BLOCK, GROUP_LANES), lambda b, h, i: (b * nb + i, h)),
        out_shape=jax.ShapeDtypeStruct((batch * seq, Q_DIM), BF16),
        scratch_shapes=[pltpu.VMEM((nb, cols), F32), pltpu.VMEM((VT_ROWS, cols), F32)],
        compiler_params=_params("arbitrary", "arbitrary", "arbitrary"), name="moba_prompt",
    )(qt, k_rep, vt, kmean_rep)


def _moba_sample_kernel(nq, n_chunks, n_seq, pt_ref, qexp_ref, kn_ref, vn_ref, expand_ref, pk_hbm, pv_hbm,
                        o_ref, cbuf, s_ref, sem):
    b = pl.program_id(0)
    pages = PAGES_PER_STEP
    keys = pages * PAGE_SIZE
    blocks_per_chunk = keys // MOBA_BLOCK
    n_blocks = n_chunks * blocks_per_chunk
    n_stream = 2 * n_chunks
    ahead = PAGE_SLOTS - 1
    assert n_stream % PAGE_SLOTS == 0 and ahead <= n_stream

    def chunk_copies(seq, j, lookup):
        pool = pk_hbm if j < n_chunks else pv_hbm
        first = (j % n_chunks) * pages
        return [pltpu.make_async_copy(pool.at[pt_ref[seq, first + p] if lookup else 0],
                                      cbuf.at[j % PAGE_SLOTS, p], sem.at[j % PAGE_SLOTS])
                for p in range(pages)]

    def chunk(j):
        return jnp.concatenate([cbuf[j % PAGE_SLOTS, p] for p in range(pages)], axis=1)

    def start(seq, j):
        for p, cp in enumerate(chunk_copies(seq, j, True)):
            cp.start(priority=p % 2)

    @pl.when(b == 0)
    def _():
        for j in range(ahead):
            start(b, j)

    qexp32 = qexp_ref[0]
    qexp = qexp32.astype(BF16)
    blk_lane = lax.broadcasted_iota(jnp.int32, (KV_DIM, LANES), 1)
    ksum = jnp.zeros((KV_DIM, LANES), F32)
    for j in range(n_stream):
        if j + ahead < n_stream:
            start(b, j + ahead)
        else:
            @pl.when(b + 1 < n_seq)
            def _():
                start(b + 1, j + ahead - n_stream)
        for cp in chunk_copies(b, j, False):
            cp.wait()
        cols = slice((j % n_chunks) * keys, (j % n_chunks + 1) * keys)
        if j < n_chunks:
            kt = chunk(j)
            s_ref[:, cols] = jnp.dot(qexp, kt.astype(BF16), preferred_element_type=F32)
            for n in range(blocks_per_chunk):
                blk_sum = jnp.sum(kt[:, n * MOBA_BLOCK:(n + 1) * MOBA_BLOCK], axis=1, keepdims=True)
                ksum = jnp.where(blk_lane == j * blocks_per_chunk + n, blk_sum, ksum)
        if j == n_chunks - 1:
            gate = jnp.dot(qexp32, ksum * (1.0 / MOBA_BLOCK), preferred_element_type=F32,
                           precision=lax.Precision.HIGHEST)
            sel_bias = jnp.where(_top3_mask(gate, n_blocks), 0.0, NEG_INF).astype(BF16)
            s_own = lax.dot_general(qexp, kn_ref[...].astype(BF16), NT_DIMS, preferred_element_type=F32)
            qi = lax.bitwise_and(lax.broadcasted_iota(jnp.int32, s_own.shape, 0), nq - 1)
            kj = lax.broadcasted_iota(jnp.int32, s_own.shape, 1)
            s_own = jnp.where(kj <= qi, s_own, NEG_INF)
            m_tile = jnp.full((N_HEADS * nq, MOBA_BLOCK), NEG_INF, F32)
            for c in range(n_chunks):
                cc = slice(c * keys, (c + 1) * keys)
                sm = s_ref[:, cc] + jnp.dot(sel_bias, expand_ref[:, cc], preferred_element_type=F32)
                s_ref[:, cc] = sm
                for n in range(blocks_per_chunk):
                    m_tile = jnp.maximum(m_tile, sm[:, n * MOBA_BLOCK:(n + 1) * MOBA_BLOCK])
            m = jnp.maximum(jnp.max(m_tile, axis=1, keepdims=True), jnp.max(s_own, axis=1, keepdims=True))
            p_own = jnp.exp2(s_own - m)
            l_own = jnp.sum(p_own, axis=1, keepdims=True)
            acc = jnp.dot(p_own.astype(BF16), vn_ref[...].astype(BF16), preferred_element_type=F32)
            l_tile = jnp.zeros((N_HEADS * nq, MOBA_BLOCK), F32)
        if j >= n_chunks:
            p = jnp.exp2(s_ref[:, cols] - m)
            for n in range(blocks_per_chunk):
                l_tile = l_tile + p[:, n * MOBA_BLOCK:(n + 1) * MOBA_BLOCK]
            acc = acc + lax.dot_general(p.astype(BF16), chunk(j).astype(BF16), NT_DIMS,
                                        preferred_element_type=F32)
    o_ref[0] = acc / (l_own + jnp.sum(l_tile, axis=1, keepdims=True))


def _moba_sample(qexp, k, v, pool_k, pool_v, page_table, row0, nq):
    db, n_pages = page_table.shape
    n_pool = pool_k.shape[0]
    pages = PAGES_PER_STEP
    n_chunks = n_pages // pages
    rows = N_HEADS * nq
    pk = pool_k.transpose(0, 2, 3, 1).reshape(n_pool, KV_DIM, PAGE_SIZE)
    pv = pool_v.transpose(0, 2, 3, 1).reshape(n_pool, KV_DIM, PAGE_SIZE)
    n_keys = n_pages * PAGE_SIZE
    expand = (jnp.arange(LANES)[:, None] == (jnp.arange(n_keys) // MOBA_BLOCK)[None, :]).astype(BF16)
    new = lambda b, pt: (row0 // nq + b, 0)
    grid_spec = pltpu.PrefetchScalarGridSpec(
        num_scalar_prefetch=1,
        grid=(db,),
        in_specs=[pl.BlockSpec((1, rows, KV_DIM), lambda b, pt: (b, 0, 0)),
                  pl.BlockSpec((nq, KV_DIM), new), pl.BlockSpec((nq, KV_DIM), new),
                  pl.BlockSpec((LANES, n_keys), lambda b, pt: (0, 0)),
                  pl.BlockSpec(memory_space=pl.ANY), pl.BlockSpec(memory_space=pl.ANY)],
        out_specs=pl.BlockSpec((1, rows, KV_DIM), lambda b, pt: (b, 0, 0)),
        scratch_shapes=[pltpu.VMEM((PAGE_SLOTS, pages, KV_DIM, PAGE_SIZE), F32),
                        pltpu.VMEM((rows, n_keys), F32),
                        pltpu.SemaphoreType.DMA((PAGE_SLOTS,))])
    return pl.pallas_call(
        functools.partial(_moba_sample_kernel, nq, n_chunks, db),
        grid_spec=grid_spec,
        out_shape=jax.ShapeDtypeStruct((db, rows, KV_DIM), F32),
        compiler_params=_params("arbitrary"), name="moba_sample",
    )(page_table, qexp, k, v, expand, pk, pv)


def _proj_route_kernel(o_ref, x_ref, wo_ref, bo_ref, g_ref, wr_ref, br_ref, x1_ref, h_ref, route_ref):
    x1 = x_ref[...] + jnp.dot(o_ref[...], wo_ref[...], preferred_element_type=F32) + bo_ref[...]
    x1_ref[...] = x1
    h = _rms(x1, g_ref[...])
    _store_token_tiles(h_ref, h)
    lg = jnp.dot(h, wr_ref[...], preferred_element_type=F32, precision=lax.Precision.HIGHEST) + br_ref[...]
    lane = lax.broadcasted_iota(jnp.int32, lg.shape, 1)
    lane_f = lane.astype(F32)
    is_group = lane < N_GROUPS
    gl = jnp.where(is_group, lg, NEG_INF)
    gmax = jnp.max(gl, axis=1, keepdims=True)
    grp = jnp.min(jnp.where(gl == gmax, lane_f, 1e9), axis=1, keepdims=True)
    g_w = 1.0 / jnp.sum(jnp.where(is_group, jnp.exp(gl - gmax), 0.0), axis=1, keepdims=True)
    lane_grp = lax.shift_right_logical(lane - N_GROUPS, 3).astype(F32)
    in_grp = (lane >= N_GROUPS) & (lane < N_GROUPS + N_EXPERTS) & (lane_grp == grp)
    el = jnp.where(in_grp, lg, NEG_INF)
    v1 = jnp.max(el, axis=1, keepdims=True)
    i1 = jnp.min(jnp.where(el == v1, lane_f, 1e9), axis=1, keepdims=True)
    el2 = jnp.where(lane_f == i1, BELOW_NEG_INF, el)
    v2 = jnp.max(el2, axis=1, keepdims=True)
    i2 = jnp.min(jnp.where(el2 == v2, lane_f, 1e9), axis=1, keepdims=True)
    e2w = jnp.exp(v2 - v1)
    w1 = g_w / (1.0 + e2w)
    w2 = g_w * e2w / (1.0 + e2w)
    route = jnp.where(lane == 0, i1 - N_GROUPS, 0.0)
    route = jnp.where(lane == 1, i2 - N_GROUPS, route)
    route = jnp.where(lane == 2, w1, route)
    route = jnp.where(lane == 3, w2, route)
    route_ref[...] = route


def _proj_route(o, x, wo, bo, g, w_gr, b_gr, w_er, b_er):
    t = x.shape[0]
    row = lambda i: (i, 0)
    fixed = lambda i: (0, 0)
    pad = LANES - N_GROUPS - N_EXPERTS
    wr = jnp.concatenate([w_gr, w_er, jnp.zeros((D_MODEL, pad), F32)], axis=1)
    br = jnp.concatenate([b_gr, b_er, jnp.zeros((pad,), F32)]).reshape(1, LANES)
    return pl.pallas_call(
        _proj_route_kernel,
        grid=(t // ROW_TILE,),
        in_specs=[pl.BlockSpec((ROW_TILE, Q_DIM), row), pl.BlockSpec((ROW_TILE, D_MODEL), row),
                  pl.BlockSpec((Q_DIM, D_MODEL), fixed), pl.BlockSpec((1, D_MODEL), fixed),
                  pl.BlockSpec((1, D_MODEL), fixed), pl.BlockSpec((D_MODEL, LANES), fixed),
                  pl.BlockSpec((1, LANES), fixed)],
        out_specs=[pl.BlockSpec((ROW_TILE, D_MODEL), row), pl.BlockSpec((ROW_TILE * TOKEN_TILE, LANES), row),
                   pl.BlockSpec((ROW_TILE, LANES), row)],
        out_shape=[jax.ShapeDtypeStruct((t, D_MODEL), F32), jax.ShapeDtypeStruct((t * TOKEN_TILE, LANES), F32),
                   jax.ShapeDtypeStruct((t, LANES), F32)],
        compiler_params=_params("arbitrary"), name="proj_route",
    )(o, x, wo.astype(BF16), bo.reshape(1, D_MODEL), g.reshape(1, D_MODEL), wr, br)


def _expert_kernel(n_steps, blk_e_ref, nvalid_ref, src_ref, src_next_ref, dst_ref, wrow_ref, h_hbm,
                   wg_ref, wu_ref, wd_ref, y_hbm, xbuf, ybuf, wg_bf, wu_bf, wd_bf, gsem, ssem):
    i = pl.program_id(0)
    slot = lax.rem(i, 2)
    rows = EXPERT_ROWS
    nvalid = nvalid_ref[i]
    nvalid_next = jnp.where(i + 1 < n_steps, nvalid_ref[jnp.minimum(i + 1, n_steps - 1)], 0)
    nvalid_prev2 = jnp.where(i >= 2, nvalid_ref[jnp.maximum(i - 2, 0)], 0)

    def tile(ref, first):
        return ref.at[pl.ds(pl.multiple_of(first, TOKEN_TILE), TOKEN_TILE)]

    def row_copy_in(idx_ref, r, s):
        return pltpu.make_async_copy(tile(h_hbm, idx_ref[0, r]), tile(xbuf.at[s], r * TOKEN_TILE), gsem.at[s])

    def row_copy_out(r, s):
        return pltpu.make_async_copy(tile(ybuf.at[s], r * TOKEN_TILE), tile(y_hbm, dst_ref[0, r]), ssem.at[s])

    def for_rows(fn, n=None):
        def body(g, carry):
            for u in range(8):
                r = g * 8 + u
                if n is None:
                    fn(r, u)
                else:
                    pl.when(r < n)(functools.partial(fn, r, u))
            return carry
        lax.fori_loop(0, rows // 8, body, 0)

    @pl.when((i == 0) & (nvalid > 0))
    def _():
        for_rows(lambda r, u: row_copy_in(src_ref, r, 0).start(priority=u % 2))

    @pl.when(nvalid_next > 0)
    def _():
        for_rows(lambda r, u: row_copy_in(src_next_ref, r, 1 - slot).start(priority=u % 2))

    @pl.when(nvalid_prev2 > 0)
    def _():
        for_rows(lambda r, u: row_copy_out(r, slot).wait(), nvalid_prev2)

    @pl.when(nvalid > 0)
    def _():
        e_changed = jnp.logical_or(i == 0, blk_e_ref[i] != blk_e_ref[jnp.maximum(i - 1, 0)])

        @pl.when(e_changed)
        def _():
            wg_bf[...] = wg_ref[0, 0].astype(BF16)
            wu_bf[...] = wu_ref[0, 0].astype(BF16)
            wd_bf[...] = wd_ref[0, 0].astype(BF16)

        for_rows(lambda r, u: row_copy_in(src_ref, r, slot).wait())
        x = _load_token_tiles(xbuf.at[slot], rows).astype(BF16)
        gate = jnp.dot(x, wg_bf[...], preferred_element_type=F32)
        up = jnp.dot(x, wu_bf[...], preferred_element_type=F32)
        hid = (gate * jax.nn.sigmoid(gate) * up).astype(BF16)
        y = jnp.dot(hid, wd_bf[...], preferred_element_type=F32)
        _store_token_tiles(ybuf.at[slot], y * wrow_ref[...])
        for_rows(lambda r, u: row_copy_out(r, slot).start(priority=u % 2), nvalid)


def _expert_ffn(h, e_idx, e_w, w_gate, w_up, w_down, layer):
    t = h.shape[0] // TOKEN_TILE
    a = t * EXPERT_TOPK
    rows = EXPERT_ROWS
    n_blk = -(-a // rows) + N_EXPERTS
    n_steps = n_blk + 2
    n_rows = n_steps * rows
    flat_e = e_idx.reshape(a)
    order = jnp.argsort(flat_e).astype(jnp.int32)
    counts = jnp.sum(flat_e[:, None] == jnp.arange(N_EXPERTS, dtype=jnp.int32)[None, :], axis=0, dtype=jnp.int32)
    padded = (counts + rows - 1) // rows * rows
    pad_end = jnp.cumsum(padded)
    pad_start = pad_end - padded
    start = jnp.cumsum(counts) - counts
    blk_first = jnp.arange(n_steps, dtype=jnp.int32) * rows
    blk_e = jnp.minimum(jnp.sum(blk_first[:, None] >= pad_end[None, :], axis=1), N_EXPERTS - 1).astype(jnp.int32)
    rank0 = blk_first - pad_start[blk_e]
    nvalid = jnp.clip(counts[blk_e] - rank0, 0, rows).astype(jnp.int32)
    within = jnp.arange(rows, dtype=jnp.int32)[None, :]
    valid = within < nvalid[:, None]
    assign = order[jnp.clip((start[blk_e] + rank0)[:, None] + within, 0, a - 1)]
    tok = assign // EXPERT_TOPK
    src = jnp.where(valid, tok, 0) * TOKEN_TILE
    dst = jnp.where(valid, (assign % EXPERT_TOPK) * t + tok, 0) * TOKEN_TILE
    wrow = jnp.where(valid, e_w.reshape(a)[assign], 0.0)
    idx_blk = lambda i, be, na: (i, 0, 0)
    idx_next = lambda i, be, na: (jnp.minimum(i + 1, n_steps - 1), 0, 0)
    w_in = lambda i, be, na: (layer, be[i], 0, 0)
    grid_spec = pltpu.PrefetchScalarGridSpec(
        num_scalar_prefetch=2,
        grid=(n_steps,),
        in_specs=[pl.BlockSpec((1, 1, rows), idx_blk, memory_space=pltpu.SMEM),
                  pl.BlockSpec((1, 1, rows), idx_next, memory_space=pltpu.SMEM),
                  pl.BlockSpec((1, 1, rows), idx_blk, memory_space=pltpu.SMEM),
                  pl.BlockSpec((rows, 1), lambda i, be, na: (i, 0)),
                  pl.BlockSpec(memory_space=pl.ANY),
                  pl.BlockSpec((1, 1, D_MODEL, EXPERT_FF), w_in), pl.BlockSpec((1, 1, D_MODEL, EXPERT_FF), w_in),
                  pl.BlockSpec((1, 1, EXPERT_FF, D_MODEL), w_in)],
        out_specs=pl.BlockSpec(memory_space=pl.ANY),
        scratch_shapes=[pltpu.VMEM((2, rows * TOKEN_TILE, LANES), F32), pltpu.VMEM((2, rows * TOKEN_TILE, LANES), F32),
                        pltpu.VMEM((D_MODEL, EXPERT_FF), BF16), pltpu.VMEM((D_MODEL, EXPERT_FF), BF16),
                        pltpu.VMEM((EXPERT_FF, D_MODEL), BF16),
                        pltpu.SemaphoreType.DMA((2,)), pltpu.SemaphoreType.DMA((2,))])

    def body(blk_e_ref, nvalid_ref, src_ref, src_next_ref, dst_ref, *rest):
        _expert_kernel(n_steps, blk_e_ref, nvalid_ref, src_ref.at[0], src_next_ref.at[0], dst_ref.at[0], *rest)

    return pl.pallas_call(
        body, grid_spec=grid_spec,
        out_shape=jax.ShapeDtypeStruct((EXPERT_TOPK * t * TOKEN_TILE, LANES), F32),
        compiler_params=_params("arbitrary"), name="expert_ffn",
    )(blk_e, nvalid, src.reshape(n_steps, 1, rows), src.reshape(n_steps, 1, rows),
      dst.reshape(n_steps, 1, rows), wrow.reshape(n_rows, 1), h, w_gate, w_up, w_down)


def _final_kernel(x_ref, y0_ref, y1_ref, g_ref, o_ref):
    y = _load_token_tiles(y0_ref, ROW_TILE) + _load_token_tiles(y1_ref, ROW_TILE)
    o_ref[...] = _rms(x_ref[...] + y, g_ref[...])


def _final_norm(x, y2, g, row0, n_rows):
    t = x.shape[0]
    off = row0 // ROW_TILE
    return pl.pallas_call(
        _final_kernel,
        grid=(n_rows // ROW_TILE,),
        in_specs=[pl.BlockSpec((ROW_TILE, D_MODEL), lambda i: (i + off, 0)),
                  pl.BlockSpec((ROW_TILE * TOKEN_TILE, LANES), lambda i: (i + off, 0)),
                  pl.BlockSpec((ROW_TILE * TOKEN_TILE, LANES), lambda i: (i + off + t // ROW_TILE, 0)),
                  pl.BlockSpec((1, D_MODEL), lambda i: (0, 0))],
        out_specs=pl.BlockSpec((ROW_TILE, D_MODEL), lambda i: (i, 0)),
        out_shape=jax.ShapeDtypeStruct((n_rows, D_MODEL), F32),
        compiler_params=_params("arbitrary"), name="final_norm",
    )(x, y2, y2, g.reshape(1, D_MODEL))


def _rope_tables(pos):
    half = HEAD_DIM // 2
    inv = ROPE_THETA ** (-jnp.arange(half, dtype=F32) / half)
    ang = pos.astype(F32)[:, None] * inv[None, :]
    cos = jnp.cos(ang)
    sin = jnp.sin(ang)
    reps = LANES // HEAD_DIM
    return jnp.tile(jnp.concatenate([cos, cos], axis=1), (1, reps)), jnp.tile(jnp.concatenate([-sin, sin], axis=1), (1, reps))


def _moe(o, x, wo, bo, g, w_gr, b_gr, w_er, b_er, w_gate, w_up, w_down, layer):
    x1, h, route = _proj_route(o, x, wo, bo, g, w_gr, b_gr, w_er, b_er)
    e_idx = jnp.clip(route[:, :EXPERT_TOPK].astype(jnp.int32), 0, N_EXPERTS - 1)
    e_w = route[:, EXPERT_TOPK:2 * EXPERT_TOPK]
    return x1, _expert_ffn(h, e_idx, e_w, w_gate, w_up, w_down, layer)


def kernel(x_prompt, x_sample, cache_swa_k, cache_swa_v, cache_moba_k, cache_moba_v, page_table, norm_mix, norm_ffn, norm_final, swa_w_qkv, swa_b_qkv, swa_sinks, swa_w_o, swa_b_o, moba_w_qkv, moba_w_o, w_group_router, b_group_router, w_expert_router, b_expert_router, w_gate, w_up, w_down):
    batch, seq, _ = x_prompt.shape
    db, nq, _ = x_sample.shape
    past_len = page_table.shape[1] * PAGE_SIZE
    tp = batch * seq
    ts = db * nq
    t = tp + ts
    x = jnp.concatenate([x_prompt.reshape(tp, D_MODEL), x_sample.reshape(ts, D_MODEL)], axis=0)
    pos = jnp.concatenate([jnp.tile(jnp.arange(seq), batch), jnp.tile(past_len + jnp.arange(nq), db)])
    cos_t, sin_t = _rope_tables(pos)

    q, k0, v0, k_rep, _, qt, vt = _qkv_rope(x, [], norm_mix[0], swa_w_qkv[0], swa_b_qkv[0], cos_t, sin_t)
    o_p = _swa_prompt(qt, k_rep, vt, swa_sinks[0], batch, seq)
    o_s, swa_ks, swa_vs = _swa_sample(q, k0, v0, cache_swa_k[0], cache_swa_v[0], swa_sinks[0], tp, nq)
    o = jnp.concatenate([o_p, o_s], axis=0)
    x1, y2 = _moe(o, x, swa_w_o[0], swa_b_o[0], norm_ffn[0], w_group_router[0], b_group_router[0],
                  w_expert_router[0], b_expert_router[0], w_gate, w_up, w_down, 0)

    x, q, k1, v1, k_rep, kmean, qt, vt = _qkv_rope(
        x1, [(y2, 0), (y2, t)], norm_mix[1], moba_w_qkv[0], jnp.zeros((QKV_DIM,), F32), cos_t, sin_t)
    nb = seq // MOBA_BLOCK
    kmean_rep = jnp.tile(kmean[:batch * nb, 0].reshape(batch, nb, N_KV_HEADS, 1, HEAD_DIM), (1, 1, 1, GROUP, 1))
    kmean_rep = kmean_rep.reshape(batch, nb, N_KV_HEADS, GROUP_LANES).transpose(0, 2, 1, 3)
    o_p = _moba_prompt(qt, k_rep, vt, kmean_rep, batch, seq)
    qs = q[tp:].reshape(db, nq, N_HEADS, 1, HEAD_DIM)
    slot = (jnp.arange(N_HEADS) // GROUP)[:, None] == jnp.arange(N_KV_HEADS)[None, :]
    qexp = jnp.where(slot[None, None, :, :, None], qs, 0.0).transpose(0, 2, 1, 3, 4).reshape(db, N_HEADS * nq, KV_DIM)
    o_s = _moba_sample(qexp, k1, v1, cache_moba_k[0], cache_moba_v[0], page_table, tp, nq)
    o_s = o_s.reshape(db, N_HEADS, nq, N_KV_HEADS, HEAD_DIM)
    o_s = jnp.sum(jnp.where(slot[None, :, None, :, None], o_s, 0.0), axis=3).transpose(0, 2, 1, 3)
    o = jnp.concatenate([o_p, o_s.reshape(ts, Q_DIM).astype(BF16)], axis=0)
    x1, y2 = _moe(o, x, moba_w_o[0], jnp.zeros((D_MODEL,), F32), norm_ffn[1], w_group_router[1],
                  b_group_router[1], w_expert_router[1], b_expert_router[1], w_gate, w_up, w_down, 1)

    y_prompt = _final_norm(x1, y2, norm_final, 0, tp).reshape(batch, seq, D_MODEL)
    y_sample = _final_norm(x1, y2, norm_final, tp, ts).reshape(db, nq, D_MODEL)
    win = min(WINDOW, seq)
    kv5 = lambda a, b, s: a.reshape(1, b, s, N_KV_HEADS, HEAD_DIM)
    k0p = k0[:tp].reshape(batch, seq, KV_DIM)[:, seq - win:]
    v0p = v0[:tp].reshape(batch, seq, KV_DIM)[:, seq - win:]
    return (y_prompt, y_sample, kv5(k0p, batch, win), kv5(v0p, batch, win),
            kv5(swa_ks, db, swa_ks.shape[1]), kv5(swa_vs, db, swa_vs.shape[1]),
            kv5(k1[:tp], batch, seq), kv5(v1[:tp], batch, seq), kv5(k1[tp:], db, nq), kv5(v1[tp:], db, nq))
```

```python
import functools

import jax
import jax.numpy as jnp
from jax import lax
from jax.experimental import pallas as pl
from jax.experimental.pallas import tpu as pltpu

D_MODEL = 1024
N_HEADS = 16
N_KV_HEADS = 4
HEAD_DIM = 64
GROUP = N_HEADS // N_KV_HEADS
Q_DIM = N_HEADS * HEAD_DIM
KV_DIM = N_KV_HEADS * HEAD_DIM
QKV_DIM = Q_DIM + 2 * KV_DIM
ROPE_THETA = 10000.0
WINDOW = 128
MOBA_BLOCK = 256
MOBA_TOPK = 3
PAGE_SIZE = 128
N_GROUPS = 8
EXPERTS_PER_GROUP = 8
N_EXPERTS = N_GROUPS * EXPERTS_PER_GROUP
EXPERT_TOPK = 2
EXPERT_FF = D_MODEL // 4
RMS_EPS = 1e-6
NEG_INF = -1e30
BELOW_NEG_INF = -3e38
LOG2_E = 1.4426950408889634
VT_ROWS = HEAD_DIM + 16

LANES = 128
GROUP_LANES = GROUP * HEAD_DIM
ROW_TILE = 256
EXPERT_ROWS = 256
PAGES_PER_STEP = 16
MOBA_BLOCKS_PER_TRIP = 4
PAGE_SLOTS = 8
VMEM_LIMIT = 56 * 1024 * 1024

F32 = jnp.float32
BF16 = jnp.bfloat16
NT_DIMS = (((1,), (1,)), ((), ()))


def _params(*sem):
    return pltpu.CompilerParams(dimension_semantics=sem, vmem_limit_bytes=VMEM_LIMIT)


def _rms(x, g):
    return x * lax.rsqrt(jnp.mean(x * x, axis=-1, keepdims=True) + RMS_EPS) * g


TOKEN_TILE = D_MODEL // LANES


def _load_token_tiles(ref, rows):
    return jnp.concatenate([ref[pl.ds(c, rows, stride=TOKEN_TILE), :] for c in range(TOKEN_TILE)], axis=1)


def _store_token_tiles(ref, x):
    for c in range(TOKEN_TILE):
        ref[pl.ds(c, x.shape[0], stride=TOKEN_TILE), :] = x[:, c * LANES:(c + 1) * LANES]


def _replicate_head(x128, odd):
    swapped = pltpu.roll(x128, HEAD_DIM, axis=1)
    low = lax.broadcasted_iota(jnp.int32, x128.shape, 1) < HEAD_DIM
    return jnp.where(low != odd, x128, swapped)


def _replicate_kv(x256, kvh):
    col = x256[:, (kvh // 2) * LANES:(kvh // 2 + 1) * LANES]
    rep = _replicate_head(col, kvh % 2 == 1)
    return jnp.concatenate([rep, rep], axis=1)


def _expand_heads(q):
    head = lax.shift_right_logical(lax.broadcasted_iota(jnp.int32, q.shape, 1), 6)
    return jnp.concatenate([jnp.where(head == h, q, 0.0) for h in range(GROUP)], axis=0)


def _collapse_heads(o_rep, rows):
    head = lax.shift_right_logical(lax.broadcasted_iota(jnp.int32, (rows, GROUP_LANES), 1), 6)
    out = jnp.zeros((rows, GROUP_LANES), F32)
    for h in range(GROUP):
        out = jnp.where(head == h, o_rep[h * rows:(h + 1) * rows], out)
    return out


def _top3_mask(gate, n_valid):
    lane = lax.broadcasted_iota(jnp.int32, gate.shape, 1)
    lane_f = lane.astype(F32)
    valid = lane < n_valid
    g = jnp.where(valid, gate, NEG_INF)
    sel = jnp.zeros(gate.shape, jnp.bool_)
    for _ in range(MOBA_TOPK):
        mx = jnp.max(g, axis=1, keepdims=True)
        first = jnp.min(jnp.where(g == mx, lane_f, 1e9), axis=1, keepdims=True)
        hit = lane_f == first
        sel = jnp.logical_or(sel, jnp.logical_and(hit, valid))
        g = jnp.where(hit, BELOW_NEG_INF, g)
    return sel


def _qkv_kernel(n_add, *refs):
    refs = list(refs)
    x_ref = refs.pop(0)
    add_refs = [refs.pop(0) for _ in range(n_add)]
    g_ref, w_ref, b_ref, cos_ref, sin_ref = refs[:5]
    outs = refs[5:]
    x = x_ref[...]
    for r in add_refs:
        x = x + _load_token_tiles(r, ROW_TILE)
    if n_add:
        outs.pop(0)[...] = x
    q_ref, k_ref, v_ref, krep_ref = outs[:4]
    h = _rms(x, g_ref[...]).astype(BF16)
    y = jnp.dot(h, w_ref[...], preferred_element_type=F32) + b_ref[...]
    cos = cos_ref[...]
    sin = sin_ref[...]
    low = lax.bitwise_and(lax.broadcasted_iota(jnp.int32, cos.shape, 1), HEAD_DIM - 1) < HEAD_DIM // 2

    def rope(slab):
        partner = jnp.where(low, pltpu.roll(slab, LANES - HEAD_DIM // 2, axis=1),
                            pltpu.roll(slab, HEAD_DIM // 2, axis=1))
        return slab * cos + partner * sin

    scale = HEAD_DIM ** -0.5 * LOG2_E
    for c in range(Q_DIM // LANES):
        q_ref[:, c * LANES:(c + 1) * LANES] = rope(y[:, c * LANES:(c + 1) * LANES]) * scale
    k = jnp.concatenate([rope(y[:, Q_DIM + c * LANES:Q_DIM + (c + 1) * LANES])
                         for c in range(KV_DIM // LANES)], axis=1)
    v = y[:, Q_DIM + KV_DIM:]
    k_ref[...] = k
    v_ref[...] = v
    for kvh in range(N_KV_HEADS):
        krep_ref[kvh] = _replicate_kv(k, kvh).astype(BF16)
    kmean_ref, qt_ref, vt_ref = outs[4:]
    kmean_ref[0] = jnp.broadcast_to(jnp.mean(k, axis=0, keepdims=True), (8, KV_DIM))
    qt_ref[...] = q_ref[...].T
    vt = v.T.reshape(N_KV_HEADS, HEAD_DIM, ROW_TILE).astype(BF16)
    vt_ref[:, 0] = jnp.concatenate([vt, jnp.ones((N_KV_HEADS, VT_ROWS - HEAD_DIM, ROW_TILE), BF16)], axis=1)


def _qkv_rope(x, adds, g, w, b, cos_t, sin_t):
    t = x.shape[0]
    nt = t // ROW_TILE
    row = lambda i: (i, 0)
    fixed = lambda i: (0, 0)
    in_specs = [pl.BlockSpec((ROW_TILE, D_MODEL), row)]
    args = [x]
    for arr, off in adds:
        in_specs.append(pl.BlockSpec((ROW_TILE * TOKEN_TILE, LANES),
                                     functools.partial(lambda o, i: (i + o, 0), off // ROW_TILE)))
        args.append(arr)
    in_specs += [pl.BlockSpec((1, D_MODEL), fixed), pl.BlockSpec((D_MODEL, QKV_DIM), fixed),
                 pl.BlockSpec((1, QKV_DIM), fixed), pl.BlockSpec((ROW_TILE, LANES), row),
                 pl.BlockSpec((ROW_TILE, LANES), row)]
    args += [g.reshape(1, D_MODEL), w.astype(BF16), b.reshape(1, QKV_DIM), cos_t, sin_t]
    out_shape, out_specs = [], []
    if adds:
        out_shape.append(jax.ShapeDtypeStruct((t, D_MODEL), F32))
        out_specs.append(pl.BlockSpec((ROW_TILE, D_MODEL), row))
    rep_shape = jax.ShapeDtypeStruct((N_KV_HEADS, t, GROUP_LANES), BF16)
    rep_spec = pl.BlockSpec((N_KV_HEADS, ROW_TILE, GROUP_LANES), lambda i: (0, i, 0))
    out_shape += [jax.ShapeDtypeStruct((t, Q_DIM), F32), jax.ShapeDtypeStruct((t, KV_DIM), F32),
                  jax.ShapeDtypeStruct((t, KV_DIM), F32), rep_shape]
    out_specs += [pl.BlockSpec((ROW_TILE, Q_DIM), row), pl.BlockSpec((ROW_TILE, KV_DIM), row),
                  pl.BlockSpec((ROW_TILE, KV_DIM), row), rep_spec]
    out_shape += [jax.ShapeDtypeStruct((nt, 8, KV_DIM), F32), jax.ShapeDtypeStruct((Q_DIM, t), F32),
                  jax.ShapeDtypeStruct((N_KV_HEADS, nt, VT_ROWS, ROW_TILE), BF16)]
    out_specs += [pl.BlockSpec((1, 8, KV_DIM), lambda i: (i, 0, 0)),
                  pl.BlockSpec((Q_DIM, ROW_TILE), lambda i: (0, i)),
                  pl.BlockSpec((N_KV_HEADS, 1, VT_ROWS, ROW_TILE), lambda i: (0, i, 0, 0))]
    return pl.pallas_call(
        functools.partial(_qkv_kernel, len(adds)),
        grid=(nt,), in_specs=in_specs, out_specs=out_specs, out_shape=out_shape,
        compiler_params=_params("arbitrary"), name="qkv_rope")(*args)


def _sink_softmax_pv(s, ok, sink_col, v):
    s = jnp.where(ok, s, NEG_INF)
    m = jnp.maximum(jnp.max(s, axis=1, keepdims=True), sink_col)
    p = jnp.exp2(s - m)
    denom = jnp.sum(p, axis=1, keepdims=True) + jnp.exp2(sink_col - m)
    return jnp.dot(p.astype(BF16), v, preferred_element_type=F32) / denom


def _sink_column(sinks_ref, kvh, rows):
    blk = lax.shift_right_logical(lax.broadcasted_iota(jnp.int32, (GROUP * rows, 1), 0), rows.bit_length() - 1)
    col = jnp.zeros((GROUP * rows, 1), F32)
    for h in range(GROUP):
        col = jnp.where(blk == h, sinks_ref[kvh * GROUP + h] * LOG2_E, col)
    return col


def _swa_prompt_kernel(sinks_ref, qt_ref, kprev_ref, kcur_ref, vprev_ref, vcur_ref, o_ref):
    i = pl.program_id(1)
    tile = ROW_TILE
    n_keys = WINDOW + tile
    key = lax.broadcasted_iota(jnp.int32, (n_keys, tile), 0)
    qry = lax.broadcasted_iota(jnp.int32, (n_keys, tile), 1)
    first_key = jnp.where(i > 0, 0, WINDOW)
    ok = (key > qry) & (key <= qry + WINDOW) & (key >= first_key)
    slabs = []
    for kvh in range(N_KV_HEADS):
        k = jnp.concatenate([kprev_ref[kvh, :, :2 * HEAD_DIM], kcur_ref[kvh, :, :2 * HEAD_DIM]], axis=0)
        vt = jnp.concatenate([vprev_ref[kvh, 0], vcur_ref[kvh, 0]], axis=1)
        outs = []
        for g in range(GROUP):
            h = kvh * GROUP + g
            qh = jnp.concatenate([qt_ref[h * HEAD_DIM:(h + 1) * HEAD_DIM, :].astype(BF16),
                                  jnp.zeros((HEAD_DIM, tile), BF16)], axis=0)
            s = jnp.where(ok, jnp.dot(k, qh, preferred_element_type=F32), NEG_INF)
            sink = sinks_ref[h] * LOG2_E
            m = jnp.maximum(jnp.max(s, axis=0, keepdims=True), sink)
            pv = jnp.dot(vt, jnp.exp2(s - m).astype(BF16), preferred_element_type=F32)
            denom = pv[HEAD_DIM:HEAD_DIM + 1] + jnp.exp2(sink - m)
            outs.append(pv[:HEAD_DIM] / denom)
        slabs.append(jnp.concatenate(outs, axis=0).T.astype(BF16))
    o_ref[...] = jnp.concatenate(slabs, axis=1)


def _swa_prompt(qt, k_rep, vt, sinks, batch, seq):
    nt = seq // ROW_TILE
    per_tile = ROW_TILE // WINDOW
    return pl.pallas_call(
        _swa_prompt_kernel,
        grid=(batch, nt),
        in_specs=[pl.BlockSpec(memory_space=pltpu.SMEM),
                  pl.BlockSpec((Q_DIM, ROW_TILE), lambda b, i: (0, b * nt + i)),
                  pl.BlockSpec((N_KV_HEADS, WINDOW, GROUP_LANES),
                               lambda b, i: (0, jnp.maximum((b * nt + i) * per_tile - 1, 0), 0)),
                  pl.BlockSpec((N_KV_HEADS, ROW_TILE, GROUP_LANES), lambda b, i: (0, b * nt + i, 0)),
                  pl.BlockSpec((N_KV_HEADS, 1, VT_ROWS, WINDOW),
                               lambda b, i: (0, jnp.maximum(b * nt + i - 1, 0), 0, per_tile - 1)),
                  pl.BlockSpec((N_KV_HEADS, 1, VT_ROWS, ROW_TILE), lambda b, i: (0, b * nt + i, 0, 0))],
        out_specs=pl.BlockSpec((ROW_TILE, Q_DIM), lambda b, i: (b * nt + i, 0)),
        out_shape=jax.ShapeDtypeStruct((batch * seq, Q_DIM), BF16),
        compiler_params=_params("arbitrary", "arbitrary"), name="swa_prompt",
    )(sinks, qt, k_rep, k_rep, vt, vt)


def _swa_sample_kernel(seqs, nq, sinks_ref, q_ref, kn_ref, vn_ref, ck_ref, cv_ref, o_ref, ko_ref, vo_ref):
    win = ck_ref.shape[1]

    def one_seq(s, carry):
        r0 = pl.multiple_of(s * nq, nq)
        q = q_ref[pl.ds(r0, nq), :]
        kn = kn_ref[pl.ds(r0, nq), :]
        vn = vn_ref[pl.ds(r0, nq), :]
        ck = ck_ref[s]
        cv = cv_ref[s]
        ko_ref[s] = jnp.concatenate([ck[nq:], kn], axis=0)
        vo_ref[s] = jnp.concatenate([cv[nq:], vn], axis=0)
        kk = jnp.concatenate([ck, kn], axis=0)
        vv = jnp.concatenate([cv, vn], axis=0)
        outs = []
        for kvh in range(N_KV_HEADS):
            qexp = _expand_heads(q[:, kvh * GROUP_LANES:(kvh + 1) * GROUP_LANES]).astype(BF16)
            k = _replicate_kv(kk, kvh).astype(BF16)
            v = _replicate_kv(vv, kvh).astype(BF16)
            sc = lax.dot_general(qexp, k, NT_DIMS, preferred_element_type=F32)
            qi = lax.bitwise_and(lax.broadcasted_iota(jnp.int32, sc.shape, 0), nq - 1)
            kj = lax.broadcasted_iota(jnp.int32, sc.shape, 1)
            ok = ((kj < win) & (kj > qi + win - WINDOW)) | ((kj >= win) & (kj - win <= qi))
            o_rep = _sink_softmax_pv(sc, ok, _sink_column(sinks_ref, kvh, nq), v)
            outs.append(_collapse_heads(o_rep, nq))
        o_ref[pl.ds(r0, nq), :] = jnp.concatenate(outs, axis=1).astype(BF16)
        return carry

    lax.fori_loop(0, seqs, one_seq, 0)


def _swa_sample(q, k, v, cache_k, cache_v, sinks, row0, nq):
    db, win = cache_k.shape[0], cache_k.shape[1]
    seqs = 8
    rows = seqs * nq
    tok = lambda g: (row0 // rows + g, 0)
    cache = lambda g: (g, 0, 0)
    return pl.pallas_call(
        functools.partial(_swa_sample_kernel, seqs, nq),
        grid=(db // seqs,),
        in_specs=[pl.BlockSpec(memory_space=pltpu.SMEM),
                  pl.BlockSpec((rows, Q_DIM), tok), pl.BlockSpec((rows, KV_DIM), tok),
                  pl.BlockSpec((rows, KV_DIM), tok),
                  pl.BlockSpec((seqs, win, KV_DIM), cache), pl.BlockSpec((seqs, win, KV_DIM), cache)],
        out_specs=[pl.BlockSpec((rows, Q_DIM), lambda g: (g, 0)),
                   pl.BlockSpec((seqs, win, KV_DIM), cache), pl.BlockSpec((seqs, win, KV_DIM), cache)],
        out_shape=[jax.ShapeDtypeStruct((db * nq, Q_DIM), BF16),
                   jax.ShapeDtypeStruct((db, win, KV_DIM), F32), jax.ShapeDtypeStruct((db, win, KV_DIM), F32)],
        compiler_params=_params("arbitrary"), name="swa_sample",
    )(sinks, q, k, v, cache_k.reshape(db, win, KV_DIM), cache_v.reshape(db, win, KV_DIM))


def _top3_rows(gate, n_valid):
    row = lax.broadcasted_iota(jnp.int32, gate.shape, 0)
    row_f = row.astype(F32)
    valid = row < n_valid
    g = jnp.where(valid, gate, NEG_INF)
    sel = jnp.zeros(gate.shape, F32)
    for _ in range(MOBA_TOPK):
        mx = jnp.max(g, axis=0, keepdims=True)
        first = jnp.min(jnp.where(g == mx, row_f, 1e9), axis=0, keepdims=True)
        hit = row_f == first
        sel = jnp.where(jnp.logical_and(hit, valid), 1.0, sel)
        g = jnp.where(hit, BELOW_NEG_INF, g)
    return sel


def _moba_prompt_kernel(qt_ref, k_ref, vt_ref, kmean_ref, o_ref, sel_ref, acc_ref):
    i = pl.program_id(2)
    qt = qt_ref[...]
    row_head = lax.shift_right_logical(lax.broadcasted_iota(jnp.int32, qt.shape, 0), 6)
    qexp32 = jnp.concatenate([jnp.where(row_head == h, qt, 0.0) for h in range(GROUP)], axis=1)
    qexp = qexp32.astype(BF16)
    gate = jnp.dot(kmean_ref[0, 0], qexp32, preferred_element_type=F32, precision=lax.Precision.HIGHEST)
    sel_ref[...] = _top3_rows(gate, i)

    q_heads = [jnp.concatenate([qexp[h * HEAD_DIM:(h + 1) * HEAD_DIM, h * MOBA_BLOCK:(h + 1) * MOBA_BLOCK],
                                jnp.zeros((HEAD_DIM, MOBA_BLOCK), BF16)], axis=0) for h in range(GROUP)]
    heads = [slice(h * MOBA_BLOCK, (h + 1) * MOBA_BLOCK) for h in range(GROUP)]

    def head_scores(j):
        start = pl.multiple_of(j * MOBA_BLOCK, MOBA_BLOCK)
        kj = k_ref[0, pl.ds(start, MOBA_BLOCK), :2 * HEAD_DIM]
        return [jnp.dot(kj, q_heads[h], preferred_element_type=F32) for h in range(GROUP)]

    m = []
    for h, s in enumerate(head_scores(i)):
        key = lax.broadcasted_iota(jnp.int32, s.shape, 0)
        qry = lax.broadcasted_iota(jnp.int32, s.shape, 1)
        s = jnp.where(key <= qry, s, NEG_INF)
        m.append(jnp.max(s, axis=0, keepdims=True))
        acc_ref[:, heads[h]] = jnp.dot(vt_ref[0, i], jnp.exp2(s - m[h]).astype(BF16), preferred_element_type=F32)

    def past_block(j, m_old):
        vtj = vt_ref[0, j]
        m_out = []
        for h, s in enumerate(head_scores(j)):
            chosen = sel_ref[pl.ds(j, 1), heads[h]] > 0.0
            m_new = jnp.where(chosen, jnp.maximum(m_old[h], jnp.max(s, axis=0, keepdims=True)), m_old[h])
            alpha = jnp.exp2(m_old[h] - m_new)
            p = jnp.exp2(s - jnp.where(chosen, m_new, -NEG_INF))
            acc_ref[:, heads[h]] = alpha * acc_ref[:, heads[h]] + jnp.dot(vtj, p.astype(BF16),
                                                                         preferred_element_type=F32)
            m_out.append(m_new)
        return tuple(m_out)

    def past_group(t, m_old):
        for u in range(MOBA_BLOCKS_PER_TRIP):
            m_old = past_block(MOBA_BLOCKS_PER_TRIP * t + u, m_old)
        return m_old

    lax.fori_loop(0, (i + MOBA_BLOCKS_PER_TRIP - 1) // MOBA_BLOCKS_PER_TRIP, past_group, tuple(m))
    ot = acc_ref[:HEAD_DIM] / acc_ref[HEAD_DIM:HEAD_DIM + 1]
    ot = jnp.concatenate([ot[:, h * MOBA_BLOCK:(h + 1) * MOBA_BLOCK] for h in range(GROUP)], axis=0)
    o_ref[...] = ot.T.astype(BF16)


def _moba_prompt(qt, k_rep, vt, kmean_rep, batch, seq):
    nb = seq // MOBA_BLOCK
    cols = GROUP * MOBA_BLOCK
    return pl.pallas_call(
        _moba_prompt_kernel,
        grid=(batch, N_KV_HEADS, nb),
        in_specs=[pl.BlockSpec((GROUP_LANES, MOBA_BLOCK), lambda b, h, i: (h, b * nb + i)),
                  pl.BlockSpec((1, seq, GROUP_LANES), lambda b, h, i: (h, b, 0)),
                  pl.BlockSpec((1, nb, VT_ROWS, MOBA_BLOCK), lambda b, h, i: (h, b, 0, 0)),
                  pl.BlockSpec((1, 1, nb, GROUP_LANES), lambda b, h, i: (b, h, 0, 0))],
        out_specs=pl.BlockSpec((MOBA_BLOCK, GROUP_LANES), lambda b, h, i: (b * nb + i, h)),
        out_shape=jax.ShapeDtypeStruct((batch * seq, Q_DIM), BF16),
        scratch_shapes=[pltpu.VMEM((nb, cols), F32), pltpu.VMEM((VT_ROWS, cols), F32)],
        compiler_params=_params("arbitrary", "arbitrary", "arbitrary"), name="moba_prompt",
    )(qt, k_rep, vt, kmean_rep)


def _moba_sample_kernel(nq, n_chunks, n_seq, pt_ref, qexp_ref, kn_ref, vn_ref, expand_ref, pk_hbm, pv_hbm,
                        o_ref, cbuf, s_ref, sem):
    b = pl.program_id(0)
    pages = PAGES_PER_STEP
    keys = pages * PAGE_SIZE
    blocks_per_chunk = keys // MOBA_BLOCK
    n_blocks = n_chunks * blocks_per_chunk
    n_stream = 2 * n_chunks
    ahead = PAGE_SLOTS - 1
    assert n_stream % PAGE_SLOTS == 0 and ahead <= n_stream

    def chunk_copies(seq, j, lookup):
        pool = pk_hbm if j < n_chunks else pv_hbm
        first = (j % n_chunks) * pages
        return [pltpu.make_async_copy(pool.at[pt_ref[seq, first + p] if lookup else 0],
                                      cbuf.at[j % PAGE_SLOTS, p], sem.at[j % PAGE_SLOTS])
                for p in range(pages)]

    def chunk(j):
        return jnp.concatenate([cbuf[j % PAGE_SLOTS, p] for p in range(pages)], axis=1)

    def start(seq, j):
        for p, cp in enumerate(chunk_copies(seq, j, True)):
            cp.start(priority=p % 2)

    @pl.when(b == 0)
    def _():
        for j in range(ahead):
            start(b, j)

    qexp32 = qexp_ref[0]
    qexp = qexp32.astype(BF16)
    blk_lane = lax.broadcasted_iota(jnp.int32, (KV_DIM, LANES), 1)
    ksum = jnp.zeros((KV_DIM, LANES), F32)
    for j in range(n_stream):
        if j + ahead < n_stream:
            start(b, j + ahead)
        else:
            @pl.when(b + 1 < n_seq)
            def _():
                start(b + 1, j + ahead - n_stream)
        for cp in chunk_copies(b, j, False):
            cp.wait()
        cols = slice((j % n_chunks) * keys, (j % n_chunks + 1) * keys)
        if j < n_chunks:
            kt = chunk(j)
            s_ref[:, cols] = jnp.dot(qexp, kt.astype(BF16), preferred_element_type=F32)
            for n in range(blocks_per_chunk):
                blk_sum = jnp.sum(kt[:, n * MOBA_BLOCK:(n + 1) * MOBA_BLOCK], axis=1, keepdims=True)
                ksum = jnp.where(blk_lane == j * blocks_per_chunk + n, blk_sum, ksum)
        if j == n_chunks - 1:
            gate = jnp.dot(qexp32, ksum * (1.0 / MOBA_BLOCK), preferred_element_type=F32,
                           precision=lax.Precision.HIGHEST)
            sel_bias = jnp.where(_top3_mask(gate, n_blocks), 0.0, NEG_INF).astype(BF16)
            s_own = lax.dot_general(qexp, kn_ref[...].astype(BF16), NT_DIMS, preferred_element_type=F32)
            qi = lax.bitwise_and(lax.broadcasted_iota(jnp.int32, s_own.shape, 0), nq - 1)
            kj = lax.broadcasted_iota(jnp.int32, s_own.shape, 1)
            s_own = jnp.where(kj <= qi, s_own, NEG_INF)
            m_tile = jnp.full((N_HEADS * nq, MOBA_BLOCK), NEG_INF, F32)
            for c in range(n_chunks):
                cc = slice(c * keys, (c + 1) * keys)
                sm = s_ref[:, cc] + jnp.dot(sel_bias, expand_ref[:, cc], preferred_element_type=F32)
                s_ref[:, cc] = sm
                for n in range(blocks_per_chunk):
                    m_tile = jnp.maximum(m_tile, sm[:, n * MOBA_BLOCK:(n + 1) * MOBA_BLOCK])
            m = jnp.maximum(jnp.max(m_tile, axis=1, keepdims=True), jnp.max(s_own, axis=1, keepdims=True))
            p_own = jnp.exp2(s_own - m)
            l_own = jnp.sum(p_own, axis=1, keepdims=True)
            acc = jnp.dot(p_own.astype(BF16), vn_ref[...].astype(BF16), preferred_element_type=F32)
            l_tile = jnp.zeros((N_HEADS * nq, MOBA_BLOCK), F32)
        if j >= n_chunks:
            p = jnp.exp2(s_ref[:, cols] - m)
            for n in range(blocks_per_chunk):
                l_tile = l_tile + p[:, n * MOBA_BLOCK:(n + 1) * MOBA_BLOCK]
            acc = acc + lax.dot_general(p.astype(BF16), chunk(j).astype(BF16), NT_DIMS,
                                        preferred_element_type=F32)
    o_ref[0] = acc / (l_own + jnp.sum(l_tile, axis=1, keepdims=True))


def _moba_sample(qexp, k, v, pool_k, pool_v, page_table, row0, nq):
    db, n_pages = page_table.shape
    n_pool = pool_k.shape[0]
    pages = PAGES_PER_STEP
    n_chunks = n_pages // pages
    rows = N_HEADS * nq
    pk = pool_k.transpose(0, 2, 3, 1).reshape(n_pool, KV_DIM, PAGE_SIZE)
    pv = pool_v.transpose(0, 2, 3, 1).reshape(n_pool, KV_DIM, PAGE_SIZE)
    n_keys = n_pages * PAGE_SIZE
    expand = (jnp.arange(LANES)[:, None] == (jnp.arange(n_keys) // MOBA_BLOCK)[None, :]).astype(BF16)
    new = lambda b, pt: (row0 // nq + b, 0)
    grid_spec = pltpu.PrefetchScalarGridSpec(
        num_scalar_prefetch=1,
        grid=(db,),
        in_specs=[pl.BlockSpec((1, rows, KV_DIM), lambda b, pt: (b, 0, 0)),
                  pl.BlockSpec((nq, KV_DIM), new), pl.BlockSpec((nq, KV_DIM), new),
                  pl.BlockSpec((LANES, n_keys), lambda b, pt: (0, 0)),
                  pl.BlockSpec(memory_space=pl.ANY), pl.BlockSpec(memory_space=pl.ANY)],
        out_specs=pl.BlockSpec((1, rows, KV_DIM), lambda b, pt: (b, 0, 0)),
        scratch_shapes=[pltpu.VMEM((PAGE_SLOTS, pages, KV_DIM, PAGE_SIZE), F32),
                        pltpu.VMEM((rows, n_keys), F32),
                        pltpu.SemaphoreType.DMA((PAGE_SLOTS,))])
    return pl.pallas_call(
        functools.partial(_moba_sample_kernel, nq, n_chunks, db),
        grid_spec=grid_spec,
        out_shape=jax.ShapeDtypeStruct((db, rows, KV_DIM), F32),
        compiler_params=_params("arbitrary"), name="moba_sample",
    )(page_table, qexp, k, v, expand, pk, pv)


def _proj_route_kernel(o_ref, x_ref, wo_ref, bo_ref, g_ref, wr_ref, br_ref, x1_ref, h_ref, route_ref):
    x1 = x_ref[...] + jnp.dot(o_ref[...], wo_ref[...], preferred_element_type=F32) + bo_ref[...]
    x1_ref[...] = x1
    h = _rms(x1, g_ref[...])
    _store_token_tiles(h_ref, h)
    h_hi = h.astype(BF16)
    h_lo = (h - h_hi.astype(F32)).astype(BF16)
    lg = (jnp.dot(h_hi, wr_ref[0], preferred_element_type=F32) + jnp.dot(h_hi, wr_ref[1], preferred_element_type=F32)
          + jnp.dot(h_lo, wr_ref[0], preferred_element_type=F32) + br_ref[...])
    lane = lax.broadcasted_iota(jnp.int32, lg.shape, 1)
    lane_f = lane.astype(F32)
    is_group = lane < N_GROUPS
    gl = jnp.where(is_group, lg, NEG_INF)
    gmax = jnp.max(gl, axis=1, keepdims=True)
    grp = jnp.min(jnp.where(gl == gmax, lane_f, 1e9), axis=1, keepdims=True)
    g_w = 1.0 / jnp.sum(jnp.where(is_group, jnp.exp(gl - gmax), 0.0), axis=1, keepdims=True)
    lane_grp = lax.shift_right_logical(lane - N_GROUPS, 3).astype(F32)
    in_grp = (lane >= N_GROUPS) & (lane < N_GROUPS + N_EXPERTS) & (lane_grp == grp)
    el = jnp.where(in_grp, lg, NEG_INF)
    v1 = jnp.max(el, axis=1, keepdims=True)
    i1 = jnp.min(jnp.where(el == v1, lane_f, 1e9), axis=1, keepdims=True)
    el2 = jnp.where(lane_f == i1, BELOW_NEG_INF, el)
    v2 = jnp.max(el2, axis=1, keepdims=True)
    i2 = jnp.min(jnp.where(el2 == v2, lane_f, 1e9), axis=1, keepdims=True)
    e2w = jnp.exp(v2 - v1)
    w1 = g_w / (1.0 + e2w)
    w2 = g_w * e2w / (1.0 + e2w)
    route = jnp.where(lane == 0, i1 - N_GROUPS, 0.0)
    route = jnp.where(lane == 1, i2 - N_GROUPS, route)
    route = jnp.where(lane == 2, w1, route)
    route = jnp.where(lane == 3, w2, route)
    route_ref[...] = route


def _proj_route(o, x, wo, bo, g, w_gr, b_gr, w_er, b_er):
    t = x.shape[0]
    row = lambda i: (i, 0)
    fixed = lambda i: (0, 0)
    pad = LANES - N_GROUPS - N_EXPERTS
    wr = jnp.concatenate([w_gr, w_er, jnp.zeros((D_MODEL, pad), F32)], axis=1)
    wr_hi = wr.astype(BF16)
    wr = jnp.stack([wr_hi, (wr - wr_hi.astype(F32)).astype(BF16)])
    br = jnp.concatenate([b_gr, b_er, jnp.zeros((pad,), F32)]).reshape(1, LANES)
    return pl.pallas_call(
        _proj_route_kernel,
        grid=(t // ROW_TILE,),
        in_specs=[pl.BlockSpec((ROW_TILE, Q_DIM), row), pl.BlockSpec((ROW_TILE, D_MODEL), row),
                  pl.BlockSpec((Q_DIM, D_MODEL), fixed), pl.BlockSpec((1, D_MODEL), fixed),
                  pl.BlockSpec((1, D_MODEL), fixed), pl.BlockSpec((2, D_MODEL, LANES), lambda i: (0, 0, 0)),
                  pl.BlockSpec((1, LANES), fixed)],
        out_specs=[pl.BlockSpec((ROW_TILE, D_MODEL), row), pl.BlockSpec((ROW_TILE * TOKEN_TILE, LANES), row),
                   pl.BlockSpec((ROW_TILE, LANES), row)],
        out_shape=[jax.ShapeDtypeStruct((t, D_MODEL), F32), jax.ShapeDtypeStruct((t * TOKEN_TILE, LANES), F32),
                   jax.ShapeDtypeStruct((t, LANES), F32)],
        compiler_params=_params("arbitrary"), name="proj_route",
    )(o, x, wo.astype(BF16), bo.reshape(1, D_MODEL), g.reshape(1, D_MODEL), wr, br)


def _expert_kernel(n_steps, blk_e_ref, nvalid_ref, src_ref, src_next_ref, dst_ref, wrow_ref, h_hbm,
                   wg_ref, wu_ref, wd_ref, y_hbm, xbuf, ybuf, wg_bf, wu_bf, wd_bf, gsem, ssem):
    i = pl.program_id(0)
    slot = lax.rem(i, 2)
    rows = EXPERT_ROWS
    nvalid = nvalid_ref[i]
    nvalid_next = jnp.where(i + 1 < n_steps, nvalid_ref[jnp.minimum(i + 1, n_steps - 1)], 0)
    nvalid_prev2 = jnp.where(i >= 2, nvalid_ref[jnp.maximum(i - 2, 0)], 0)

    def tile(ref, first):
        return ref.at[pl.ds(pl.multiple_of(first, TOKEN_TILE), TOKEN_TILE)]

    def row_copy_in(idx_ref, r, s):
        return pltpu.make_async_copy(tile(h_hbm, idx_ref[0, r]), tile(xbuf.at[s], r * TOKEN_TILE), gsem.at[s])

    def row_copy_out(r, s):
        return pltpu.make_async_copy(tile(ybuf.at[s], r * TOKEN_TILE), tile(y_hbm, dst_ref[0, r]), ssem.at[s])

    def for_rows(fn, n=None):
        def body(g, carry):
            for u in range(8):
                r = g * 8 + u
                if n is None:
                    fn(r, u)
                else:
                    pl.when(r < n)(functools.partial(fn, r, u))
            return carry
        lax.fori_loop(0, rows // 8, body, 0)

    @pl.when((i == 0) & (nvalid > 0))
    def _():
        for_rows(lambda r, u: row_copy_in(src_ref, r, 0).start(priority=u % 2))

    @pl.when(nvalid_next > 0)
    def _():
        for_rows(lambda r, u: row_copy_in(src_next_ref, r, 1 - slot).start(priority=u % 2))

    @pl.when(nvalid_prev2 > 0)
    def _():
        for_rows(lambda r, u: row_copy_out(r, slot).wait(), nvalid_prev2)

    @pl.when(nvalid > 0)
    def _():
        e_changed = jnp.logical_or(i == 0, blk_e_ref[i] != blk_e_ref[jnp.maximum(i - 1, 0)])

        @pl.when(e_changed)
        def _():
            wg_bf[...] = wg_ref[0, 0].astype(BF16)
            wu_bf[...] = wu_ref[0, 0].astype(BF16)
            wd_bf[...] = wd_ref[0, 0].astype(BF16)

        for_rows(lambda r, u: row_copy_in(src_ref, r, slot).wait())
        x = _load_token_tiles(xbuf.at[slot], rows).astype(BF16)
        gate = jnp.dot(x, wg_bf[...], preferred_element_type=F32)
        up = jnp.dot(x, wu_bf[...], preferred_element_type=F32)
        hid = (gate * jax.nn.sigmoid(gate) * up).astype(BF16)
        y = jnp.dot(hid, wd_bf[...], preferred_element_type=F32)
        _store_token_tiles(ybuf.at[slot], y * wrow_ref[...])
        for_rows(lambda r, u: row_copy_out(r, slot).start(priority=u % 2), nvalid)


def _expert_ffn(h, e_idx, e_w, w_gate, w_up, w_down, layer):
    t = h.shape[0] // TOKEN_TILE
    a = t * EXPERT_TOPK
    rows = EXPERT_ROWS
    n_blk = -(-a // rows) + N_EXPERTS
    n_steps = n_blk + 2
    n_rows = n_steps * rows
    flat_e = e_idx.reshape(a)
    order = jnp.argsort(flat_e).astype(jnp.int32)
    counts = jnp.sum(flat_e[:, None] == jnp.arange(N_EXPERTS, dtype=jnp.int32)[None, :], axis=0, dtype=jnp.int32)
    padded = (counts + rows - 1) // rows * rows
    pad_end = jnp.cumsum(padded)
    pad_start = pad_end - padded
    start = jnp.cumsum(counts) - counts
    blk_first = jnp.arange(n_steps, dtype=jnp.int32) * rows
    blk_e = jnp.minimum(jnp.sum(blk_first[:, None] >= pad_end[None, :], axis=1), N_EXPERTS - 1).astype(jnp.int32)
    rank0 = blk_first - pad_start[blk_e]
    nvalid = jnp.clip(counts[blk_e] - rank0, 0, rows).astype(jnp.int32)
    within = jnp.arange(rows, dtype=jnp.int32)[None, :]
    valid = within < nvalid[:, None]
    assign = order[jnp.clip((start[blk_e] + rank0)[:, None] + within, 0, a - 1)]
    tok = assign // EXPERT_TOPK
    src = jnp.where(valid, tok, 0) * TOKEN_TILE
    dst = jnp.where(valid, (assign % EXPERT_TOPK) * t + tok, 0) * TOKEN_TILE
    wrow = jnp.where(valid, e_w.reshape(a)[assign], 0.0)
    idx_blk = lambda i, be, na: (i, 0, 0)
    idx_next = lambda i, be, na: (jnp.minimum(i + 1, n_steps - 1), 0, 0)
    w_in = lambda i, be, na: (layer, be[i], 0, 0)
    grid_spec = pltpu.PrefetchScalarGridSpec(
        num_scalar_prefetch=2,
        grid=(n_steps,),
        in_specs=[pl.BlockSpec((1, 1, rows), idx_blk, memory_space=pltpu.SMEM),
                  pl.BlockSpec((1, 1, rows), idx_next, memory_space=pltpu.SMEM),
                  pl.BlockSpec((1, 1, rows), idx_blk, memory_space=pltpu.SMEM),
                  pl.BlockSpec((rows, 1), lambda i, be, na: (i, 0)),
                  pl.BlockSpec(memory_space=pl.ANY),
                  pl.BlockSpec((1, 1, D_MODEL, EXPERT_FF), w_in), pl.BlockSpec((1, 1, D_MODEL, EXPERT_FF), w_in),
                  pl.BlockSpec((1, 1, EXPERT_FF, D_MODEL), w_in)],
        out_specs=pl.BlockSpec(memory_space=pl.ANY),
        scratch_shapes=[pltpu.VMEM((2, rows * TOKEN_TILE, LANES), F32), pltpu.VMEM((2, rows * TOKEN_TILE, LANES), F32),
                        pltpu.VMEM((D_MODEL, EXPERT_FF), BF16), pltpu.VMEM((D_MODEL, EXPERT_FF), BF16),
                        pltpu.VMEM((EXPERT_FF, D_MODEL), BF16),
                        pltpu.SemaphoreType.DMA((2,)), pltpu.SemaphoreType.DMA((2,))])

    def body(blk_e_ref, nvalid_ref, src_ref, src_next_ref, dst_ref, *rest):
        _expert_kernel(n_steps, blk_e_ref, nvalid_ref, src_ref.at[0], src_next_ref.at[0], dst_ref.at[0], *rest)

    return pl.pallas_call(
        body, grid_spec=grid_spec,
        out_shape=jax.ShapeDtypeStruct((EXPERT_TOPK * t * TOKEN_TILE, LANES), F32),
        compiler_params=_params("arbitrary"), name="expert_ffn",
    )(blk_e, nvalid, src.reshape(n_steps, 1, rows), src.reshape(n_steps, 1, rows),
      dst.reshape(n_steps, 1, rows), wrow.reshape(n_rows, 1), h, w_gate, w_up, w_down)


def _final_kernel(x_ref, y0_ref, y1_ref, g_ref, o_ref):
    y = _load_token_tiles(y0_ref, ROW_TILE) + _load_token_tiles(y1_ref, ROW_TILE)
    o_ref[...] = _rms(x_ref[...] + y, g_ref[...])


def _final_norm(x, y2, g, row0, n_rows):
    t = x.shape[0]
    off = row0 // ROW_TILE
    return pl.pallas_call(
        _final_kernel,
        grid=(n_rows // ROW_TILE,),
        in_specs=[pl.BlockSpec((ROW_TILE, D_MODEL), lambda i: (i + off, 0)),
                  pl.BlockSpec((ROW_TILE * TOKEN_TILE, LANES), lambda i: (i + off, 0)),
                  pl.BlockSpec((ROW_TILE * TOKEN_TILE, LANES), lambda i: (i + off + t // ROW_TILE, 0)),
                  pl.BlockSpec((1, D_MODEL), lambda i: (0, 0))],
        out_specs=pl.BlockSpec((ROW_TILE, D_MODEL), lambda i: (i, 0)),
        out_shape=jax.ShapeDtypeStruct((n_rows, D_MODEL), F32),
        compiler_params=_params("arbitrary"), name="final_norm",
    )(x, y2, y2, g.reshape(1, D_MODEL))


def _rope_tables(pos):
    half = HEAD_DIM // 2
    inv = ROPE_THETA ** (-jnp.arange(half, dtype=F32) / half)
    ang = pos.astype(F32)[:, None] * inv[None, :]
    cos = jnp.cos(ang)
    sin = jnp.sin(ang)
    reps = LANES // HEAD_DIM
    return jnp.tile(jnp.concatenate([cos, cos], axis=1), (1, reps)), jnp.tile(jnp.concatenate([-sin, sin], axis=1), (1, reps))


def _moe(o, x, wo, bo, g, w_gr, b_gr, w_er, b_er, w_gate, w_up, w_down, layer):
    x1, h, route = _proj_route(o, x, wo, bo, g, w_gr, b_gr, w_er, b_er)
    e_idx = jnp.clip(route[:, :EXPERT_TOPK].astype(jnp.int32), 0, N_EXPERTS - 1)
    e_w = route[:, EXPERT_TOPK:2 * EXPERT_TOPK]
    return x1, _expert_ffn(h, e_idx, e_w, w_gate, w_up, w_down, layer)


def kernel(x_prompt, x_sample, cache_swa_k, cache_swa_v, cache_moba_k, cache_moba_v, page_table, norm_mix, norm_ffn, norm_final, swa_w_qkv, swa_b_qkv, swa_sinks, swa_w_o, swa_b_o, moba_w_qkv, moba_w_o, w_group_router, b_group_router, w_expert_router, b_expert_router, w_gate, w_up, w_down):
    batch, seq, _ = x_prompt.shape
    db, nq, _ = x_sample.shape
    past_len = page_table.shape[1] * PAGE_SIZE
    tp = batch * seq
    ts = db * nq
    t = tp + ts
    x = jnp.concatenate([x_prompt.reshape(tp, D_MODEL), x_sample.reshape(ts, D_MODEL)], axis=0)
    pos = jnp.concatenate([jnp.tile(jnp.arange(seq), batch), jnp.tile(past_len + jnp.arange(nq), db)])
    cos_t, sin_t = _rope_tables(pos)

    q, k0, v0, k_rep, _, qt, vt = _qkv_rope(x, [], norm_mix[0], swa_w_qkv[0], swa_b_qkv[0], cos_t, sin_t)
    o_p = _swa_prompt(qt, k_rep, vt, swa_sinks[0], batch, seq)
    o_s, swa_ks, swa_vs = _swa_sample(q, k0, v0, cache_swa_k[0], cache_swa_v[0], swa_sinks[0], tp, nq)
    o = jnp.concatenate([o_p, o_s], axis=0)
    x1, y2 = _moe(o, x, swa_w_o[0], swa_b_o[0], norm_ffn[0], w_group_router[0], b_group_router[0],
                  w_expert_router[0], b_expert_router[0], w_gate, w_up, w_down, 0)

    x, q, k1, v1, k_rep, kmean, qt, vt = _qkv_rope(
        x1, [(y2, 0), (y2, t)], norm_mix[1], moba_w_qkv[0], jnp.zeros((QKV_DIM,), F32), cos_t, sin_t)
    nb = seq // MOBA_BLOCK
    kmean_rep = jnp.tile(kmean[:batch * nb, 0].reshape(batch, nb, N_KV_HEADS, 1, HEAD_DIM), (1, 1, 1, GROUP, 1))
    kmean_rep = kmean_rep.reshape(batch, nb, N_KV_HEADS, GROUP_LANES).transpose(0, 2, 1, 3)
    o_p = _moba_prompt(qt, k_rep, vt, kmean_rep, batch, seq)
    qs = q[tp:].reshape(db, nq, N_HEADS, 1, HEAD_DIM)
    slot = (jnp.arange(N_HEADS) // GROUP)[:, None] == jnp.arange(N_KV_HEADS)[None, :]
    qexp = jnp.where(slot[None, None, :, :, None], qs, 0.0).transpose(0, 2, 1, 3, 4).reshape(db, N_HEADS * nq, KV_DIM)
    o_s = _moba_sample(qexp, k1, v1, cache_moba_k[0], cache_moba_v[0], page_table, tp, nq)
    o_s = o_s.reshape(db, N_HEADS, nq, N_KV_HEADS, HEAD_DIM)
    o_s = jnp.sum(jnp.where(slot[None, :, None, :, None], o_s, 0.0), axis=3).transpose(0, 2, 1, 3)
    o = jnp.concatenate([o_p, o_s.reshape(ts, Q_DIM).astype(BF16)], axis=0)
    x1, y2 = _moe(o, x, moba_w_o[0], jnp.zeros((D_MODEL,), F32), norm_ffn[1], w_group_router[1],
                  b_group_router[1], w_expert_router[1], b_expert_router[1], w_gate, w_up, w_down, 1)

    y_prompt = _final_norm(x1, y2, norm_final, 0, tp).reshape(batch, seq, D_MODEL)
    y_sample = _final_norm(x1, y2, norm_final, tp, ts).reshape(db, nq, D_MODEL)
    win = min(WINDOW, seq)
    kv5 = lambda a, b, s: a.reshape(1, b, s, N_KV_HEADS, HEAD_DIM)
    k0p = k0[:tp].reshape(batch, seq, KV_DIM)[:, seq - win:]
    v0p = v0[:tp].reshape(batch, seq, KV_DIM)[:, seq - win:]
    return (y_prompt, y_sample, kv5(k0p, batch, win), kv5(v0p, batch, win),
            kv5(swa_ks, db, swa_ks.shape[1]), kv5(swa_vs, db, swa_vs.shape[1]),
            kv5(k1[:tp], batch, seq), kv5(v1[:tp], batch, seq), kv5(k1[tp:], db, nq), kv5(v1[tp:], db, nq))
```

```python
import functools

import jax
import jax.numpy as jnp
from jax import lax
from jax.experimental import pallas as pl
from jax.experimental.pallas import tpu as pltpu

D_MODEL = 1024
N_HEADS = 16
N_KV_HEADS = 4
HEAD_DIM = 64
GROUP = N_HEADS // N_KV_HEADS
Q_DIM = N_HEADS * HEAD_DIM
KV_DIM = N_KV_HEADS * HEAD_DIM
QKV_DIM = Q_DIM + 2 * KV_DIM
ROPE_THETA = 10000.0
WINDOW = 128
MOBA_BLOCK = 256
MOBA_TOPK = 3
PAGE_SIZE = 128
N_GROUPS = 8
EXPERTS_PER_GROUP = 8
N_EXPERTS = N_GROUPS * EXPERTS_PER_GROUP
EXPERT_TOPK = 2
EXPERT_FF = D_MODEL // 4
RMS_EPS = 1e-6
NEG_INF = -1e30
BELOW_NEG_INF = -3e38
LOG2_E = 1.4426950408889634
VT_ROWS = HEAD_DIM + 16

LANES = 128
GROUP_LANES = GROUP * HEAD_DIM
ROW_TILE = 256
EXPERT_ROWS = 256
PAGES_PER_STEP = 16
MOBA_BLOCKS_PER_TRIP = 4
PAGE_SLOTS = 8
VMEM_LIMIT = 56 * 1024 * 1024

F32 = jnp.float32
BF16 = jnp.bfloat16
NT_DIMS = (((1,), (1,)), ((), ()))


def _params(*sem):
    return pltpu.CompilerParams(dimension_semantics=sem, vmem_limit_bytes=VMEM_LIMIT)


def _rms(x, g):
    return x * lax.rsqrt(jnp.mean(x * x, axis=-1, keepdims=True) + RMS_EPS) * g


TOKEN_TILE = D_MODEL // LANES


def _load_token_tiles(ref, rows):
    return jnp.concatenate([ref[pl.ds(c, rows, stride=TOKEN_TILE), :] for c in range(TOKEN_TILE)], axis=1)


def _store_token_tiles(ref, x):
    for c in range(TOKEN_TILE):
        ref[pl.ds(c, x.shape[0], stride=TOKEN_TILE), :] = x[:, c * LANES:(c + 1) * LANES]


def _dot_split(a, b):
    a_hi = a.astype(BF16)
    a_lo = (a - a_hi.astype(F32)).astype(BF16)
    b_hi = b.astype(BF16)
    b_lo = (b - b_hi.astype(F32)).astype(BF16)
    return (jnp.dot(a_hi, b_hi, preferred_element_type=F32) + jnp.dot(a_hi, b_lo, preferred_element_type=F32)
            + jnp.dot(a_lo, b_hi, preferred_element_type=F32))


def _row_parts(x):
    if isinstance(x, tuple):
        return list(x), x[0].shape[0] // ROW_TILE
    return [x], 0


def _row_part_specs(width, n_first):
    if not n_first:
        return [pl.BlockSpec((ROW_TILE, width), lambda i: (i, 0))]
    return [pl.BlockSpec((ROW_TILE, width), lambda i: (jnp.minimum(i, n_first - 1), 0)),
            pl.BlockSpec((ROW_TILE, width), lambda i: (jnp.maximum(i - n_first, 0), 0))]


def _read_row_parts(refs, n_first):
    if not n_first:
        return refs.pop(0)[...]
    first, second = refs.pop(0), refs.pop(0)
    return jnp.where(pl.program_id(0) < n_first, first[...], second[...])


def _replicate_head(x128, odd):
    swapped = pltpu.roll(x128, HEAD_DIM, axis=1)
    low = lax.broadcasted_iota(jnp.int32, x128.shape, 1) < HEAD_DIM
    return jnp.where(low != odd, x128, swapped)


def _replicate_kv(x256, kvh):
    col = x256[:, (kvh // 2) * LANES:(kvh // 2 + 1) * LANES]
    rep = _replicate_head(col, kvh % 2 == 1)
    return jnp.concatenate([rep, rep], axis=1)


def _expand_heads(q):
    head = lax.shift_right_logical(lax.broadcasted_iota(jnp.int32, q.shape, 1), 6)
    return jnp.concatenate([jnp.where(head == h, q, 0.0) for h in range(GROUP)], axis=0)


def _collapse_heads(o_rep, rows):
    head = lax.shift_right_logical(lax.broadcasted_iota(jnp.int32, (rows, GROUP_LANES), 1), 6)
    out = jnp.zeros((rows, GROUP_LANES), F32)
    for h in range(GROUP):
        out = jnp.where(head == h, o_rep[h * rows:(h + 1) * rows], out)
    return out


def _top3_mask(gate, n_valid):
    lane = lax.broadcasted_iota(jnp.int32, gate.shape, 1)
    lane_f = lane.astype(F32)
    valid = lane < n_valid
    g = jnp.where(valid, gate, NEG_INF)
    sel = jnp.zeros(gate.shape, jnp.bool_)
    for _ in range(MOBA_TOPK):
        mx = jnp.max(g, axis=1, keepdims=True)
        first = jnp.min(jnp.where(g == mx, lane_f, 1e9), axis=1, keepdims=True)
        hit = lane_f == first
        sel = jnp.logical_or(sel, jnp.logical_and(hit, valid))
        g = jnp.where(hit, BELOW_NEG_INF, g)
    return sel


def _qkv_kernel(n_add, n_first, *refs):
    refs = list(refs)
    x = _read_row_parts(refs, n_first)
    add_refs = [refs.pop(0) for _ in range(n_add)]
    g_ref, w_ref, b_ref, cos_ref, sin_ref = refs[:5]
    outs = refs[5:]
    for r in add_refs:
        x = x + _load_token_tiles(r, ROW_TILE)
    if n_add:
        outs.pop(0)[...] = x
    q_ref, k_ref, v_ref, krep_ref = outs[:4]
    h = _rms(x, g_ref[...]).astype(BF16)
    y = jnp.dot(h, w_ref[...], preferred_element_type=F32) + b_ref[...]
    cos = cos_ref[...]
    sin = sin_ref[...]
    low = lax.bitwise_and(lax.broadcasted_iota(jnp.int32, cos.shape, 1), HEAD_DIM - 1) < HEAD_DIM // 2

    def rope(slab):
        partner = jnp.where(low, pltpu.roll(slab, LANES - HEAD_DIM // 2, axis=1),
                            pltpu.roll(slab, HEAD_DIM // 2, axis=1))
        return slab * cos + partner * sin

    scale = HEAD_DIM ** -0.5 * LOG2_E
    for c in range(Q_DIM // LANES):
        q_ref[:, c * LANES:(c + 1) * LANES] = rope(y[:, c * LANES:(c + 1) * LANES]) * scale
    k = jnp.concatenate([rope(y[:, Q_DIM + c * LANES:Q_DIM + (c + 1) * LANES])
                         for c in range(KV_DIM // LANES)], axis=1)
    v = y[:, Q_DIM + KV_DIM:]
    k_ref[...] = k
    v_ref[...] = v
    for kvh in range(N_KV_HEADS):
        krep_ref[kvh] = _replicate_kv(k, kvh).astype(BF16)
    kmean_ref, qt_ref, vt_ref = outs[4:]
    kmean_ref[0] = jnp.broadcast_to(jnp.mean(k, axis=0, keepdims=True), (8, KV_DIM))
    qt_ref[...] = q_ref[...].T
    vt = v.T.reshape(N_KV_HEADS, HEAD_DIM, ROW_TILE).astype(BF16)
    vt_ref[:, 0] = jnp.concatenate([vt, jnp.ones((N_KV_HEADS, VT_ROWS - HEAD_DIM, ROW_TILE), BF16)], axis=1)


def _qkv_rope(x, adds, g, w, b, cos_t, sin_t):
    args, n_first = _row_parts(x)
    t = sum(a.shape[0] for a in args)
    nt = t // ROW_TILE
    row = lambda i: (i, 0)
    fixed = lambda i: (0, 0)
    in_specs = _row_part_specs(D_MODEL, n_first)
    for arr, off in adds:
        in_specs.append(pl.BlockSpec((ROW_TILE * TOKEN_TILE, LANES),
                                     functools.partial(lambda o, i: (i + o, 0), off // ROW_TILE)))
        args.append(arr)
    in_specs += [pl.BlockSpec((1, D_MODEL), fixed), pl.BlockSpec((D_MODEL, QKV_DIM), fixed),
                 pl.BlockSpec((1, QKV_DIM), fixed), pl.BlockSpec((ROW_TILE, LANES), row),
                 pl.BlockSpec((ROW_TILE, LANES), row)]
    args += [g.reshape(1, D_MODEL), w.astype(BF16), b.reshape(1, QKV_DIM), cos_t, sin_t]
    out_shape, out_specs = [], []
    if adds:
        out_shape.append(jax.ShapeDtypeStruct((t, D_MODEL), F32))
        out_specs.append(pl.BlockSpec((ROW_TILE, D_MODEL), row))
    rep_shape = jax.ShapeDtypeStruct((N_KV_HEADS, t, GROUP_LANES), BF16)
    rep_spec = pl.BlockSpec((N_KV_HEADS, ROW_TILE, GROUP_LANES), lambda i: (0, i, 0))
    out_shape += [jax.ShapeDtypeStruct((t, Q_DIM), F32), jax.ShapeDtypeStruct((t, KV_DIM), F32),
                  jax.ShapeDtypeStruct((t, KV_DIM), F32), rep_shape]
    out_specs += [pl.BlockSpec((ROW_TILE, Q_DIM), row), pl.BlockSpec((ROW_TILE, KV_DIM), row),
                  pl.BlockSpec((ROW_TILE, KV_DIM), row), rep_spec]
    out_shape += [jax.ShapeDtypeStruct((nt, 8, KV_DIM), F32), jax.ShapeDtypeStruct((Q_DIM, t), F32),
                  jax.ShapeDtypeStruct((N_KV_HEADS, nt, VT_ROWS, ROW_TILE), BF16)]
    out_specs += [pl.BlockSpec((1, 8, KV_DIM), lambda i: (i, 0, 0)),
                  pl.BlockSpec((Q_DIM, ROW_TILE), lambda i: (0, i)),
                  pl.BlockSpec((N_KV_HEADS, 1, VT_ROWS, ROW_TILE), lambda i: (0, i, 0, 0))]
    return pl.pallas_call(
        functools.partial(_qkv_kernel, len(adds), n_first),
        grid=(nt,), in_specs=in_specs, out_specs=out_specs, out_shape=out_shape,
        compiler_params=_params("arbitrary"), name="qkv_rope")(*args)


def _sink_softmax_pv(s, ok, sink_col, v):
    s = jnp.where(ok, s, NEG_INF)
    m = jnp.maximum(jnp.max(s, axis=1, keepdims=True), sink_col)
    p = jnp.exp2(s - m)
    denom = jnp.sum(p, axis=1, keepdims=True) + jnp.exp2(sink_col - m)
    return jnp.dot(p.astype(BF16), v, preferred_element_type=F32) / denom


def _sink_column(sinks_ref, kvh, rows):
    blk = lax.shift_right_logical(lax.broadcasted_iota(jnp.int32, (GROUP * rows, 1), 0), rows.bit_length() - 1)
    col = jnp.zeros((GROUP * rows, 1), F32)
    for h in range(GROUP):
        col = jnp.where(blk == h, sinks_ref[kvh * GROUP + h] * LOG2_E, col)
    return col


def _swa_prompt_kernel(sinks_ref, qt_ref, kprev_ref, kcur_ref, vprev_ref, vcur_ref, o_ref):
    i = pl.program_id(1)
    tile = ROW_TILE
    n_keys = WINDOW + tile
    key = lax.broadcasted_iota(jnp.int32, (n_keys, tile), 0)
    qry = lax.broadcasted_iota(jnp.int32, (n_keys, tile), 1)
    first_key = jnp.where(i > 0, 0, WINDOW)
    ok = (key > qry) & (key <= qry + WINDOW) & (key >= first_key)
    slabs = []
    for kvh in range(N_KV_HEADS):
        k = jnp.concatenate([kprev_ref[kvh, :, :2 * HEAD_DIM], kcur_ref[kvh, :, :2 * HEAD_DIM]], axis=0)
        vt = jnp.concatenate([vprev_ref[kvh, 0], vcur_ref[kvh, 0]], axis=1)
        outs = []
        for g in range(GROUP):
            h = kvh * GROUP + g
            qh = jnp.concatenate([qt_ref[h * HEAD_DIM:(h + 1) * HEAD_DIM, :].astype(BF16),
                                  jnp.zeros((HEAD_DIM, tile), BF16)], axis=0)
            s = jnp.where(ok, jnp.dot(k, qh, preferred_element_type=F32), NEG_INF)
            sink = sinks_ref[h] * LOG2_E
            m = jnp.maximum(jnp.max(s, axis=0, keepdims=True), sink)
            pv = jnp.dot(vt, jnp.exp2(s - m).astype(BF16), preferred_element_type=F32)
            denom = pv[HEAD_DIM:HEAD_DIM + 1] + jnp.exp2(sink - m)
            outs.append(pv[:HEAD_DIM] / denom)
        slabs.append(jnp.concatenate(outs, axis=0).T.astype(BF16))
    o_ref[...] = jnp.concatenate(slabs, axis=1)


def _swa_prompt(qt, k_rep, vt, sinks, batch, seq):
    nt = seq // ROW_TILE
    per_tile = ROW_TILE // WINDOW
    return pl.pallas_call(
        _swa_prompt_kernel,
        grid=(batch, nt),
        in_specs=[pl.BlockSpec(memory_space=pltpu.SMEM),
                  pl.BlockSpec((Q_DIM, ROW_TILE), lambda b, i: (0, b * nt + i)),
                  pl.BlockSpec((N_KV_HEADS, WINDOW, GROUP_LANES),
                               lambda b, i: (0, jnp.maximum((b * nt + i) * per_tile - 1, 0), 0)),
                  pl.BlockSpec((N_KV_HEADS, ROW_TILE, GROUP_LANES), lambda b, i: (0, b * nt + i, 0)),
                  pl.BlockSpec((N_KV_HEADS, 1, VT_ROWS, WINDOW),
                               lambda b, i: (0, jnp.maximum(b * nt + i - 1, 0), 0, per_tile - 1)),
                  pl.BlockSpec((N_KV_HEADS, 1, VT_ROWS, ROW_TILE), lambda b, i: (0, b * nt + i, 0, 0))],
        out_specs=pl.BlockSpec((ROW_TILE, Q_DIM), lambda b, i: (b * nt + i, 0)),
        out_shape=jax.ShapeDtypeStruct((batch * seq, Q_DIM), BF16),
        compiler_params=_params("arbitrary", "arbitrary"), name="swa_prompt",
    )(sinks, qt, k_rep, k_rep, vt, vt)


def _swa_sample_kernel(seqs, nq, sinks_ref, q_ref, kn_ref, vn_ref, ck_ref, cv_ref, o_ref, ko_ref, vo_ref):
    win = ck_ref.shape[1]

    def one_seq(s, carry):
        r0 = pl.multiple_of(s * nq, nq)
        q = q_ref[pl.ds(r0, nq), :]
        kn = kn_ref[pl.ds(r0, nq), :]
        vn = vn_ref[pl.ds(r0, nq), :]
        ck = ck_ref[s]
        cv = cv_ref[s]
        ko_ref[s] = jnp.concatenate([ck[nq:], kn], axis=0)
        vo_ref[s] = jnp.concatenate([cv[nq:], vn], axis=0)
        kk = jnp.concatenate([ck, kn], axis=0)
        vv = jnp.concatenate([cv, vn], axis=0)
        outs = []
        for kvh in range(N_KV_HEADS):
            qexp = _expand_heads(q[:, kvh * GROUP_LANES:(kvh + 1) * GROUP_LANES]).astype(BF16)
            k = _replicate_kv(kk, kvh).astype(BF16)
            v = _replicate_kv(vv, kvh).astype(BF16)
            sc = lax.dot_general(qexp, k, NT_DIMS, preferred_element_type=F32)
            qi = lax.bitwise_and(lax.broadcasted_iota(jnp.int32, sc.shape, 0), nq - 1)
            kj = lax.broadcasted_iota(jnp.int32, sc.shape, 1)
            ok = ((kj < win) & (kj > qi + win - WINDOW)) | ((kj >= win) & (kj - win <= qi))
            o_rep = _sink_softmax_pv(sc, ok, _sink_column(sinks_ref, kvh, nq), v)
            outs.append(_collapse_heads(o_rep, nq))
        o_ref[pl.ds(r0, nq), :] = jnp.concatenate(outs, axis=1).astype(BF16)
        return carry

    lax.fori_loop(0, seqs, one_seq, 0)


def _swa_sample(q, k, v, cache_k, cache_v, sinks, row0, nq):
    db, win = cache_k.shape[0], cache_k.shape[1]
    seqs = 8
    rows = seqs * nq
    tok = lambda g: (row0 // rows + g, 0)
    cache = lambda g: (g, 0, 0)
    return pl.pallas_call(
        functools.partial(_swa_sample_kernel, seqs, nq),
        grid=(db // seqs,),
        in_specs=[pl.BlockSpec(memory_space=pltpu.SMEM),
                  pl.BlockSpec((rows, Q_DIM), tok), pl.BlockSpec((rows, KV_DIM), tok),
                  pl.BlockSpec((rows, KV_DIM), tok),
                  pl.BlockSpec((seqs, win, KV_DIM), cache), pl.BlockSpec((seqs, win, KV_DIM), cache)],
        out_specs=[pl.BlockSpec((rows, Q_DIM), lambda g: (g, 0)),
                   pl.BlockSpec((seqs, win, KV_DIM), cache), pl.BlockSpec((seqs, win, KV_DIM), cache)],
        out_shape=[jax.ShapeDtypeStruct((db * nq, Q_DIM), BF16),
                   jax.ShapeDtypeStruct((db, win, KV_DIM), F32), jax.ShapeDtypeStruct((db, win, KV_DIM), F32)],
        compiler_params=_params("arbitrary"), name="swa_sample",
    )(sinks, q, k, v, cache_k.reshape(db, win, KV_DIM), cache_v.reshape(db, win, KV_DIM))


def _top3_rows(gate, n_valid):
    row = lax.broadcasted_iota(jnp.int32, gate.shape, 0)
    row_f = row.astype(F32)
    valid = row < n_valid
    g = jnp.where(valid, gate, NEG_INF)
    sel = jnp.zeros(gate.shape, F32)
    for _ in range(MOBA_TOPK):
        mx = jnp.max(g, axis=0, keepdims=True)
        first = jnp.min(jnp.where(g == mx, row_f, 1e9), axis=0, keepdims=True)
        hit = row_f == first
        sel = jnp.where(jnp.logical_and(hit, valid), 1.0, sel)
        g = jnp.where(hit, BELOW_NEG_INF, g)
    return sel


def _moba_prompt_kernel(qt_ref, k_ref, vt_ref, kmean_ref, o_ref, sel_ref, acc_ref):
    i = pl.program_id(2)
    qt = qt_ref[...]
    row_head = lax.shift_right_logical(lax.broadcasted_iota(jnp.int32, qt.shape, 0), 6)
    qexp32 = jnp.concatenate([jnp.where(row_head == h, qt, 0.0) for h in range(GROUP)], axis=1)
    qexp = qexp32.astype(BF16)
    gate = _dot_split(kmean_ref[0, 0], qexp32)
    sel_ref[...] = _top3_rows(gate, i)

    q_heads = [jnp.concatenate([qexp[h * HEAD_DIM:(h + 1) * HEAD_DIM, h * MOBA_BLOCK:(h + 1) * MOBA_BLOCK],
                                jnp.zeros((HEAD_DIM, MOBA_BLOCK), BF16)], axis=0) for h in range(GROUP)]
    heads = [slice(h * MOBA_BLOCK, (h + 1) * MOBA_BLOCK) for h in range(GROUP)]

    def head_scores(j):
        start = pl.multiple_of(j * MOBA_BLOCK, MOBA_BLOCK)
        kj = k_ref[0, pl.ds(start, MOBA_BLOCK), :2 * HEAD_DIM]
        return [jnp.dot(kj, q_heads[h], preferred_element_type=F32) for h in range(GROUP)]

    m = []
    for h, s in enumerate(head_scores(i)):
        key = lax.broadcasted_iota(jnp.int32, s.shape, 0)
        qry = lax.broadcasted_iota(jnp.int32, s.shape, 1)
        s = jnp.where(key <= qry, s, NEG_INF)
        m.append(jnp.max(s, axis=0, keepdims=True))
        acc_ref[:, heads[h]] = jnp.dot(vt_ref[0, i], jnp.exp2(s - m[h]).astype(BF16), preferred_element_type=F32)

    def past_block(j, m_old):
        vtj = vt_ref[0, j]
        m_out = []
        for h, s in enumerate(head_scores(j)):
            chosen = sel_ref[pl.ds(j, 1), heads[h]] > 0.0
            m_new = jnp.where(chosen, jnp.maximum(m_old[h], jnp.max(s, axis=0, keepdims=True)), m_old[h])
            alpha = jnp.exp2(m_old[h] - m_new)
            p = jnp.exp2(s - jnp.where(chosen, m_new, -NEG_INF))
            acc_ref[:, heads[h]] = alpha * acc_ref[:, heads[h]] + jnp.dot(vtj, p.astype(BF16),
                                                                         preferred_element_type=F32)
            m_out.append(m_new)
        return tuple(m_out)

    def past_group(t, m_old):
        for u in range(MOBA_BLOCKS_PER_TRIP):
            m_old = past_block(MOBA_BLOCKS_PER_TRIP * t + u, m_old)
        return m_old

    lax.fori_loop(0, (i + MOBA_BLOCKS_PER_TRIP - 1) // MOBA_BLOCKS_PER_TRIP, past_group, tuple(m))
    ot = acc_ref[:HEAD_DIM] / acc_ref[HEAD_DIM:HEAD_DIM + 1]
    ot = jnp.concatenate([ot[:, h * MOBA_BLOCK:(h + 1) * MOBA_BLOCK] for h in range(GROUP)], axis=0)
    o_ref[...] = ot.T.astype(BF16)


def _moba_prompt(qt, k_rep, vt, kmean_rep, batch, seq):
    nb = seq // MOBA_BLOCK
    cols = GROUP * MOBA_BLOCK
    return pl.pallas_call(
        _moba_prompt_kernel,
        grid=(batch, N_KV_HEADS, nb),
        in_specs=[pl.BlockSpec((GROUP_LANES, MOBA_BLOCK), lambda b, h, i: (h, b * nb + i)),
                  pl.BlockSpec((1, seq, GROUP_LANES), lambda b, h, i: (h, b, 0)),
                  pl.BlockSpec((1, nb, VT_ROWS, MOBA_BLOCK), lambda b, h, i: (h, b, 0, 0)),
                  pl.BlockSpec((1, 1, nb, GROUP_LANES), lambda b, h, i: (b, h, 0, 0))],
        out_specs=pl.BlockSpec((MOBA_BLOCK, GROUP_LANES), lambda b, h, i: (b * nb + i, h)),
        out_shape=jax.ShapeDtypeStruct((batch * seq, Q_DIM), BF16),
        scratch_shapes=[pltpu.VMEM((nb, cols), F32), pltpu.VMEM((VT_ROWS, cols), F32)],
        compiler_params=_params("arbitrary", "arbitrary", "arbitrary"), name="moba_prompt",
    )(qt, k_rep, vt, kmean_rep)


def _moba_sample_kernel(nq, n_chunks, n_seq, pt_ref, qexp_ref, kn_ref, vn_ref, expand_ref, pk_hbm, pv_hbm,
                        o_ref, cbuf, s_ref, sem):
    b = pl.program_id(0)
    pages = PAGES_PER_STEP
    keys = pages * PAGE_SIZE
    blocks_per_chunk = keys // MOBA_BLOCK
    n_blocks = n_chunks * blocks_per_chunk
    n_stream = 2 * n_chunks
    ahead = PAGE_SLOTS - 1
    assert n_stream % PAGE_SLOTS == 0 and ahead <= n_stream

    def chunk_copies(seq, j, lookup):
        pool = pk_hbm if j < n_chunks else pv_hbm
        first = (j % n_chunks) * pages
        return [pltpu.make_async_copy(pool.at[pt_ref[seq, first + p] if lookup else 0],
                                      cbuf.at[j % PAGE_SLOTS, p], sem.at[j % PAGE_SLOTS])
                for p in range(pages)]

    def chunk(j):
        return jnp.concatenate([cbuf[j % PAGE_SLOTS, p] for p in range(pages)], axis=1)

    def start(seq, j):
        for p, cp in enumerate(chunk_copies(seq, j, True)):
            cp.start(priority=p % 2)

    @pl.when(b == 0)
    def _():
        for j in range(ahead):
            start(b, j)

    qexp32 = qexp_ref[0]
    qexp = qexp32.astype(BF16)
    blk_lane = lax.broadcasted_iota(jnp.int32, (KV_DIM, LANES), 1)
    ksum = jnp.zeros((KV_DIM, LANES), F32)
    for j in range(n_stream):
        if j + ahead < n_stream:
            start(b, j + ahead)
        else:
            @pl.when(b + 1 < n_seq)
            def _():
                start(b + 1, j + ahead - n_stream)
        for cp in chunk_copies(b, j, False):
            cp.wait()
        cols = slice((j % n_chunks) * keys, (j % n_chunks + 1) * keys)
        if j < n_chunks:
            kt = chunk(j)
            s_ref[:, cols] = jnp.dot(qexp, kt.astype(BF16), preferred_element_type=F32)
            for n in range(blocks_per_chunk):
                blk_sum = jnp.sum(kt[:, n * MOBA_BLOCK:(n + 1) * MOBA_BLOCK], axis=1, keepdims=True)
                ksum = jnp.where(blk_lane == j * blocks_per_chunk + n, blk_sum, ksum)
        if j == n_chunks - 1:
            gate = _dot_split(qexp32, ksum * (1.0 / MOBA_BLOCK))
            sel_bias = jnp.where(_top3_mask(gate, n_blocks), 0.0, NEG_INF).astype(BF16)
            s_own = lax.dot_general(qexp, kn_ref[...].astype(BF16), NT_DIMS, preferred_element_type=F32)
            qi = lax.bitwise_and(lax.broadcasted_iota(jnp.int32, s_own.shape, 0), nq - 1)
            kj = lax.broadcasted_iota(jnp.int32, s_own.shape, 1)
            s_own = jnp.where(kj <= qi, s_own, NEG_INF)
            m_tile = jnp.full((N_HEADS * nq, MOBA_BLOCK), NEG_INF, F32)
            for c in range(n_chunks):
                cc = slice(c * keys, (c + 1) * keys)
                sm = s_ref[:, cc] + jnp.dot(sel_bias, expand_ref[:, cc], preferred_element_type=F32)
                s_ref[:, cc] = sm
                for n in range(blocks_per_chunk):
                    m_tile = jnp.maximum(m_tile, sm[:, n * MOBA_BLOCK:(n + 1) * MOBA_BLOCK])
            m = jnp.maximum(jnp.max(m_tile, axis=1, keepdims=True), jnp.max(s_own, axis=1, keepdims=True))
            p_own = jnp.exp2(s_own - m)
            l_own = jnp.sum(p_own, axis=1, keepdims=True)
            acc = jnp.dot(p_own.astype(BF16), vn_ref[...].astype(BF16), preferred_element_type=F32)
            l_tile = jnp.zeros((N_HEADS * nq, MOBA_BLOCK), F32)
        if j >= n_chunks:
            p = jnp.exp2(s_ref[:, cols] - m)
            for n in range(blocks_per_chunk):
                l_tile = l_tile + p[:, n * MOBA_BLOCK:(n + 1) * MOBA_BLOCK]
            acc = acc + lax.dot_general(p.astype(BF16), chunk(j).astype(BF16), NT_DIMS,
                                        preferred_element_type=F32)
    o_ref[0] = acc / (l_own + jnp.sum(l_tile, axis=1, keepdims=True))


def _moba_sample(qexp, k, v, pool_k, pool_v, page_table, row0, nq):
    db, n_pages = page_table.shape
    n_pool = pool_k.shape[0]
    pages = PAGES_PER_STEP
    n_chunks = n_pages // pages
    rows = N_HEADS * nq
    pk = pool_k.transpose(0, 2, 3, 1).reshape(n_pool, KV_DIM, PAGE_SIZE)
    pv = pool_v.transpose(0, 2, 3, 1).reshape(n_pool, KV_DIM, PAGE_SIZE)
    n_keys = n_pages * PAGE_SIZE
    expand = (jnp.arange(LANES)[:, None] == (jnp.arange(n_keys) // MOBA_BLOCK)[None, :]).astype(BF16)
    new = lambda b, pt: (row0 // nq + b, 0)
    grid_spec = pltpu.PrefetchScalarGridSpec(
        num_scalar_prefetch=1,
        grid=(db,),
        in_specs=[pl.BlockSpec((1, rows, KV_DIM), lambda b, pt: (b, 0, 0)),
                  pl.BlockSpec((nq, KV_DIM), new), pl.BlockSpec((nq, KV_DIM), new),
                  pl.BlockSpec((LANES, n_keys), lambda b, pt: (0, 0)),
                  pl.BlockSpec(memory_space=pl.ANY), pl.BlockSpec(memory_space=pl.ANY)],
        out_specs=pl.BlockSpec((1, rows, KV_DIM), lambda b, pt: (b, 0, 0)),
        scratch_shapes=[pltpu.VMEM((PAGE_SLOTS, pages, KV_DIM, PAGE_SIZE), F32),
                        pltpu.VMEM((rows, n_keys), F32),
                        pltpu.SemaphoreType.DMA((PAGE_SLOTS,))])
    return pl.pallas_call(
        functools.partial(_moba_sample_kernel, nq, n_chunks, db),
        grid_spec=grid_spec,
        out_shape=jax.ShapeDtypeStruct((db, rows, KV_DIM), F32),
        compiler_params=_params("arbitrary"), name="moba_sample",
    )(page_table, qexp, k, v, expand, pk, pv)


def _proj_route_kernel(n_first_o, n_first_x, *refs):
    refs = list(refs)
    o = _read_row_parts(refs, n_first_o)
    x = _read_row_parts(refs, n_first_x)
    wo_ref, bo_ref, g_ref, wr_ref, br_ref, x1_ref, h_ref, route_ref = refs
    x1 = x + jnp.dot(o, wo_ref[...], preferred_element_type=F32) + bo_ref[...]
    x1_ref[...] = x1
    h = _rms(x1, g_ref[...])
    _store_token_tiles(h_ref, h)
    h_hi = h.astype(BF16)
    h_lo = (h - h_hi.astype(F32)).astype(BF16)
    lg = (jnp.dot(h_hi, wr_ref[0], preferred_element_type=F32) + jnp.dot(h_hi, wr_ref[1], preferred_element_type=F32)
          + jnp.dot(h_lo, wr_ref[0], preferred_element_type=F32) + br_ref[...])
    lane = lax.broadcasted_iota(jnp.int32, lg.shape, 1)
    lane_f = lane.astype(F32)
    is_group = lane < N_GROUPS
    gl = jnp.where(is_group, lg, NEG_INF)
    gmax = jnp.max(gl, axis=1, keepdims=True)
    grp = jnp.min(jnp.where(gl == gmax, lane_f, 1e9), axis=1, keepdims=True)
    g_w = 1.0 / jnp.sum(jnp.where(is_group, jnp.exp(gl - gmax), 0.0), axis=1, keepdims=True)
    lane_grp = lax.shift_right_logical(lane - N_GROUPS, 3).astype(F32)
    in_grp = (lane >= N_GROUPS) & (lane < N_GROUPS + N_EXPERTS) & (lane_grp == grp)
    el = jnp.where(in_grp, lg, NEG_INF)
    v1 = jnp.max(el, axis=1, keepdims=True)
    i1 = jnp.min(jnp.where(el == v1, lane_f, 1e9), axis=1, keepdims=True)
    el2 = jnp.where(lane_f == i1, BELOW_NEG_INF, el)
    v2 = jnp.max(el2, axis=1, keepdims=True)
    i2 = jnp.min(jnp.where(el2 == v2, lane_f, 1e9), axis=1, keepdims=True)
    e2w = jnp.exp(v2 - v1)
    w1 = g_w / (1.0 + e2w)
    w2 = g_w * e2w / (1.0 + e2w)
    route = jnp.where(lane == 0, i1 - N_GROUPS, 0.0)
    route = jnp.where(lane == 1, i2 - N_GROUPS, route)
    route = jnp.where(lane == 2, w1, route)
    route = jnp.where(lane == 3, w2, route)
    route_ref[...] = route


def _proj_route(o, x, wo, bo, g, w_gr, b_gr, w_er, b_er):
    o_args, n_first_o = _row_parts(o)
    x_args, n_first_x = _row_parts(x)
    t = sum(a.shape[0] for a in x_args)
    row = lambda i: (i, 0)
    fixed = lambda i: (0, 0)
    pad = LANES - N_GROUPS - N_EXPERTS
    wr = jnp.concatenate([w_gr, w_er, jnp.zeros((D_MODEL, pad), F32)], axis=1)
    wr_hi = wr.astype(BF16)
    wr = jnp.stack([wr_hi, (wr - wr_hi.astype(F32)).astype(BF16)])
    br = jnp.concatenate([b_gr, b_er, jnp.zeros((pad,), F32)]).reshape(1, LANES)
    return pl.pallas_call(
        functools.partial(_proj_route_kernel, n_first_o, n_first_x),
        grid=(t // ROW_TILE,),
        in_specs=_row_part_specs(Q_DIM, n_first_o) + _row_part_specs(D_MODEL, n_first_x)
        + [pl.BlockSpec((Q_DIM, D_MODEL), fixed), pl.BlockSpec((1, D_MODEL), fixed),
           pl.BlockSpec((1, D_MODEL), fixed), pl.BlockSpec((2, D_MODEL, LANES), lambda i: (0, 0, 0)),
           pl.BlockSpec((1, LANES), fixed)],
        out_specs=[pl.BlockSpec((ROW_TILE, D_MODEL), row), pl.BlockSpec((ROW_TILE * TOKEN_TILE, LANES), row),
                   pl.BlockSpec((ROW_TILE, LANES), row)],
        out_shape=[jax.ShapeDtypeStruct((t, D_MODEL), F32), jax.ShapeDtypeStruct((t * TOKEN_TILE, LANES), F32),
                   jax.ShapeDtypeStruct((t, LANES), F32)],
        compiler_params=_params("arbitrary"), name="proj_route",
    )(*o_args, *x_args, wo.astype(BF16), bo.reshape(1, D_MODEL), g.reshape(1, D_MODEL), wr, br)


def _expert_kernel(n_steps, blk_e_ref, nvalid_ref, src_ref, src_next_ref, dst_ref, wrow_ref, h_hbm,
                   wg_ref, wu_ref, wd_ref, y_hbm, xbuf, ybuf, wg_bf, wu_bf, wd_bf, gsem, ssem):
    i = pl.program_id(0)
    slot = lax.rem(i, 2)
    rows = EXPERT_ROWS
    nvalid = nvalid_ref[i]
    nvalid_next = jnp.where(i + 1 < n_steps, nvalid_ref[jnp.minimum(i + 1, n_steps - 1)], 0)
    nvalid_prev2 = jnp.where(i >= 2, nvalid_ref[jnp.maximum(i - 2, 0)], 0)

    def tile(ref, first):
        return ref.at[pl.ds(pl.multiple_of(first, TOKEN_TILE), TOKEN_TILE)]

    def row_copy_in(idx_ref, r, s):
        return pltpu.make_async_copy(tile(h_hbm, idx_ref[0, r]), tile(xbuf.at[s], r * TOKEN_TILE), gsem.at[s])

    def row_copy_out(r, s):
        return pltpu.make_async_copy(tile(ybuf.at[s], r * TOKEN_TILE), tile(y_hbm, dst_ref[0, r]), ssem.at[s])

    def for_rows(fn, n=None):
        def body(g, carry):
            for u in range(8):
                r = g * 8 + u
                if n is None:
                    fn(r, u)
                else:
                    pl.when(r < n)(functools.partial(fn, r, u))
            return carry
        lax.fori_loop(0, rows // 8, body, 0)

    @pl.when((i == 0) & (nvalid > 0))
    def _():
        for_rows(lambda r, u: row_copy_in(src_ref, r, 0).start(priority=u % 2))

    @pl.when(nvalid_next > 0)
    def _():
        for_rows(lambda r, u: row_copy_in(src_next_ref, r, 1 - slot).start(priority=u % 2))

    @pl.when(nvalid_prev2 > 0)
    def _():
        for_rows(lambda r, u: row_copy_out(r, slot).wait(), nvalid_prev2)

    @pl.when(nvalid > 0)
    def _():
        e_changed = jnp.logical_or(i == 0, blk_e_ref[i] != blk_e_ref[jnp.maximum(i - 1, 0)])

        @pl.when(e_changed)
        def _():
            wg_bf[...] = wg_ref[0, 0].astype(BF16)
            wu_bf[...] = wu_ref[0, 0].astype(BF16)
            wd_bf[...] = wd_ref[0, 0].astype(BF16)

        for_rows(lambda r, u: row_copy_in(src_ref, r, slot).wait())
        x = _load_token_tiles(xbuf.at[slot], rows).astype(BF16)
        gate = jnp.dot(x, wg_bf[...], preferred_element_type=F32)
        up = jnp.dot(x, wu_bf[...], preferred_element_type=F32)
        hid = (gate * jax.nn.sigmoid(gate) * up).astype(BF16)
        y = jnp.dot(hid, wd_bf[...], preferred_element_type=F32)
        _store_token_tiles(ybuf.at[slot], y * wrow_ref[...])
        for_rows(lambda r, u: row_copy_out(r, slot).start(priority=u % 2), nvalid)


def _expert_ffn(h, e_idx, e_w, w_gate, w_up, w_down, layer):
    t = h.shape[0] // TOKEN_TILE
    a = t * EXPERT_TOPK
    rows = EXPERT_ROWS
    n_blk = -(-a // rows) + N_EXPERTS
    n_steps = n_blk + 2
    n_rows = n_steps * rows
    flat_e = e_idx.reshape(a)
    order = jnp.argsort(flat_e).astype(jnp.int32)
    counts = jnp.sum(flat_e[:, None] == jnp.arange(N_EXPERTS, dtype=jnp.int32)[None, :], axis=0, dtype=jnp.int32)
    padded = (counts + rows - 1) // rows * rows
    pad_end = jnp.cumsum(padded)
    pad_start = pad_end - padded
    start = jnp.cumsum(counts) - counts
    blk_first = jnp.arange(n_steps, dtype=jnp.int32) * rows
    blk_e = jnp.minimum(jnp.sum(blk_first[:, None] >= pad_end[None, :], axis=1), N_EXPERTS - 1).astype(jnp.int32)
    rank0 = blk_first - pad_start[blk_e]
    nvalid = jnp.clip(counts[blk_e] - rank0, 0, rows).astype(jnp.int32)
    within = jnp.arange(rows, dtype=jnp.int32)[None, :]
    valid = within < nvalid[:, None]
    assign = order[jnp.clip((start[blk_e] + rank0)[:, None] + within, 0, a - 1)]
    tok = assign // EXPERT_TOPK
    src = jnp.where(valid, tok, 0) * TOKEN_TILE
    dst = jnp.where(valid, (assign % EXPERT_TOPK) * t + tok, 0) * TOKEN_TILE
    wrow = jnp.where(valid, e_w.reshape(a)[assign], 0.0)
    idx_blk = lambda i, be, na: (i, 0, 0)
    idx_next = lambda i, be, na: (jnp.minimum(i + 1, n_steps - 1), 0, 0)
    w_in = lambda i, be, na: (layer, be[i], 0, 0)
    grid_spec = pltpu.PrefetchScalarGridSpec(
        num_scalar_prefetch=2,
        grid=(n_steps,),
        in_specs=[pl.BlockSpec((1, 1, rows), idx_blk, memory_space=pltpu.SMEM),
                  pl.BlockSpec((1, 1, rows), idx_next, memory_space=pltpu.SMEM),
                  pl.BlockSpec((1, 1, rows), idx_blk, memory_space=pltpu.SMEM),
                  pl.BlockSpec((rows, 1), lambda i, be, na: (i, 0)),
                  pl.BlockSpec(memory_space=pl.ANY),
                  pl.BlockSpec((1, 1, D_MODEL, EXPERT_FF), w_in), pl.BlockSpec((1, 1, D_MODEL, EXPERT_FF), w_in),
                  pl.BlockSpec((1, 1, EXPERT_FF, D_MODEL), w_in)],
        out_specs=pl.BlockSpec(memory_space=pl.ANY),
        scratch_shapes=[pltpu.VMEM((2, rows * TOKEN_TILE, LANES), F32), pltpu.VMEM((2, rows * TOKEN_TILE, LANES), F32),
                        pltpu.VMEM((D_MODEL, EXPERT_FF), BF16), pltpu.VMEM((D_MODEL, EXPERT_FF), BF16),
                        pltpu.VMEM((EXPERT_FF, D_MODEL), BF16),
                        pltpu.SemaphoreType.DMA((2,)), pltpu.SemaphoreType.DMA((2,))])

    def body(blk_e_ref, nvalid_ref, src_ref, src_next_ref, dst_ref, *rest):
        _expert_kernel(n_steps, blk_e_ref, nvalid_ref, src_ref.at[0], src_next_ref.at[0], dst_ref.at[0], *rest)

    return pl.pallas_call(
        body, grid_spec=grid_spec,
        out_shape=jax.ShapeDtypeStruct((EXPERT_TOPK * t * TOKEN_TILE, LANES), F32),
        compiler_params=_params("arbitrary"), name="expert_ffn",
    )(blk_e, nvalid, src.reshape(n_steps, 1, rows), src.reshape(n_steps, 1, rows),
      dst.reshape(n_steps, 1, rows), wrow.reshape(n_rows, 1), h, w_gate, w_up, w_down)


def _final_kernel(x_ref, y0_ref, y1_ref, g_ref, o_ref):
    y = _load_token_tiles(y0_ref, ROW_TILE) + _load_token_tiles(y1_ref, ROW_TILE)
    o_ref[...] = _rms(x_ref[...] + y, g_ref[...])


def _final_norm(x, y2, g, row0, n_rows):
    t = x.shape[0]
    off = row0 // ROW_TILE
    return pl.pallas_call(
        _final_kernel,
        grid=(n_rows // ROW_TILE,),
        in_specs=[pl.BlockSpec((ROW_TILE, D_MODEL), lambda i: (i + off, 0)),
                  pl.BlockSpec((ROW_TILE * TOKEN_TILE, LANES), lambda i: (i + off, 0)),
                  pl.BlockSpec((ROW_TILE * TOKEN_TILE, LANES), lambda i: (i + off + t // ROW_TILE, 0)),
                  pl.BlockSpec((1, D_MODEL), lambda i: (0, 0))],
        out_specs=pl.BlockSpec((ROW_TILE, D_MODEL), lambda i: (i, 0)),
        out_shape=jax.ShapeDtypeStruct((n_rows, D_MODEL), F32),
        compiler_params=_params("arbitrary"), name="final_norm",
    )(x, y2, y2, g.reshape(1, D_MODEL))


def _rope_tables(pos):
    half = HEAD_DIM // 2
    inv = ROPE_THETA ** (-jnp.arange(half, dtype=F32) / half)
    ang = pos.astype(F32)[:, None] * inv[None, :]
    cos = jnp.cos(ang)
    sin = jnp.sin(ang)
    reps = LANES // HEAD_DIM
    return jnp.tile(jnp.concatenate([cos, cos], axis=1), (1, reps)), jnp.tile(jnp.concatenate([-sin, sin], axis=1), (1, reps))


def _moe(o, x, wo, bo, g, w_gr, b_gr, w_er, b_er, w_gate, w_up, w_down, layer):
    x1, h, route = _proj_route(o, x, wo, bo, g, w_gr, b_gr, w_er, b_er)
    e_idx = jnp.clip(route[:, :EXPERT_TOPK].astype(jnp.int32), 0, N_EXPERTS - 1)
    e_w = route[:, EXPERT_TOPK:2 * EXPERT_TOPK]
    return x1, _expert_ffn(h, e_idx, e_w, w_gate, w_up, w_down, layer)


def kernel(x_prompt, x_sample, cache_swa_k, cache_swa_v, cache_moba_k, cache_moba_v, page_table, norm_mix, norm_ffn, norm_final, swa_w_qkv, swa_b_qkv, swa_sinks, swa_w_o, swa_b_o, moba_w_qkv, moba_w_o, w_group_router, b_group_router, w_expert_router, b_expert_router, w_gate, w_up, w_down):
    batch, seq, _ = x_prompt.shape
    db, nq, _ = x_sample.shape
    past_len = page_table.shape[1] * PAGE_SIZE
    tp = batch * seq
    ts = db * nq
    t = tp + ts
    x = (x_prompt.reshape(tp, D_MODEL), x_sample.reshape(ts, D_MODEL))
    pos =jnp.concatenate([jnp.tile(jnp.arange(seq), batch), jnp.tile(past_len + jnp.arange(nq), db)])
    cos_t, sin_t = _rope_tables(pos)

    q, k0, v0, k_rep, _, qt, vt = _qkv_rope(x, [], norm_mix[0], swa_w_qkv[0], swa_b_qkv[0], cos_t, sin_t)
    o_p = _swa_prompt(qt, k_rep, vt, swa_sinks[0], batch, seq)
    o_s, swa_ks, swa_vs = _swa_sample(q, k0, v0, cache_swa_k[0], cache_swa_v[0], swa_sinks[0], tp, nq)
    x1, y2 = _moe((o_p, o_s), x, swa_w_o[0], swa_b_o[0], norm_ffn[0], w_group_router[0], b_group_router[0],
                  w_expert_router[0], b_expert_router[0], w_gate, w_up, w_down, 0)

    x, q, k1, v1, k_rep, kmean, qt, vt = _qkv_rope(
        x1, [(y2, 0), (y2, t)], norm_mix[1], moba_w_qkv[0], jnp.zeros((QKV_DIM,), F32), cos_t, sin_t)
    nb = seq // MOBA_BLOCK
    kmean_rep = jnp.tile(kmean[:batch * nb, 0].reshape(batch, nb, N_KV_HEADS, 1, HEAD_DIM), (1, 1, 1, GROUP, 1))
    kmean_rep = kmean_rep.reshape(batch, nb, N_KV_HEADS, GROUP_LANES).transpose(0, 2, 1, 3)
    o_p = _moba_prompt(qt, k_rep, vt, kmean_rep, batch, seq)
    qs = q[tp:].reshape(db, nq, N_HEADS, 1, HEAD_DIM)
    slot = (jnp.arange(N_HEADS) // GROUP)[:, None] == jnp.arange(N_KV_HEADS)[None, :]
    qexp = jnp.where(slot[None, None, :, :, None], qs, 0.0).transpose(0, 2, 1, 3, 4).reshape(db, N_HEADS * nq, KV_DIM)
    o_s = _moba_sample(qexp, k1, v1, cache_moba_k[0], cache_moba_v[0], page_table, tp, nq)
    o_s = o_s.reshape(db, N_HEADS, nq, N_KV_HEADS, HEAD_DIM)
    o_s = jnp.sum(jnp.where(slot[None, :, None, :, None], o_s, 0.0), axis=3).transpose(0, 2, 1, 3)
    x1, y2 = _moe((o_p, o_s.reshape(ts, Q_DIM).astype(BF16)), x, moba_w_o[0], jnp.zeros((D_MODEL,), F32), norm_ffn[1], w_group_router[1],
                  b_group_router[1], w_expert_router[1], b_expert_router[1], w_gate, w_up, w_down, 1)

    y_prompt = _final_norm(x1, y2, norm_final, 0, tp).reshape(batch, seq, D_MODEL)
    y_sample = _final_norm(x1, y2, norm_final, tp, ts).reshape(db, nq, D_MODEL)
    win = min(WINDOW, seq)
    kv5 = lambda a, b, s: a.reshape(1, b, s, N_KV_HEADS, HEAD_DIM)
    k0p = k0[:tp].reshape(batch, seq, KV_DIM)[:, seq - win:]
    v0p = v0[:tp].reshape(batch, seq, KV_DIM)[:, seq - win:]
    return (y_prompt, y_sample, kv5(k0p, batch, win), kv5(v0p, batch, win),
            kv5(swa_ks, db, swa_ks.shape[1]), kv5(swa_vs, db, swa_vs.shape[1]),
            kv5(k1[:tp], batch, seq), kv5(v1[:tp], batch, seq), kv5(k1[tp:], db, nq), kv5(v1[tp:], db, nq))
```

```python
import functools

import jax
import jax.numpy as jnp
from jax import lax
from jax.experimental import pallas as pl
from jax.experimental.pallas import tpu as pltpu

D_MODEL = 1024
N_HEADS = 16
N_KV_HEADS = 4
HEAD_DIM = 64
GROUP = N_HEADS // N_KV_HEADS
Q_DIM = N_HEADS * HEAD_DIM
KV_DIM = N_KV_HEADS * HEAD_DIM
QKV_DIM = Q_DIM + 2 * KV_DIM
ROPE_THETA = 10000.0
WINDOW = 128
MOBA_BLOCK = 256
MOBA_TOPK = 3
PAGE_SIZE = 128
N_GROUPS = 8
EXPERTS_PER_GROUP = 8
N_EXPERTS = N_GROUPS * EXPERTS_PER_GROUP
EXPERT_TOPK = 2
EXPERT_FF = D_MODEL // 4
RMS_EPS = 1e-6
NEG_INF = -1e30
BELOW_NEG_INF = -3e38
LOG2_E = 1.4426950408889634
VT_ROWS = HEAD_DIM + 16

LANES = 128
GROUP_LANES = GROUP * HEAD_DIM
ROW_TILE = 256
EXPERT_ROWS = 256
PAGES_PER_STEP = 16
MOBA_BLOCKS_PER_TRIP = 4
PAGE_SLOTS = 8
VMEM_LIMIT = 56 * 1024 * 1024

F32 = jnp.float32
BF16 = jnp.bfloat16
NT_DIMS = (((1,), (1,)), ((), ()))


def _params(*sem):
    return pltpu.CompilerParams(dimension_semantics=sem, vmem_limit_bytes=VMEM_LIMIT)


def _rms(x, g):
    return x * lax.rsqrt(jnp.mean(x * x, axis=-1, keepdims=True) + RMS_EPS) * g


TOKEN_TILE = D_MODEL // LANES


def _load_token_tiles(ref, rows):
    return jnp.concatenate([ref[pl.ds(c, rows, stride=TOKEN_TILE), :] for c in range(TOKEN_TILE)], axis=1)


def _store_token_tiles(ref, x):
    for c in range(TOKEN_TILE):
        ref[pl.ds(c, x.shape[0], stride=TOKEN_TILE), :] = x[:, c * LANES:(c + 1) * LANES]


def _dot_split(a, b):
    a_hi = a.astype(BF16)
    a_lo = (a - a_hi.astype(F32)).astype(BF16)
    b_hi = b.astype(BF16)
    b_lo = (b - b_hi.astype(F32)).astype(BF16)
    return (jnp.dot(a_hi, b_hi, preferred_element_type=F32) + jnp.dot(a_hi, b_lo, preferred_element_type=F32)
            + jnp.dot(a_lo, b_hi, preferred_element_type=F32))


def _row_parts(x):
    if isinstance(x, tuple):
        return list(x), x[0].shape[0] // ROW_TILE
    return [x], 0


def _row_part_specs(width, n_first):
    if not n_first:
        return [pl.BlockSpec((ROW_TILE, width), lambda i: (i, 0))]
    return [pl.BlockSpec((ROW_TILE, width), lambda i: (jnp.minimum(i, n_first - 1), 0)),
            pl.BlockSpec((ROW_TILE, width), lambda i: (jnp.maximum(i - n_first, 0), 0))]


def _read_row_parts(refs, n_first):
    if not n_first:
        return refs.pop(0)[...]
    first, second = refs.pop(0), refs.pop(0)
    return jnp.where(pl.program_id(0) < n_first, first[...], second[...])


def _replicate_head(x128, odd):
    swapped = pltpu.roll(x128, HEAD_DIM, axis=1)
    low = lax.broadcasted_iota(jnp.int32, x128.shape, 1) < HEAD_DIM
    return jnp.where(low != odd, x128, swapped)


def _replicate_kv(x256, kvh):
    col = x256[:, (kvh // 2) * LANES:(kvh // 2 + 1) * LANES]
    rep = _replicate_head(col, kvh % 2 == 1)
    return jnp.concatenate([rep, rep], axis=1)


def _expand_heads(q):
    head = lax.shift_right_logical(lax.broadcasted_iota(jnp.int32, q.shape, 1), 6)
    return jnp.concatenate([jnp.where(head == h, q, 0.0) for h in range(GROUP)], axis=0)


def _collapse_heads(o_rep, rows):
    head = lax.shift_right_logical(lax.broadcasted_iota(jnp.int32, (rows, GROUP_LANES), 1), 6)
    out = jnp.zeros((rows, GROUP_LANES), F32)
    for h in range(GROUP):
        out = jnp.where(head == h, o_rep[h * rows:(h + 1) * rows], out)
    return out


def _top3_mask(gate, n_valid):
    lane = lax.broadcasted_iota(jnp.int32, gate.shape, 1)
    lane_f = lane.astype(F32)
    valid = lane < n_valid
    g = jnp.where(valid, gate, NEG_INF)
    sel = jnp.zeros(gate.shape, jnp.bool_)
    for _ in range(MOBA_TOPK):
        mx = jnp.max(g, axis=1, keepdims=True)
        first = jnp.min(jnp.where(g == mx, lane_f, 1e9), axis=1, keepdims=True)
        hit = lane_f == first
        sel = jnp.logical_or(sel, jnp.logical_and(hit, valid))
        g = jnp.where(hit, BELOW_NEG_INF, g)
    return sel


def _qkv_kernel(n_add, n_first, *refs):
    refs = list(refs)
    x = _read_row_parts(refs, n_first)
    add_refs = [refs.pop(0) for _ in range(n_add)]
    g_ref, w_ref, b_ref, cos_ref, sin_ref = refs[:5]
    outs = refs[5:]
    for r in add_refs:
        x = x + _load_token_tiles(r, ROW_TILE)
    if n_add:
        outs.pop(0)[...] = x
    q_ref, k_ref, v_ref, krep_ref = outs[:4]
    h = _rms(x, g_ref[...]).astype(BF16)
    y = jnp.dot(h, w_ref[...], preferred_element_type=F32) + b_ref[...]
    cos = cos_ref[...]
    sin = sin_ref[...]
    low = lax.bitwise_and(lax.broadcasted_iota(jnp.int32, cos.shape, 1), HEAD_DIM - 1) < HEAD_DIM // 2

    def rope(slab):
        partner = jnp.where(low, pltpu.roll(slab, LANES - HEAD_DIM // 2, axis=1),
                            pltpu.roll(slab, HEAD_DIM // 2, axis=1))
        return slab * cos + partner * sin

    scale = HEAD_DIM ** -0.5 * LOG2_E
    for c in range(Q_DIM // LANES):
        q_ref[:, c * LANES:(c + 1) * LANES] = rope(y[:, c * LANES:(c + 1) * LANES]) * scale
    k = jnp.concatenate([rope(y[:, Q_DIM + c * LANES:Q_DIM + (c + 1) * LANES])
                         for c in range(KV_DIM // LANES)], axis=1)
    v = y[:, Q_DIM + KV_DIM:]
    k_ref[...] = k
    v_ref[...] = v
    for kvh in range(N_KV_HEADS):
        krep_ref[kvh] = _replicate_kv(k, kvh).astype(BF16)
    kmean_ref, qt_ref, vt_ref = outs[4:]
    kmean_ref[0] = jnp.broadcast_to(jnp.mean(k, axis=0, keepdims=True), (8, KV_DIM))
    qt_ref[...] = q_ref[...].T
    vt = v.T.reshape(N_KV_HEADS, HEAD_DIM, ROW_TILE).astype(BF16)
    vt_ref[:, 0] = jnp.concatenate([vt, jnp.ones((N_KV_HEADS, VT_ROWS - HEAD_DIM, ROW_TILE), BF16)], axis=1)


def _qkv_rope(x, adds, g, w, b, cos_t, sin_t):
    args, n_first = _row_parts(x)
    t = sum(a.shape[0] for a in args)
    nt = t // ROW_TILE
    row = lambda i: (i, 0)
    fixed = lambda i: (0, 0)
    in_specs = _row_part_specs(D_MODEL, n_first)
    for arr, off in adds:
        in_specs.append(pl.BlockSpec((ROW_TILE * TOKEN_TILE, LANES),
                                     functools.partial(lambda o, i: (i + o, 0), off // ROW_TILE)))
        args.append(arr)
    in_specs += [pl.BlockSpec((1, D_MODEL), fixed), pl.BlockSpec((D_MODEL, QKV_DIM), fixed),
                 pl.BlockSpec((1, QKV_DIM), fixed), pl.BlockSpec((ROW_TILE, LANES), row),
                 pl.BlockSpec((ROW_TILE, LANES), row)]
    args += [g.reshape(1, D_MODEL), w.astype(BF16), b.reshape(1, QKV_DIM), cos_t, sin_t]
    out_shape, out_specs = [], []
    if adds:
        out_shape.append(jax.ShapeDtypeStruct((t, D_MODEL), F32))
        out_specs.append(pl.BlockSpec((ROW_TILE, D_MODEL), row))
    rep_shape = jax.ShapeDtypeStruct((N_KV_HEADS, t, GROUP_LANES), BF16)
    rep_spec = pl.BlockSpec((N_KV_HEADS, ROW_TILE, GROUP_LANES), lambda i: (0, i, 0))
    out_shape += [jax.ShapeDtypeStruct((t, Q_DIM), F32), jax.ShapeDtypeStruct((t, KV_DIM), F32),
                  jax.ShapeDtypeStruct((t, KV_DIM), F32), rep_shape]
    out_specs += [pl.BlockSpec((ROW_TILE, Q_DIM), row), pl.BlockSpec((ROW_TILE, KV_DIM), row),
                  pl.BlockSpec((ROW_TILE, KV_DIM), row), rep_spec]
    out_shape += [jax.ShapeDtypeStruct((nt, 8, KV_DIM), F32), jax.ShapeDtypeStruct((Q_DIM, t), F32),
                  jax.ShapeDtypeStruct((N_KV_HEADS, nt, VT_ROWS, ROW_TILE), BF16)]
    out_specs += [pl.BlockSpec((1, 8, KV_DIM), lambda i: (i, 0, 0)),
                  pl.BlockSpec((Q_DIM, ROW_TILE), lambda i: (0, i)),
                  pl.BlockSpec((N_KV_HEADS, 1, VT_ROWS, ROW_TILE), lambda i: (0, i, 0, 0))]
    return pl.pallas_call(
        functools.partial(_qkv_kernel, len(adds), n_first),
        grid=(nt,), in_specs=in_specs, out_specs=out_specs, out_shape=out_shape,
        compiler_params=_params("arbitrary"), name="qkv_rope")(*args)


def _sink_softmax_pv(s, ok, sink_col, v):
    s = jnp.where(ok, s, NEG_INF)
    m = jnp.maximum(jnp.max(s, axis=1, keepdims=True), sink_col)
    p = jnp.exp2(s - m)
    denom = jnp.sum(p, axis=1, keepdims=True) + jnp.exp2(sink_col - m)
    return jnp.dot(p.astype(BF16), v, preferred_element_type=F32) / denom


def _sink_column(sinks_ref, kvh, rows):
    blk = lax.shift_right_logical(lax.broadcasted_iota(jnp.int32, (GROUP * rows, 1), 0), rows.bit_length() - 1)
    col = jnp.zeros((GROUP * rows, 1), F32)
    for h in range(GROUP):
        col = jnp.where(blk == h, sinks_ref[kvh * GROUP + h] * LOG2_E, col)
    return col


def _swa_prompt_kernel(sinks_ref, qt_ref, kprev_ref, kcur_ref, vprev_ref, vcur_ref, o_ref):
    i = pl.program_id(1)
    tile = ROW_TILE
    n_keys = WINDOW + tile
    key = lax.broadcasted_iota(jnp.int32, (n_keys, tile), 0)
    qry = lax.broadcasted_iota(jnp.int32, (n_keys, tile), 1)
    first_key = jnp.where(i > 0, 0, WINDOW)
    ok = (key > qry) & (key <= qry + WINDOW) & (key >= first_key)
    slabs = []
    for kvh in range(N_KV_HEADS):
        k = jnp.concatenate([kprev_ref[kvh, :, :2 * HEAD_DIM], kcur_ref[kvh, :, :2 * HEAD_DIM]], axis=0)
        vt = jnp.concatenate([vprev_ref[kvh, 0], vcur_ref[kvh, 0]], axis=1)
        outs = []
        for g in range(GROUP):
            h = kvh * GROUP + g
            qh = jnp.concatenate([qt_ref[h * HEAD_DIM:(h + 1) * HEAD_DIM, :].astype(BF16),
                                  jnp.zeros((HEAD_DIM, tile), BF16)], axis=0)
            s = jnp.where(ok, jnp.dot(k, qh, preferred_element_type=F32), NEG_INF)
            sink = sinks_ref[h] * LOG2_E
            m = jnp.maximum(jnp.max(s, axis=0, keepdims=True), sink)
            pv = jnp.dot(vt, jnp.exp2(s - m).astype(BF16), preferred_element_type=F32)
            denom = pv[HEAD_DIM:HEAD_DIM + 1] + jnp.exp2(sink - m)
            outs.append(pv[:HEAD_DIM] / denom)
        slabs.append(jnp.concatenate(outs, axis=0).T.astype(BF16))
    o_ref[...] = jnp.concatenate(slabs, axis=1)


def _swa_prompt(qt, k_rep, vt, sinks, batch, seq):
    nt = seq // ROW_TILE
    per_tile = ROW_TILE // WINDOW
    return pl.pallas_call(
        _swa_prompt_kernel,
        grid=(batch, nt),
        in_specs=[pl.BlockSpec(memory_space=pltpu.SMEM),
                  pl.BlockSpec((Q_DIM, ROW_TILE), lambda b, i: (0, b * nt + i)),
                  pl.BlockSpec((N_KV_HEADS, WINDOW, GROUP_LANES),
                               lambda b, i: (0, jnp.maximum((b * nt + i) * per_tile - 1, 0), 0)),
                  pl.BlockSpec((N_KV_HEADS, ROW_TILE, GROUP_LANES), lambda b, i: (0, b * nt + i, 0)),
                  pl.BlockSpec((N_KV_HEADS, 1, VT_ROWS, WINDOW),
                               lambda b, i: (0, jnp.maximum(b * nt + i - 1, 0), 0, per_tile - 1)),
                  pl.BlockSpec((N_KV_HEADS, 1, VT_ROWS, ROW_TILE), lambda b, i: (0, b * nt + i, 0, 0))],
        out_specs=pl.BlockSpec((ROW_TILE, Q_DIM), lambda b, i: (b * nt + i, 0)),
        out_shape=jax.ShapeDtypeStruct((batch * seq, Q_DIM), BF16),
        compiler_params=_params("arbitrary", "arbitrary"), name="swa_prompt",
    )(sinks, qt, k_rep, k_rep, vt, vt)


def _swa_sample_kernel(seqs, nq, sinks_ref, q_ref, kn_ref, vn_ref, ck_ref, cv_ref, o_ref, ko_ref, vo_ref):
    win = ck_ref.shape[1]

    def one_seq(s, carry):
        r0 = pl.multiple_of(s * nq, nq)
        q = q_ref[pl.ds(r0, nq), :]
        kn = kn_ref[pl.ds(r0, nq), :]
        vn = vn_ref[pl.ds(r0, nq), :]
        ck = ck_ref[s]
        cv = cv_ref[s]
        ko_ref[s] = jnp.concatenate([ck[nq:], kn], axis=0)
        vo_ref[s] = jnp.concatenate([cv[nq:], vn], axis=0)
        kk = jnp.concatenate([ck, kn], axis=0)
        vv = jnp.concatenate([cv, vn], axis=0)
        outs = []
        for kvh in range(N_KV_HEADS):
            qexp = _expand_heads(q[:, kvh * GROUP_LANES:(kvh + 1) * GROUP_LANES]).astype(BF16)
            k = _replicate_kv(kk, kvh).astype(BF16)
            v = _replicate_kv(vv, kvh).astype(BF16)
            sc = lax.dot_general(qexp, k, NT_DIMS, preferred_element_type=F32)
            qi = lax.bitwise_and(lax.broadcasted_iota(jnp.int32, sc.shape, 0), nq - 1)
            kj = lax.broadcasted_iota(jnp.int32, sc.shape, 1)
            ok = ((kj < win) & (kj > qi + win - WINDOW)) | ((kj >= win) & (kj - win <= qi))
            o_rep = _sink_softmax_pv(sc, ok, _sink_column(sinks_ref, kvh, nq), v)
            outs.append(_collapse_heads(o_rep, nq))
        o_ref[pl.ds(r0, nq), :] = jnp.concatenate(outs, axis=1).astype(BF16)
        return carry

    lax.fori_loop(0, seqs, one_seq, 0)


def _swa_sample(q, k, v, cache_k, cache_v, sinks, row0, nq):
    db, win = cache_k.shape[0], cache_k.shape[1]
    seqs = 8
    rows = seqs * nq
    tok = lambda g: (row0 // rows + g, 0)
    cache = lambda g: (g, 0, 0)
    return pl.pallas_call(
        functools.partial(_swa_sample_kernel, seqs, nq),
        grid=(db // seqs,),
        in_specs=[pl.BlockSpec(memory_space=pltpu.SMEM),
                  pl.BlockSpec((rows, Q_DIM), tok), pl.BlockSpec((rows, KV_DIM), tok),
                  pl.BlockSpec((rows, KV_DIM), tok),
                  pl.BlockSpec((seqs, win, KV_DIM), cache), pl.BlockSpec((seqs, win, KV_DIM), cache)],
        out_specs=[pl.BlockSpec((rows, Q_DIM), lambda g: (g, 0)),
                   pl.BlockSpec((seqs, win, KV_DIM), cache), pl.BlockSpec((seqs, win, KV_DIM), cache)],
        out_shape=[jax.ShapeDtypeStruct((db * nq, Q_DIM), BF16),
                   jax.ShapeDtypeStruct((db, win, KV_DIM), F32), jax.ShapeDtypeStruct((db, win, KV_DIM), F32)],
        compiler_params=_params("arbitrary"), name="swa_sample",
    )(sinks, q, k, v, cache_k.reshape(db, win, KV_DIM), cache_v.reshape(db, win, KV_DIM))


def _top3_rows(gate, n_valid):
    row = lax.broadcasted_iota(jnp.int32, gate.shape, 0)
    row_f = row.astype(F32)
    valid = row < n_valid
    g = jnp.where(valid, gate, NEG_INF)
    sel = jnp.zeros(gate.shape, F32)
    for _ in range(MOBA_TOPK):
        mx = jnp.max(g, axis=0, keepdims=True)
        first = jnp.min(jnp.where(g == mx, row_f, 1e9), axis=0, keepdims=True)
        hit = row_f == first
        sel = jnp.where(jnp.logical_and(hit, valid), 1.0, sel)
        g = jnp.where(hit, BELOW_NEG_INF, g)
    return sel


def _moba_prompt_kernel(qt_ref, k_ref, vt_ref, kmean_ref, o_ref, sel_ref, acc_ref):
    i = pl.program_id(2)
    q_heads32 = [jnp.concatenate([qt_ref[h * HEAD_DIM:(h + 1) * HEAD_DIM, :], jnp.zeros((HEAD_DIM, MOBA_BLOCK), F32)],
                                 axis=0) for h in range(GROUP)]
    kmean2 = kmean_ref[0, 0][:, :2 * HEAD_DIM]
    sel_ref[...] = _top3_rows(jnp.concatenate([_dot_split(kmean2, q) for q in q_heads32], axis=1), i)
    q_heads = [q.astype(BF16) for q in q_heads32]
    heads = [slice(h * MOBA_BLOCK, (h + 1) * MOBA_BLOCK) for h in range(GROUP)]

    def head_scores(j):
        start = pl.multiple_of(j * MOBA_BLOCK, MOBA_BLOCK)
        kj = k_ref[0, pl.ds(start, MOBA_BLOCK), :2 * HEAD_DIM]
        return [jnp.dot(kj, q_heads[h], preferred_element_type=F32) for h in range(GROUP)]

    m = []
    for h, s in enumerate(head_scores(i)):
        key = lax.broadcasted_iota(jnp.int32, s.shape, 0)
        qry = lax.broadcasted_iota(jnp.int32, s.shape, 1)
        s = jnp.where(key <= qry, s, NEG_INF)
        m.append(jnp.max(s, axis=0, keepdims=True))
        acc_ref[:, heads[h]] = jnp.dot(vt_ref[0, i], jnp.exp2(s - m[h]).astype(BF16), preferred_element_type=F32)

    def past_block(j, m_old):
        vtj = vt_ref[0, j]
        m_out = []
        for h, s in enumerate(head_scores(j)):
            chosen = sel_ref[pl.ds(j, 1), heads[h]] > 0.0
            m_new = jnp.where(chosen, jnp.maximum(m_old[h], jnp.max(s, axis=0, keepdims=True)), m_old[h])
            alpha = jnp.exp2(m_old[h] - m_new)
            p = jnp.exp2(s - jnp.where(chosen, m_new, -NEG_INF))
            acc_ref[:, heads[h]] = alpha * acc_ref[:, heads[h]] + jnp.dot(vtj, p.astype(BF16),
                                                                         preferred_element_type=F32)
            m_out.append(m_new)
        return tuple(m_out)

    def past_group(t, m_old):
        for u in range(MOBA_BLOCKS_PER_TRIP):
            m_old = past_block(MOBA_BLOCKS_PER_TRIP * t + u, m_old)
        return m_old

    lax.fori_loop(0, (i + MOBA_BLOCKS_PER_TRIP - 1) // MOBA_BLOCKS_PER_TRIP, past_group, tuple(m))
    ot = acc_ref[:HEAD_DIM] / acc_ref[HEAD_DIM:HEAD_DIM + 1]
    ot = jnp.concatenate([ot[:, h * MOBA_BLOCK:(h + 1) * MOBA_BLOCK] for h in range(GROUP)], axis=0)
    o_ref[...] = ot.T.astype(BF16)


def _moba_prompt(qt, k_rep, vt, kmean_rep, batch, seq):
    nb = seq // MOBA_BLOCK
    cols = GROUP * MOBA_BLOCK
    return pl.pallas_call(
        _moba_prompt_kernel,
        grid=(batch, N_KV_HEADS, nb),
        in_specs=[pl.BlockSpec((GROUP_LANES, MOBA_BLOCK), lambda b, h, i: (h, b * nb + i)),
                  pl.BlockSpec((1, seq, GROUP_LANES), lambda b, h, i: (h, b, 0)),
                  pl.BlockSpec((1, nb, VT_ROWS, MOBA_BLOCK), lambda b, h, i: (h, b, 0, 0)),
                  pl.BlockSpec((1, 1, nb, GROUP_LANES), lambda b, h, i: (b, h, 0, 0))],
        out_specs=pl.BlockSpec((MOBA_BLOCK, GROUP_LANES), lambda b, h, i: (b * nb + i, h)),
        out_shape=jax.ShapeDtypeStruct((batch * seq, Q_DIM), BF16),
        scratch_shapes=[pltpu.VMEM((nb, cols), F32), pltpu.VMEM((VT_ROWS, cols), F32)],
        compiler_params=_params("arbitrary", "arbitrary", "arbitrary"), name="moba_prompt",
    )(qt, k_rep, vt, kmean_rep)


def _moba_sample_kernel(nq, n_chunks, n_seq, pt_ref, qexp_ref, kn_ref, vn_ref, expand_ref, pk_hbm, pv_hbm,
                        o_ref, cbuf, s_ref, sem):
    b = pl.program_id(0)
    pages = PAGES_PER_STEP
    keys = pages * PAGE_SIZE
    blocks_per_chunk = keys // MOBA_BLOCK
    n_blocks = n_chunks * blocks_per_chunk
    n_stream = 2 * n_chunks
    ahead = PAGE_SLOTS - 1
    assert n_stream % PAGE_SLOTS == 0 and ahead <= n_stream

    def chunk_copies(seq, j, lookup):
        pool = pk_hbm if j < n_chunks else pv_hbm
        first = (j % n_chunks) * pages
        return [pltpu.make_async_copy(pool.at[pt_ref[seq, first + p] if lookup else 0],
                                      cbuf.at[j % PAGE_SLOTS, p], sem.at[j % PAGE_SLOTS])
                for p in range(pages)]

    def chunk(j):
        return jnp.concatenate([cbuf[j % PAGE_SLOTS, p] for p in range(pages)], axis=1)

    def start(seq, j):
        for p, cp in enumerate(chunk_copies(seq, j, True)):
            cp.start(priority=p % 2)

    @pl.when(b == 0)
    def _():
        for j in range(ahead):
            start(b, j)

    qexp32 = qexp_ref[0]
    qexp = qexp32.astype(BF16)
    blk_lane = lax.broadcasted_iota(jnp.int32, (KV_DIM, LANES), 1)
    ksum = jnp.zeros((KV_DIM, LANES), F32)
    for j in range(n_stream):
        if j + ahead < n_stream:
            start(b, j + ahead)
        else:
            @pl.when(b + 1 < n_seq)
            def _():
                start(b + 1, j + ahead - n_stream)
        for cp in chunk_copies(b, j, False):
            cp.wait()
        cols = slice((j % n_chunks) * keys, (j % n_chunks + 1) * keys)
        if j < n_chunks:
            kt = chunk(j)
            s_ref[:, cols] = jnp.dot(qexp, kt.astype(BF16), preferred_element_type=F32)
            for n in range(blocks_per_chunk):
                blk_sum = jnp.sum(kt[:, n * MOBA_BLOCK:(n + 1) * MOBA_BLOCK], axis=1, keepdims=True)
                ksum = jnp.where(blk_lane == j * blocks_per_chunk + n, blk_sum, ksum)
        if j == n_chunks - 1:
            gate = _dot_split(qexp32, ksum * (1.0 / MOBA_BLOCK))
            sel_bias = jnp.where(_top3_mask(gate, n_blocks), 0.0, NEG_INF).astype(BF16)
            s_own = lax.dot_general(qexp, kn_ref[...].astype(BF16), NT_DIMS, preferred_element_type=F32)
            qi = lax.bitwise_and(lax.broadcasted_iota(jnp.int32, s_own.shape, 0), nq - 1)
            kj = lax.broadcasted_iota(jnp.int32, s_own.shape, 1)
            s_own = jnp.where(kj <= qi, s_own, NEG_INF)
            m_tile = jnp.full((N_HEADS * nq, MOBA_BLOCK), NEG_INF, F32)
            for c in range(n_chunks):
                cc = slice(c * keys, (c + 1) * keys)
                sm = s_ref[:, cc] + jnp.dot(sel_bias, expand_ref[:, cc], preferred_element_type=F32)
                s_ref[:, cc] = sm
                for n in range(blocks_per_chunk):
                    m_tile = jnp.maximum(m_tile, sm[:, n * MOBA_BLOCK:(n + 1) * MOBA_BLOCK])
            m = jnp.maximum(jnp.max(m_tile, axis=1, keepdims=True), jnp.max(s_own, axis=1, keepdims=True))
            p_own = jnp.exp2(s_own - m)
            l_own = jnp.sum(p_own, axis=1, keepdims=True)
            acc = jnp.dot(p_own.astype(BF16), vn_ref[...].astype(BF16), preferred_element_type=F32)
            l_tile = jnp.zeros((N_HEADS * nq, MOBA_BLOCK), F32)
        if j >= n_chunks:
            p = jnp.exp2(s_ref[:, cols] - m)
            for n in range(blocks_per_chunk):
                l_tile = l_tile + p[:, n * MOBA_BLOCK:(n + 1) * MOBA_BLOCK]
            acc = acc + lax.dot_general(p.astype(BF16), chunk(j).astype(BF16), NT_DIMS,
                                        preferred_element_type=F32)
    o_ref[0] = acc / (l_own + jnp.sum(l_tile, axis=1, keepdims=True))


def _moba_sample(qexp, k, v, pool_k, pool_v, page_table, row0, nq):
    db, n_pages = page_table.shape
    n_pool = pool_k.shape[0]
    pages = PAGES_PER_STEP
    n_chunks = n_pages // pages
    rows = N_HEADS * nq
    pk = pool_k.transpose(0, 2, 3, 1).reshape(n_pool, KV_DIM, PAGE_SIZE)
    pv = pool_v.transpose(0, 2, 3, 1).reshape(n_pool, KV_DIM, PAGE_SIZE)
    n_keys = n_pages * PAGE_SIZE
    expand = (jnp.arange(LANES)[:, None] == (jnp.arange(n_keys) // MOBA_BLOCK)[None, :]).astype(BF16)
    new = lambda b, pt: (row0 // nq + b, 0)
    grid_spec = pltpu.PrefetchScalarGridSpec(
        num_scalar_prefetch=1,
        grid=(db,),
        in_specs=[pl.BlockSpec((1, rows, KV_DIM), lambda b, pt: (b, 0, 0)),
                  pl.BlockSpec((nq, KV_DIM), new), pl.BlockSpec((nq, KV_DIM), new),
                  pl.BlockSpec((LANES, n_keys), lambda b, pt: (0, 0)),
                  pl.BlockSpec(memory_space=pl.ANY), pl.BlockSpec(memory_space=pl.ANY)],
        out_specs=pl.BlockSpec((1, rows, KV_DIM), lambda b, pt: (b, 0, 0)),
        scratch_shapes=[pltpu.VMEM((PAGE_SLOTS, pages, KV_DIM, PAGE_SIZE), F32),
                        pltpu.VMEM((rows, n_keys), F32),
                        pltpu.SemaphoreType.DMA((PAGE_SLOTS,))])
    return pl.pallas_call(
        functools.partial(_moba_sample_kernel, nq, n_chunks, db),
        grid_spec=grid_spec,
        out_shape=jax.ShapeDtypeStruct((db, rows, KV_DIM), F32),
        compiler_params=_params("arbitrary"), name="moba_sample",
    )(page_table, qexp, k, v, expand, pk, pv)


def _proj_route_kernel(n_first_o, n_first_x, *refs):
    refs = list(refs)
    o = _read_row_parts(refs, n_first_o)
    x = _read_row_parts(refs, n_first_x)
    wo_ref, bo_ref, g_ref, wr_ref, br_ref, x1_ref, h_ref, route_ref = refs
    x1 = x + jnp.dot(o, wo_ref[...], preferred_element_type=F32) + bo_ref[...]
    x1_ref[...] = x1
    h = _rms(x1, g_ref[...])
    _store_token_tiles(h_ref, h)
    h_hi = h.astype(BF16)
    h_lo = (h - h_hi.astype(F32)).astype(BF16)
    lg = (jnp.dot(h_hi, wr_ref[0], preferred_element_type=F32) + jnp.dot(h_hi, wr_ref[1], preferred_element_type=F32)
          + jnp.dot(h_lo, wr_ref[0], preferred_element_type=F32) + br_ref[...])
    lane = lax.broadcasted_iota(jnp.int32, lg.shape, 1)
    lane_f = lane.astype(F32)
    is_group = lane < N_GROUPS
    gl = jnp.where(is_group, lg, NEG_INF)
    gmax = jnp.max(gl, axis=1, keepdims=True)
    grp = jnp.min(jnp.where(gl == gmax, lane_f, 1e9), axis=1, keepdims=True)
    g_w = 1.0 / jnp.sum(jnp.where(is_group, jnp.exp(gl - gmax), 0.0), axis=1, keepdims=True)
    lane_grp = lax.shift_right_logical(lane - N_GROUPS, 3).astype(F32)
    in_grp = (lane >= N_GROUPS) & (lane < N_GROUPS + N_EXPERTS) & (lane_grp == grp)
    el = jnp.where(in_grp, lg, NEG_INF)
    v1 = jnp.max(el, axis=1, keepdims=True)
    i1 = jnp.min(jnp.where(el == v1, lane_f, 1e9), axis=1, keepdims=True)
    el2 = jnp.where(lane_f == i1, BELOW_NEG_INF, el)
    v2 = jnp.max(el2, axis=1, keepdims=True)
    i2 = jnp.min(jnp.where(el2 == v2, lane_f, 1e9), axis=1, keepdims=True)
    e2w = jnp.exp(v2 - v1)
    w1 = g_w / (1.0 + e2w)
    w2 = g_w * e2w / (1.0 + e2w)
    route = jnp.where(lane == 0, i1 - N_GROUPS, 0.0)
    route = jnp.where(lane == 1, i2 - N_GROUPS, route)
    route = jnp.where(lane == 2, w1, route)
    route = jnp.where(lane == 3, w2, route)
    route_ref[...] = route


def _proj_route(o, x, wo, bo, g, w_gr, b_gr, w_er, b_er):
    o_args, n_first_o = _row_parts(o)
    x_args, n_first_x = _row_parts(x)
    t = sum(a.shape[0] for a in x_args)
    row = lambda i: (i, 0)
    fixed = lambda i: (0, 0)
    pad = LANES - N_GROUPS - N_EXPERTS
    wr = jnp.concatenate([w_gr, w_er, jnp.zeros((D_MODEL, pad), F32)], axis=1)
    wr_hi = wr.astype(BF16)
    wr = jnp.stack([wr_hi, (wr - wr_hi.astype(F32)).astype(BF16)])
    br = jnp.concatenate([b_gr, b_er, jnp.zeros((pad,), F32)]).reshape(1, LANES)
    return pl.pallas_call(
        functools.partial(_proj_route_kernel, n_first_o, n_first_x),
        grid=(t // ROW_TILE,),
        in_specs=_row_part_specs(Q_DIM, n_first_o) + _row_part_specs(D_MODEL, n_first_x)
        + [pl.BlockSpec((Q_DIM, D_MODEL), fixed), pl.BlockSpec((1, D_MODEL), fixed),
           pl.BlockSpec((1, D_MODEL), fixed), pl.BlockSpec((2, D_MODEL, LANES), lambda i: (0, 0, 0)),
           pl.BlockSpec((1, LANES), fixed)],
        out_specs=[pl.BlockSpec((ROW_TILE, D_MODEL), row), pl.BlockSpec((ROW_TILE * TOKEN_TILE, LANES), row),
                   pl.BlockSpec((ROW_TILE, LANES), row)],
        out_shape=[jax.ShapeDtypeStruct((t, D_MODEL), F32), jax.ShapeDtypeStruct((t * TOKEN_TILE, LANES), F32),
                   jax.ShapeDtypeStruct((t, LANES), F32)],
        compiler_params=_params("arbitrary"), name="proj_route",
    )(*o_args, *x_args, wo.astype(BF16), bo.reshape(1, D_MODEL), g.reshape(1, D_MODEL), wr, br)


def _expert_kernel(n_steps, blk_e_ref, nvalid_ref, src_ref, src_next_ref, dst_ref, wrow_ref, h_hbm,
                   wg_ref, wu_ref, wd_ref, y_hbm, xbuf, ybuf, wg_bf, wu_bf, wd_bf, gsem, ssem):
    i = pl.program_id(0)
    slot = lax.rem(i, 2)
    rows = EXPERT_ROWS
    nvalid = nvalid_ref[i]
    nvalid_next = jnp.where(i + 1 < n_steps, nvalid_ref[jnp.minimum(i + 1, n_steps - 1)], 0)
    nvalid_prev2 = jnp.where(i >= 2, nvalid_ref[jnp.maximum(i - 2, 0)], 0)

    def tile(ref, first):
        return ref.at[pl.ds(pl.multiple_of(first, TOKEN_TILE), TOKEN_TILE)]

    def row_copy_in(idx_ref, r, s):
        return pltpu.make_async_copy(tile(h_hbm, idx_ref[0, r]), tile(xbuf.at[s], r * TOKEN_TILE), gsem.at[s])

    def row_copy_out(r, s):
        return pltpu.make_async_copy(tile(ybuf.at[s], r * TOKEN_TILE), tile(y_hbm, dst_ref[0, r]), ssem.at[s])

    def for_rows(fn, n=None):
        def body(g, carry):
            for u in range(8):
                r = g * 8 + u
                if n is None:
                    fn(r, u)
                else:
                    pl.when(r < n)(functools.partial(fn, r, u))
            return carry
        lax.fori_loop(0, rows // 8, body, 0)

    @pl.when((i == 0) & (nvalid > 0))
    def _():
        for_rows(lambda r, u: row_copy_in(src_ref, r, 0).start(priority=u % 2))

    @pl.when(nvalid_next > 0)
    def _():
        for_rows(lambda r, u: row_copy_in(src_next_ref, r, 1 - slot).start(priority=u % 2))

    @pl.when(nvalid_prev2 > 0)
    def _():
        for_rows(lambda r, u: row_copy_out(r, slot).wait(), nvalid_prev2)

    @pl.when(nvalid > 0)
    def _():
        e_changed = jnp.logical_or(i == 0, blk_e_ref[i] != blk_e_ref[jnp.maximum(i - 1, 0)])

        @pl.when(e_changed)
        def _():
            wg_bf[...] = wg_ref[0, 0].astype(BF16)
            wu_bf[...] = wu_ref[0, 0].astype(BF16)
            wd_bf[...] = wd_ref[0, 0].astype(BF16)

        for_rows(lambda r, u: row_copy_in(src_ref, r, slot).wait())
        x = _load_token_tiles(xbuf.at[slot], rows).astype(BF16)
        gate = jnp.dot(x, wg_bf[...], preferred_element_type=F32)
        up = jnp.dot(x, wu_bf[...], preferred_element_type=F32)
        hid = (gate * jax.nn.sigmoid(gate) * up).astype(BF16)
        y = jnp.dot(hid, wd_bf[...], preferred_element_type=F32)
        _store_token_tiles(ybuf.at[slot], y * wrow_ref[...])
        for_rows(lambda r, u: row_copy_out(r, slot).start(priority=u % 2), nvalid)


def _expert_ffn(h, e_idx, e_w, w_gate, w_up, w_down, layer):
    t = h.shape[0] // TOKEN_TILE
    a = t * EXPERT_TOPK
    rows = EXPERT_ROWS
    n_blk = -(-a // rows) + N_EXPERTS
    n_steps = n_blk + 2
    n_rows = n_steps * rows
    flat_e = e_idx.reshape(a)
    order = jnp.argsort(flat_e).astype(jnp.int32)
    counts = jnp.sum(flat_e[:, None] == jnp.arange(N_EXPERTS, dtype=jnp.int32)[None, :], axis=0, dtype=jnp.int32)
    padded = (counts + rows - 1) // rows * rows
    pad_end = jnp.cumsum(padded)
    pad_start = pad_end - padded
    start = jnp.cumsum(counts) - counts
    blk_first = jnp.arange(n_steps, dtype=jnp.int32) * rows
    blk_e = jnp.minimum(jnp.sum(blk_first[:, None] >= pad_end[None, :], axis=1), N_EXPERTS - 1).astype(jnp.int32)
    rank0 = blk_first - pad_start[blk_e]
    nvalid = jnp.clip(counts[blk_e] - rank0, 0, rows).astype(jnp.int32)
    within = jnp.arange(rows, dtype=jnp.int32)[None, :]
    valid = within < nvalid[:, None]
    assign = order[jnp.clip((start[blk_e] + rank0)[:, None] + within, 0, a - 1)]
    tok = assign // EXPERT_TOPK
    src = jnp.where(valid, tok, 0) * TOKEN_TILE
    dst = jnp.where(valid, (assign % EXPERT_TOPK) * t + tok, 0) * TOKEN_TILE
    wrow = jnp.where(valid, e_w.reshape(a)[assign], 0.0)
    idx_blk = lambda i, be, na: (i, 0, 0)
    idx_next = lambda i, be, na: (jnp.minimum(i + 1, n_steps - 1), 0, 0)
    w_in = lambda i, be, na: (layer, be[i], 0, 0)
    grid_spec = pltpu.PrefetchScalarGridSpec(
        num_scalar_prefetch=2,
        grid=(n_steps,),
        in_specs=[pl.BlockSpec((1, 1, rows), idx_blk, memory_space=pltpu.SMEM),
                  pl.BlockSpec((1, 1, rows), idx_next, memory_space=pltpu.SMEM),
                  pl.BlockSpec((1, 1, rows), idx_blk, memory_space=pltpu.SMEM),
                  pl.BlockSpec((rows, 1), lambda i, be, na: (i, 0)),
                  pl.BlockSpec(memory_space=pl.ANY),
                  pl.BlockSpec((1, 1, D_MODEL, EXPERT_FF), w_in), pl.BlockSpec((1, 1, D_MODEL, EXPERT_FF), w_in),
                  pl.BlockSpec((1, 1, EXPERT_FF, D_MODEL), w_in)],
        out_specs=pl.BlockSpec(memory_space=pl.ANY),
        scratch_shapes=[pltpu.VMEM((2, rows * TOKEN_TILE, LANES), F32), pltpu.VMEM((2, rows * TOKEN_TILE, LANES), F32),
                        pltpu.VMEM((D_MODEL, EXPERT_FF), BF16), pltpu.VMEM((D_MODEL, EXPERT_FF), BF16),
                        pltpu.VMEM((EXPERT_FF, D_MODEL), BF16),
                        pltpu.SemaphoreType.DMA((2,)), pltpu.SemaphoreType.DMA((2,))])

    def body(blk_e_ref, nvalid_ref, src_ref, src_next_ref, dst_ref, *rest):
        _expert_kernel(n_steps, blk_e_ref, nvalid_ref, src_ref.at[0], src_next_ref.at[0], dst_ref.at[0], *rest)

    return pl.pallas_call(
        body, grid_spec=grid_spec,
        out_shape=jax.ShapeDtypeStruct((EXPERT_TOPK * t * TOKEN_TILE, LANES), F32),
        compiler_params=_params("arbitrary"), name="expert_ffn",
    )(blk_e, nvalid, src.reshape(n_steps, 1, rows), src.reshape(n_steps, 1, rows),
      dst.reshape(n_steps, 1, rows), wrow.reshape(n_rows, 1), h, w_gate, w_up, w_down)


def _final_kernel(x_ref, y0_ref, y1_ref, g_ref, o_ref):
    y = _load_token_tiles(y0_ref, ROW_TILE) + _load_token_tiles(y1_ref, ROW_TILE)
    o_ref[...] = _rms(x_ref[...] + y, g_ref[...])


def _final_norm(x, y2, g, row0, n_rows):
    t = x.shape[0]
    off = row0 // ROW_TILE
    return pl.pallas_call(
        _final_kernel,
        grid=(n_rows // ROW_TILE,),
        in_specs=[pl.BlockSpec((ROW_TILE, D_MODEL), lambda i: (i + off, 0)),
                  pl.BlockSpec((ROW_TILE * TOKEN_TILE, LANES), lambda i: (i + off, 0)),
                  pl.BlockSpec((ROW_TILE * TOKEN_TILE, LANES), lambda i: (i + off + t // ROW_TILE, 0)),
                  pl.BlockSpec((1, D_MODEL), lambda i: (0, 0))],
        out_specs=pl.BlockSpec((ROW_TILE, D_MODEL), lambda i: (i, 0)),
        out_shape=jax.ShapeDtypeStruct((n_rows, D_MODEL), F32),
        compiler_params=_params("arbitrary"), name="final_norm",
    )(x, y2, y2, g.reshape(1, D_MODEL))


def _rope_tables(pos):
    half = HEAD_DIM // 2
    inv = ROPE_THETA ** (-jnp.arange(half, dtype=F32) / half)
    ang = pos.astype(F32)[:, None] * inv[None, :]
    cos = jnp.cos(ang)
    sin = jnp.sin(ang)
    reps = LANES // HEAD_DIM
    return jnp.tile(jnp.concatenate([cos, cos], axis=1), (1, reps)), jnp.tile(jnp.concatenate([-sin, sin], axis=1), (1, reps))


def _moe(o, x, wo, bo, g, w_gr, b_gr, w_er, b_er, w_gate, w_up, w_down, layer):
    x1, h, route = _proj_route(o, x, wo, bo, g, w_gr, b_gr, w_er, b_er)
    e_idx = jnp.clip(route[:, :EXPERT_TOPK].astype(jnp.int32), 0, N_EXPERTS - 1)
    e_w = route[:, EXPERT_TOPK:2 * EXPERT_TOPK]
    return x1, _expert_ffn(h, e_idx, e_w, w_gate, w_up, w_down, layer)


def kernel(x_prompt, x_sample, cache_swa_k, cache_swa_v, cache_moba_k, cache_moba_v, page_table, norm_mix, norm_ffn, norm_final, swa_w_qkv, swa_b_qkv, swa_sinks, swa_w_o, swa_b_o, moba_w_qkv, moba_w_o, w_group_router, b_group_router, w_expert_router, b_expert_router, w_gate, w_up, w_down):
    batch, seq, _ = x_prompt.shape
    db, nq, _ = x_sample.shape
    past_len = page_table.shape[1] * PAGE_SIZE
    tp = batch * seq
    ts = db * nq
    t = tp + ts
    x = (x_prompt.reshape(tp, D_MODEL), x_sample.reshape(ts, D_MODEL))
    pos =jnp.concatenate([jnp.tile(jnp.arange(seq), batch), jnp.tile(past_len + jnp.arange(nq), db)])
    cos_t, sin_t = _rope_tables(pos)

    q, k0, v0, k_rep, _, qt, vt = _qkv_rope(x, [], norm_mix[0], swa_w_qkv[0], swa_b_qkv[0], cos_t, sin_t)
    o_p = _swa_prompt(qt, k_rep, vt, swa_sinks[0], batch, seq)
    o_s, swa_ks, swa_vs = _swa_sample(q, k0, v0, cache_swa_k[0], cache_swa_v[0], swa_sinks[0], tp, nq)
    x1, y2 = _moe((o_p, o_s), x, swa_w_o[0], swa_b_o[0], norm_ffn[0], w_group_router[0], b_group_router[0],
                  w_expert_router[0], b_expert_router[0], w_gate, w_up, w_down, 0)

    x, q, k1, v1, k_rep, kmean, qt, vt = _qkv_rope(
        x1, [(y2, 0), (y2, t)], norm_mix[1], moba_w_qkv[0], jnp.zeros((QKV_DIM,), F32), cos_t, sin_t)
    nb = seq // MOBA_BLOCK
    kmean_rep = jnp.tile(kmean[:batch * nb, 0].reshape(batch, nb, N_KV_HEADS, 1, HEAD_DIM), (1, 1, 1, GROUP, 1))
    kmean_rep = kmean_rep.reshape(batch, nb, N_KV_HEADS, GROUP_LANES).transpose(0, 2, 1, 3)
    o_p = _moba_prompt(qt, k_rep, vt, kmean_rep, batch, seq)
    qs = q[tp:].reshape(db, nq, N_HEADS, 1, HEAD_DIM)
    slot = (jnp.arange(N_HEADS) // GROUP)[:, None] == jnp.arange(N_KV_HEADS)[None, :]
    qexp = jnp.where(slot[None, None, :, :, None], qs, 0.0).transpose(0, 2, 1, 3, 4).reshape(db, N_HEADS * nq, KV_DIM)
    o_s = _moba_sample(qexp, k1, v1, cache_moba_k[0], cache_moba_v[0], page_table, tp, nq)
    o_s = o_s.reshape(db, N_HEADS, nq, N_KV_HEADS, HEAD_DIM)
    o_s = jnp.sum(jnp.where(slot[None, :, None, :, None], o_s, 0.0), axis=3).transpose(0, 2, 1, 3)
    x1, y2 = _moe((o_p, o_s.reshape(ts, Q_DIM).astype(BF16)), x, moba_w_o[0], jnp.zeros((D_MODEL,), F32), norm_ffn[1], w_group_router[1],
                  b_group_router[1], w_expert_router[1], b_expert_router[1], w_gate, w_up, w_down, 1)

    y_prompt = _final_norm(x1, y2, norm_final, 0, tp).reshape(batch, seq, D_MODEL)
    y_sample = _final_norm(x1, y2, norm_final, tp, ts).reshape(db, nq, D_MODEL)
    win = min(WINDOW, seq)
    kv5 = lambda a, b, s: a.reshape(1, b, s, N_KV_HEADS, HEAD_DIM)
    k0p = k0[:tp].reshape(batch, seq, KV_DIM)[:, seq - win:]
    v0p = v0[:tp].reshape(batch, seq, KV_DIM)[:, seq - win:]
    return (y_prompt, y_sample, kv5(k0p, batch, win), kv5(v0p, batch, win),
            kv5(swa_ks, db, swa_ks.shape[1]), kv5(swa_vs, db, swa_vs.shape[1]),
            kv5(k1[:tp], batch, seq), kv5(v1[:tp], batch, seq), kv5(k1[tp:], db, nq), kv5(v1[tp:], db, nq))
```
